```python
import math
import jax, jax.numpy as jnp
from jax import lax
import numpy as np

D_MODEL = 1024
BATCH = 8
SEQ = 2048
DEPTH = 2
DEC_BATCH = 128
DEC_SEQ = 8
PAST_LEN = 16384
PAGE_SIZE = 128

EPS = 1e-6
D_FF = 2816
C_POOL = D_MODEL // 4
C_GLA = D_MODEL // 4
C_HGRN = D_MODEL // 4
C_SSM = D_MODEL // 4
POOL_WINDOWS = (2, 4, 8, 16)
POOL_GC = C_POOL // 4
POOL_PAST = 16 - 1
GLA_H = 4
GLA_DV = C_GLA // GLA_H
GLA_DK = GLA_DV // 2
GLA_RANK = 16
GLA_TAU = 16.0
GLA_CHUNK = 64
HG_H = 4
HG_DV = C_HGRN // HG_H
HG_DK = 64
HG_CHUNK = 64
SSM_P = 64
SSM_H = C_SSM // SSM_P
SSM_G = 2
SSM_HG = SSM_H // SSM_G
SSM_N = 128
SSM_CONV = 4
SSM_CONV_DIM = C_SSM + 2 * SSM_G * SSM_N
SSM_CHUNK = 64
IN_SIZES = (C_POOL,
            GLA_H * GLA_DK, GLA_H * GLA_DK, C_GLA, C_GLA, GLA_RANK,
            HG_H * HG_DK, HG_H * HG_DK, C_HGRN, C_HGRN,
            C_SSM, SSM_CONV_DIM, SSM_H)
N_IN = sum(IN_SIZES)

kernel_name = 'hybrid_pool_gla_hgrn2_ssd_macaron_step'


def rms_norm(x, g, eps=EPS):
    xf = x.astype(jnp.float32)
    y = xf * lax.rsqrt(jnp.mean(xf * xf, axis=-1, keepdims=True) + eps)
    return (y * g.astype(jnp.float32)).astype(x.dtype)


def swiglu(x, w_gate, w_up, w_down):
    return (jax.nn.silu(x @ w_gate) * (x @ w_up)) @ w_down


def _pad_time(a, t_pad):
    return jnp.pad(a, [(0, 0), (0, t_pad - a.shape[1])] + [(0, 0)] * (a.ndim - 2))


def pool_mixer(xp, buf, pos0, pool_w, pool_scale):
    B, T, C = xp.shape
    xe = jnp.concatenate([buf.astype(xp.dtype), xp], axis=1)
    cs = jnp.cumsum(xe.astype(jnp.float32), axis=1)
    cs = jnp.concatenate([jnp.zeros((B, 1, C), jnp.float32), cs], axis=1)
    pos = pos0 + jnp.arange(T)
    upper = cs[:, POOL_PAST + 1:POOL_PAST + 1 + T]
    diffs = []
    for gi, w in enumerate(POOL_WINDOWS):
        sl = slice(gi * POOL_GC, (gi + 1) * POOL_GC)
        lower = cs[:, POOL_PAST + 1 - w:POOL_PAST + 1 - w + T, sl]
        cnt = jnp.minimum(pos + 1, w).astype(jnp.float32)[None, :, None]
        diffs.append(((upper[..., sl] - lower) / cnt).astype(xp.dtype) - xp[..., sl])
    d = jnp.stack(diffs, axis=2)
    y = jnp.einsum('btgc,gcd->btgd', d, pool_w).reshape(B, T, C) * pool_scale
    return y, xe[:, -POOL_PAST:]


def chunked_gla(q, k, v, log_a, s0, chunk):
    B, T, H, K = q.shape
    V = v.shape[-1]
    L = min(chunk, T)
    n = -(-T // L)
    Tp = n * L

    def prep(a):
        a = _pad_time(a.astype(jnp.float32), Tp)
        return a.reshape((B, n, L) + a.shape[2:]).swapaxes(0, 1)

    causal = jnp.tril(jnp.ones((L, L), dtype=bool))[None, :, :, None, None]

    def step(S, inp):
        qc, kc, vc, gc = inp
        b = jnp.cumsum(gc, axis=1)
        o = jnp.einsum('blhk,bhkv->blhv', qc * jnp.exp(b), S)
        dec = jnp.exp(jnp.where(causal, b[:, :, None] - b[:, None, :], -jnp.inf))
        att = jnp.einsum('bihk,bjhk,bijhk->bijh', qc, kc, dec)
        o = o + jnp.einsum('bijh,bjhv->bihv', att, vc)
        w_end = jnp.exp(b[:, -1:] - b)
        S = S * jnp.exp(b[:, -1])[..., None] + jnp.einsum('bjhk,bjhv->bhkv', kc * w_end, vc)
        return S, o

    S, o = lax.scan(step, s0.astype(jnp.float32), (prep(q), prep(k), prep(v), prep(log_a)))
    o = o.swapaxes(0, 1).reshape(B, Tp, H, V)[:, :T]
    return o.astype(q.dtype), S.astype(s0.dtype)


def chunked_ssd(x, dt, A, Bm, Cm, s0, chunk):
    B, T, G, Hg, P = x.shape
    L = min(chunk, T)
    n = -(-T // L)
    Tp = n * L
    dt = dt.astype(jnp.float32)
    log_a = dt * A.astype(jnp.float32)
    xdt = x.astype(jnp.float32) * dt[..., None]

    def prep(a):
        a = _pad_time(a.astype(jnp.float32), Tp)
        return a.reshape((B, n, L) + a.shape[2:]).swapaxes(0, 1)

    causal = jnp.tril(jnp.ones((L, L), dtype=bool))[None, :, :, None, None]

    def step(S, inp):
        xc, ac, bc, cc = inp
        cum = jnp.cumsum(ac, axis=1)
        decay = jnp.exp(jnp.where(causal, cum[:, :, None] - cum[:, None, :], -jnp.inf))
        cb = jnp.einsum('bign,bjgn->bijg', cc, bc)
        y = jnp.einsum('bijg,bijgh,bjghp->bighp', cb, decay, xc)
        y = y + jnp.einsum('bign,bghpn->bighp', cc, S) * jnp.exp(cum)[..., None]
        w_end = jnp.exp(cum[:, -1:] - cum)
        S = S * jnp.exp(cum[:, -1])[..., None, None] + jnp.einsum('bjgn,bjgh,bjghp->bghpn', bc, w_end, xc)
        return S, y

    S, y = lax.scan(step, s0.astype(jnp.float32), (prep(xdt), prep(log_a), prep(Bm), prep(Cm)))
    y = y.swapaxes(0, 1).reshape(B, Tp, G, Hg, P)[:, :T]
    return y, S.astype(s0.dtype)


def run_layer(x, pool_buf, gla_s, hgrn_s, ssm_s, conv_buf, pos0, lb,
              ffn1_norm, ffn1_w_gate, ffn1_w_up, ffn1_w_down, mix_norm, w_in, pool_w, pool_scale,
              gla_w_gate, gla_gate_bias, gla_norm, hgrn_norm, ssm_conv_w, ssm_conv_b, ssm_dt_bias,
              ssm_A_log, ssm_D, ssm_norm, w_out, ffn2_norm, ffn2_w_gate, ffn2_w_up, ffn2_w_down):
    B, T, _ = x.shape
    x = x + 0.5 * swiglu(rms_norm(x, ffn1_norm), ffn1_w_gate, ffn1_w_up, ffn1_w_down)
    h = rms_norm(x, mix_norm)
    proj = h @ w_in
    split_at = np.cumsum(IN_SIZES)[:-1].tolist()
    (p_x, g_q, g_k, g_v, g_r, g_lr, r_q, r_f, r_i, r_g,
     s_z, s_xbc, s_dt) = jnp.split(proj, split_at, axis=-1)

    o_pool, new_pool = pool_mixer(p_x, pool_buf, pos0, pool_w, pool_scale)

    q = g_q.reshape(B, T, GLA_H, GLA_DK) * (GLA_DK ** -0.5)
    k = g_k.reshape(B, T, GLA_H, GLA_DK)
    v = g_v.reshape(B, T, GLA_H, GLA_DV)
    gate_logit = (g_lr @ gla_w_gate + gla_gate_bias).astype(jnp.float32)
    log_alpha = (jax.nn.log_sigmoid(gate_logit) / GLA_TAU).reshape(B, T, GLA_H, GLA_DK)
    o, new_gla = chunked_gla(q, k, v, log_alpha, gla_s, GLA_CHUNK)
    o_gla = (rms_norm(o, gla_norm) * jax.nn.silu(g_r.reshape(B, T, GLA_H, GLA_DV))).reshape(B, T, C_GLA)

    hq = jax.nn.silu(r_q).reshape(B, T, HG_H, HG_DK)
    lbh = lb.reshape(HG_H, HG_DK)
    zf = r_f.astype(jnp.float32).reshape(B, T, HG_H, HG_DK)
    log_f = jnp.logaddexp(jnp.log(lbh), jnp.log1p(-lbh) + jax.nn.log_sigmoid(zf))
    hk = -jnp.expm1(log_f)
    hi = r_i.reshape(B, T, HG_H, HG_DV)
    o, new_hgrn = chunked_gla(hq, hk, hi, log_f, hgrn_s, HG_CHUNK)
    o_hgrn = (rms_norm(o, hgrn_norm) * jax.nn.silu(r_g.reshape(B, T, HG_H, HG_DV))).reshape(B, T, C_HGRN)

    xe = jnp.concatenate([conv_buf.astype(s_xbc.dtype), s_xbc], axis=1)
    conv = lax.conv_general_dilated(xe, ssm_conv_w[:, None, :].astype(xe.dtype), window_strides=(1,),
                                    padding='VALID', dimension_numbers=('NWC', 'WIO', 'NWC'),
                                    feature_group_count=SSM_CONV_DIM)
    conv = jax.nn.silu(conv + ssm_conv_b)
    new_conv = xe[:, -(SSM_CONV - 1):]
    xs, Bm, Cm = jnp.split(conv, [C_SSM, C_SSM + SSM_G * SSM_N], axis=-1)
    xs = xs.reshape(B, T, SSM_G, SSM_HG, SSM_P)
    Bm = Bm.reshape(B, T, SSM_G, SSM_N)
    Cm = Cm.reshape(B, T, SSM_G, SSM_N)
    dt = jax.nn.softplus(s_dt.astype(jnp.float32) + ssm_dt_bias.astype(jnp.float32)).reshape(B, T, SSM_G, SSM_HG)
    A = -jnp.exp(ssm_A_log.astype(jnp.float32)).reshape(SSM_G, SSM_HG)
    y, new_ssm = chunked_ssd(xs, dt, A, Bm, Cm, ssm_s.reshape(B, SSM_G, SSM_HG, SSM_P, SSM_N), SSM_CHUNK)
    y = y + ssm_D.astype(jnp.float32).reshape(SSM_G, SSM_HG, 1) * xs.astype(jnp.float32)
    y = y.reshape(B, T, C_SSM).astype(x.dtype) * jax.nn.silu(s_z)
    o_ssm = rms_norm(y.reshape(B, T, SSM_G, C_SSM // SSM_G),
                     ssm_norm.reshape(SSM_G, C_SSM // SSM_G)).reshape(B, T, C_SSM)
    new_ssm = new_ssm.reshape(B, SSM_H, SSM_P, SSM_N)

    mix = jnp.concatenate([o_pool, o_gla, o_hgrn, o_ssm], axis=-1) @ w_out
    x = x + mix
    x = x + 0.5 * swiglu(rms_norm(x, ffn2_norm), ffn2_w_gate, ffn2_w_up, ffn2_w_down)
    return x, new_pool, new_gla, new_hgrn, new_ssm, new_conv


def setup_inputs(seed: int = 0) -> dict:
    key = jax.random.key(seed)
    ks = list(jax.random.split(key, 48))

    def nrm(shape, s):
        return jax.random.normal(ks.pop(), shape, jnp.float32) * s

    def gain(shape):
        return 1.0 + nrm(shape, 0.02)

    inp = {}
    inp['x_prompt'] = nrm((BATCH, SEQ, D_MODEL), 1.0)
    inp['x_sample'] = nrm((DEC_BATCH, DEC_SEQ, D_MODEL), 1.0)
    inp['state_pool'] = nrm((DEPTH, DEC_BATCH, POOL_PAST, C_POOL), 1.0)
    inp['state_gla'] = nrm((DEPTH, DEC_BATCH, GLA_H, GLA_DK, GLA_DV), 0.5)
    inp['state_hgrn'] = nrm((DEPTH, DEC_BATCH, HG_H, HG_DK, HG_DV), 0.5)
    inp['state_ssm'] = nrm((DEPTH, DEC_BATCH, SSM_H, SSM_P, SSM_N), 0.3)
    inp['state_conv'] = nrm((DEPTH, DEC_BATCH, SSM_CONV - 1, SSM_CONV_DIM), 1.0)
    inp['ffn1_norm'] = gain((DEPTH, D_MODEL))
    inp['ffn1_w_gate'] = nrm((DEPTH, D_MODEL, D_FF), D_MODEL ** -0.5)
    inp['ffn1_w_up'] = nrm((DEPTH, D_MODEL, D_FF), D_MODEL ** -0.5)
    inp['ffn1_w_down'] = nrm((DEPTH, D_FF, D_MODEL), D_FF ** -0.5)
    inp['mix_norm'] = gain((DEPTH, D_MODEL))
    inp['w_in'] = nrm((DEPTH, D_MODEL, N_IN), D_MODEL ** -0.5)
    inp['pool_w'] = nrm((DEPTH, len(POOL_WINDOWS), POOL_GC, POOL_GC), POOL_GC ** -0.5)
    inp['pool_scale'] = gain((DEPTH, C_POOL))
    inp['gla_w_gate'] = nrm((DEPTH, GLA_RANK, GLA_H * GLA_DK), GLA_RANK ** -0.5)
    inp['gla_gate_bias'] = nrm((DEPTH, GLA_H * GLA_DK), 0.1)
    inp['gla_norm'] = gain((DEPTH, GLA_DV))
    inp['hgrn_lb_logits'] = nrm((DEPTH, HG_H * HG_DK), 0.5)
    inp['hgrn_norm'] = gain((DEPTH, HG_DV))
    inp['ssm_conv_w'] = nrm((DEPTH, SSM_CONV, SSM_CONV_DIM), 0.5)
    inp['ssm_conv_b'] = nrm((DEPTH, SSM_CONV_DIM), 0.02)
    dt0 = jnp.exp(jax.random.uniform(ks.pop(), (DEPTH, SSM_H), jnp.float32,
                                     minval=math.log(1e-3), maxval=math.log(1e-1)))
    inp['ssm_dt_bias'] = dt0 + jnp.log(-jnp.expm1(-dt0))
    inp['ssm_A_log'] = jnp.log(jax.random.uniform(ks.pop(), (DEPTH, SSM_H), jnp.float32, minval=1.0, maxval=16.0))
    inp['ssm_D'] = gain((DEPTH, SSM_H))
    inp['ssm_norm'] = gain((DEPTH, C_SSM))
    inp['w_out'] = nrm((DEPTH, D_MODEL, D_MODEL), D_MODEL ** -0.5)
    inp['ffn2_norm'] = gain((DEPTH, D_MODEL))
    inp['ffn2_w_gate'] = nrm((DEPTH, D_MODEL, D_FF), D_MODEL ** -0.5)
    inp['ffn2_w_up'] = nrm((DEPTH, D_MODEL, D_FF), D_MODEL ** -0.5)
    inp['ffn2_w_down'] = nrm((DEPTH, D_FF, D_MODEL), D_FF ** -0.5)
    inp['final_norm'] = gain((D_MODEL,))
    return inp


def reference(x_prompt, x_sample, state_pool, state_gla, state_hgrn, state_ssm, state_conv,
              ffn1_norm, ffn1_w_gate, ffn1_w_up, ffn1_w_down, mix_norm, w_in, pool_w, pool_scale,
              gla_w_gate, gla_gate_bias, gla_norm, hgrn_lb_logits, hgrn_norm,
              ssm_conv_w, ssm_conv_b, ssm_dt_bias, ssm_A_log, ssm_D, ssm_norm, w_out,
              ffn2_norm, ffn2_w_gate, ffn2_w_up, ffn2_w_down, final_norm):
    lb_cum = jnp.cumsum(jax.nn.softmax(hgrn_lb_logits.astype(jnp.float32), axis=0), axis=0)
    lower_bounds = lb_cum - lb_cum[0:1]

    bp = x_prompt.shape[0]
    hp, hs = x_prompt, x_sample
    pp, ps, gp, gs, rp, rs, sp, ss, cp, cs = ([] for _ in range(10))
    for l in range(DEPTH):
        lw = (ffn1_norm[l], ffn1_w_gate[l], ffn1_w_up[l], ffn1_w_down[l], mix_norm[l], w_in[l],
              pool_w[l], pool_scale[l], gla_w_gate[l], gla_gate_bias[l], gla_norm[l], hgrn_norm[l],
              ssm_conv_w[l], ssm_conv_b[l], ssm_dt_bias[l], ssm_A_log[l], ssm_D[l], ssm_norm[l], w_out[l],
              ffn2_norm[l], ffn2_w_gate[l], ffn2_w_up[l], ffn2_w_down[l])
        hp, a, b, c, d, e = run_layer(
            hp,
            jnp.zeros((bp, POOL_PAST, C_POOL), state_pool.dtype),
            jnp.zeros((bp, GLA_H, GLA_DK, GLA_DV), state_gla.dtype),
            jnp.zeros((bp, HG_H, HG_DK, HG_DV), state_hgrn.dtype),
            jnp.zeros((bp, SSM_H, SSM_P, SSM_N), state_ssm.dtype),
            jnp.zeros((bp, SSM_CONV - 1, SSM_CONV_DIM), state_conv.dtype),
            0, lower_bounds[l], *lw)
        pp.append(a); gp.append(b); rp.append(c); sp.append(d); cp.append(e)
        hs, a, b, c, d, e = run_layer(
            hs, state_pool[l], state_gla[l], state_hgrn[l], state_ssm[l], state_conv[l],
            PAST_LEN, lower_bounds[l], *lw)
        ps.append(a); gs.append(b); rs.append(c); ss.append(d); cs.append(e)

    y_prompt = rms_norm(hp, final_norm)
    y_sample = rms_norm(hs, final_norm)
    return (y_prompt, y_sample,
            jnp.stack(pp), jnp.stack(ps),
            jnp.stack(gp), jnp.stack(gs),
            jnp.stack(rp), jnp.stack(rs),
            jnp.stack(sp), jnp.stack(ss),
            jnp.stack(cp), jnp.stack(cs))
```

```python
import functools

import jax
import jax.numpy as jnp
from jax import lax
from jax.experimental import pallas as pl
from jax.experimental.pallas import tpu as pltpu

F32 = jnp.float32
BF16 = jnp.bfloat16

D_MODEL = 1024
D_FF = 2816
DEPTH = 2
EPS = 1e-6
PAST_LEN = 16384

C_MIX = 256
POOL_WINDOWS = (2, 4, 8, 16)
POOL_GC = 64
POOL_PAST = 15
GLA_H, GLA_DK, GLA_DV, GLA_RANK, GLA_TAU = 4, 32, 64, 16, 16.0
HG_H, HG_DK, HG_DV = 4, 64, 64
SSM_H, SSM_P, SSM_G, SSM_N, SSM_CONV = 4, 64, 2, 128, 4
SSM_CONV_DIM = 768
N_IN = 3092

OFF_PX, OFF_GQ, OFF_GK, OFF_GV, OFF_GR = 0, 256, 384, 512, 768
OFF_RQ, OFF_RF, OFF_RI, OFF_RG = 1024, 1280, 1536, 1792
OFF_SZ, OFF_XBC = 2048, 2304
OFF_LR = 3072
OFF_DTX = 3200
W_PROJ = 3456

VMEM_LIMIT_BYTES = 56 * 1024 * 1024
SUBLANES = 8

ROW_TILE = 512
PROMPT_CHUNK = 64
SUB_BLOCK = 16


def _sigmoid(x):
    return 1.0 / (1.0 + jnp.exp(-x))


def _silu(x):
    return x * _sigmoid(x)


def _log_sigmoid(x):
    return jnp.minimum(x, 0.0) - jnp.log(1.0 + jnp.exp(-jnp.abs(x)))


def _softplus(x):
    return jnp.maximum(x, 0.0) + jnp.log(1.0 + jnp.exp(-jnp.abs(x)))


def _rms(x, g):
    ms = jnp.mean(x * x, axis=-1, keepdims=True)
    return x * lax.rsqrt(ms + EPS) * g


def _dot(a, b):
    return jnp.dot(a.astype(BF16), b.astype(BF16), preferred_element_type=F32)


def _dot_nt(a, b):
    return lax.dot_general(a.astype(BF16), b.astype(BF16), (((1,), (1,)), ((), ())),
                           preferred_element_type=F32)


def _dot_tn(a, b):
    return lax.dot_general(a.astype(BF16), b.astype(BF16), (((0,), (0,)), ((), ())),
                           preferred_element_type=F32)


def _split3(x):
    hi = x.astype(BF16)
    r1 = x - hi.astype(F32)
    mid = r1.astype(BF16)
    lo = (r1 - mid.astype(F32)).astype(BF16)
    return hi, mid, lo


def _dot_exact_rhs(a01, x):
    hi, mid, lo = _split3(x)
    a = a01.astype(BF16)
    return (jnp.dot(a, hi, preferred_element_type=F32) + jnp.dot(a, mid, preferred_element_type=F32)
            + jnp.dot(a, lo, preferred_element_type=F32))


def _dot_exact_lhs(x, b01):
    hi, mid, lo = _split3(x)
    b = b01.astype(BF16)
    return (jnp.dot(hi, b, preferred_element_type=F32) + jnp.dot(mid, b, preferred_element_type=F32)
            + jnp.dot(lo, b, preferred_element_type=F32))


def _swiglu_half(x, g, wg_ref, wu_ref, wd_ref, n_chunks):
    h = _rms(x, g).astype(BF16)
    ck = D_FF // n_chunks
    acc = None
    for c in range(n_chunks):
        sl = slice(c * ck, (c + 1) * ck)
        gate = jnp.dot(h, wg_ref[:, sl], preferred_element_type=F32)
        up = jnp.dot(h, wu_ref[:, sl], preferred_element_type=F32)
        act = (_silu(gate) * up).astype(BF16)
        d = jnp.dot(act, wd_ref[sl, :], preferred_element_type=F32)
        acc = d if acc is None else acc + d
    return x + 0.5 * acc


def _pre_kernel(x_ref, g1_ref, wg_ref, wu_ref, wd_ref, gm_ref, win_ref, x1_ref, proj_ref, *, n_chunks):
    x1 = _swiglu_half(x_ref[...], g1_ref[...], wg_ref, wu_ref, wd_ref, n_chunks)
    x1_ref[...] = x1
    h = _rms(x1, gm_ref[...]).astype(BF16)
    proj_ref[...] = jnp.dot(h, win_ref[...], preferred_element_type=F32)


def _post_kernel(x_ref, cat_ref, wout_ref, g2_ref, wg_ref, wu_ref, wd_ref, gf_ref, o_ref, *, n_chunks, final):
    x2 = x_ref[...] + jnp.dot(cat_ref[...].astype(BF16), wout_ref[...], preferred_element_type=F32)
    x3 = _swiglu_half(x2, g2_ref[...], wg_ref, wu_ref, wd_ref, n_chunks)
    if final:
        x3 = _rms(x3, gf_ref[...])
    o_ref[...] = x3


def _const_spec(shape):
    nd = len(shape)
    return pl.BlockSpec(shape, lambda *_: (0,) * nd, pipeline_mode=pl.Buffered(1))


def _dense_params():
    return pltpu.CompilerParams(dimension_semantics=("arbitrary",), vmem_limit_bytes=VMEM_LIMIT_BYTES)


def _pre_call(x, g1, wg, wu, wd, gm, win):
    rows = x.shape[0]
    row_spec = lambda w: pl.BlockSpec((ROW_TILE, w), lambda i: (i, 0))
    return pl.pallas_call(
        functools.partial(_pre_kernel, n_chunks=2),
        grid=(rows // ROW_TILE,),
        in_specs=[row_spec(D_MODEL), _const_spec((1, D_MODEL)), _const_spec((D_MODEL, D_FF)),
                  _const_spec((D_MODEL, D_FF)), _const_spec((D_FF, D_MODEL)), _const_spec((1, D_MODEL)),
                  _const_spec((D_MODEL, W_PROJ))],
        out_specs=[row_spec(D_MODEL), row_spec(W_PROJ)],
        out_shape=[jax.ShapeDtypeStruct((rows, D_MODEL), F32), jax.ShapeDtypeStruct((rows, W_PROJ), F32)],
        compiler_params=_dense_params(),
        name="pre_ffn_inproj",
    )(x, g1, wg, wu, wd, gm, win)


def _post_call(x, cat, wout, g2, wg, wu, wd, gf, final):
    rows = x.shape[0]
    row_spec = pl.BlockSpec((ROW_TILE, D_MODEL), lambda i: (i, 0))
    return pl.pallas_call(
        functools.partial(_post_kernel, n_chunks=2, final=final),
        grid=(rows // ROW_TILE,),
        in_specs=[row_spec, row_spec, _const_spec((D_MODEL, D_MODEL)), _const_spec((1, D_MODEL)),
                  _const_spec((D_MODEL, D_FF)), _const_spec((D_MODEL, D_FF)), _const_spec((D_FF, D_MODEL)),
                  _const_spec((1, D_MODEL))],
        out_specs=row_spec,
        out_shape=jax.ShapeDtypeStruct((rows, D_MODEL), F32),
        compiler_params=_dense_params(),
        name="post_outproj_ffn",
    )(x, cat, wout, g2, wg, wu, wd, gf)


def _lane_group(shape, width):
    return lax.broadcasted_iota(jnp.int32, shape, len(shape) - 1) // width


def _block_diag_mask(rows, row_w, cols, col_w):
    r = lax.broadcasted_iota(jnp.int32, (rows, cols), 0) // row_w
    c = lax.broadcasted_iota(jnp.int32, (rows, cols), 1) // col_w
    return r == c


def _tril(n):
    r = lax.broadcasted_iota(jnp.int32, (n, n), 0)
    c = lax.broadcasted_iota(jnp.int32, (n, n), 1)
    return r >= c


def _select_heads(stacked, n_heads, rows, head_w):
    grp = _lane_group((rows, n_heads * head_w), head_w)
    out = jnp.zeros((rows, n_heads * head_w), F32)
    for h in range(n_heads):
        out = jnp.where(grp == h, stacked[h * rows:(h + 1) * rows], out)
    return out


def _linear_attention(q, k, v, la, st_ref, n_heads, dk, chunk, sub):
    nk = n_heads * dk
    nv = n_heads * 64
    n_sub = chunk // sub
    b = _dot_exact_rhs(_tril(chunk).astype(F32), la)
    refs = [None] + [b[sub * i - 1:sub * i, :] for i in range(1, n_sub)]
    c_loc = jnp.concatenate([b[0:sub]] + [b[sub * i:sub * (i + 1)] - refs[i] for i in range(1, n_sub)], axis=0) \
        if n_sub > 1 else b
    st = st_ref[...]
    o = _dot_nt(q * jnp.exp(b), st)

    qt = q * jnp.exp(c_loc)
    kgrp = _lane_group((sub, nk), dk)
    outs = []
    for i in range(n_sub):
        rows = slice(sub * i, sub * (i + 1))
        o_i = o[rows]
        if i > 0:
            prev = slice(0, sub * i)
            kt = k[prev] * jnp.exp(refs[i] - b[prev])
            q_stack = jnp.concatenate([jnp.where(kgrp == h, qt[rows], 0.0) for h in range(n_heads)], axis=0)
            att = _dot_nt(q_stack, kt)
            o_i = o_i + _select_heads(_dot(att, v[prev]), n_heads, sub, 64)
        c_i = c_loc[rows]
        q_i = q[rows]
        row_id = lax.broadcasted_iota(jnp.int32, (sub, nk), 0)
        e_list = []
        for j in range(sub):
            c_j = c_loc[sub * i + j:sub * i + j + 1, :]
            k_j = k[sub * i + j:sub * i + j + 1, :]
            dec = jnp.exp(jnp.where(row_id >= j, c_i - c_j, -jnp.inf))
            e_list.append(dec * q_i * k_j)
        e_all = jnp.concatenate(e_list, axis=0)
        expand = _block_diag_mask(nk, dk, nv, 64).astype(F32)
        r_d = _dot(e_all, expand)
        for j in range(sub):
            o_i = o_i + r_d[j * sub:(j + 1) * sub] * v[sub * i + j:sub * i + j + 1, :]
        outs.append(o_i)
    o = jnp.concatenate(outs, axis=0) if n_sub > 1 else outs[0]

    b_last = b[chunk - 1:chunk, :]
    k_end = k * jnp.exp(b_last - b)
    upd = _dot_tn(v, k_end)
    st_ref[...] = st * jnp.exp(b_last) + jnp.where(_block_diag_mask(nv, 64, nk, dk), upd, 0.0)
    return o


def _load_state_t(s, n_heads, dk):
    st = s.T
    tiled = jnp.concatenate([st] * n_heads, axis=0)
    return jnp.where(_block_diag_mask(n_heads * 64, 64, n_heads * dk, dk), tiled, 0.0)


def _store_state_t(st, n_heads, dk):
    acc = st[0:64]
    for h in range(1, n_heads):
        acc = acc + st[64 * h:64 * (h + 1)]
    return acc.T


def _segment_rms(o, width, gain):
    n = o.shape[-1]
    seg = _block_diag_mask(n, width, n, width).astype(F32)
    ms = _dot_exact_lhs(o * o, seg) * (1.0 / width)
    return o * lax.rsqrt(ms + EPS) * gain


def _mixer_kernel(proj_ref, poolb_ref, glas_ref, hgs_ref, ssms_ref, convb_ref,
                  poolw_ref, pools_ref, wgate_ref, gbias_ref, gnorm_ref, lbl_ref, hnorm_ref,
                  convw_ref, convbias_ref, dtb_ref, alog_ref, dskip_ref, snorm_ref, cat_alias_ref,
                  cat_ref, npool_ref, ngla_ref, nhg_ref, nssm_ref, nconv_ref,
                  pool_scr, conv_scr, gla_st, hg_st, ssm_st, *, chunk, pos0, layer):
    del cat_alias_ref
    c = pl.program_id(1)
    n_c = pl.num_programs(1)
    sub = min(SUB_BLOCK, chunk)
    pool_base = 24
    conv_base = SUBLANES

    @pl.when(c == 0)
    def _():
        pool_scr[...] = jnp.zeros(pool_scr.shape, F32)
        pool_scr[0, pool_base - POOL_PAST:pool_base, :] = poolb_ref[0]
        conv_scr[...] = jnp.zeros(conv_scr.shape, F32)
        conv_scr[conv_base - (SSM_CONV - 1):conv_base, :] = convb_ref[0]
        gla_st[...] = _load_state_t(glas_ref[0], GLA_H, GLA_DK)
        hg_st[...] = _load_state_t(hgs_ref[0], HG_H, HG_DK)
        for g in range(SSM_G):
            ssm_st[g] = ssms_ref[0, g].T

    xp = proj_ref[:, OFF_PX:OFF_PX + C_MIX]
    pool_scr[0, pool_base:pool_base + chunk, :] = xp
    n_ext = pool_base - SUBLANES + chunk
    for s, shift in enumerate((1, 2, 4)):
        cur = pool_scr[s, SUBLANES:SUBLANES + n_ext, :]
        prev = pool_scr[s, SUBLANES - shift:SUBLANES - shift + n_ext, :]
        pool_scr[s + 1, SUBLANES:SUBLANES + n_ext, :] = cur + prev
    s2 = pool_scr[1, pool_base:pool_base + chunk, :]
    s4 = pool_scr[2, pool_base:pool_base + chunk, :]
    s8 = pool_scr[3, pool_base:pool_base + chunk, :]
    s16 = s8 + pool_scr[3, pool_base - 8:pool_base - 8 + chunk, :]
    grp = _lane_group((chunk, C_MIX), POOL_GC)
    win_sum = jnp.where(grp == 0, s2, jnp.where(grp == 1, s4, jnp.where(grp == 2, s8, s16)))
    win = jnp.where(grp == 0, 2, jnp.where(grp == 1, 4, jnp.where(grp == 2, 8, 16)))
    pos = pos0 + c * chunk + lax.broadcasted_iota(jnp.int32, (chunk, C_MIX), 0)
    cnt = jnp.minimum(pos + 1, win).astype(F32)
    d = win_sum / cnt - xp
    o_pool = _dot(d, poolw_ref[...]) * pools_ref[...]
    new_pool = pool_scr[0, pool_base + chunk - POOL_PAST:pool_base + chunk, :]
    pool_scr[0, pool_base - POOL_PAST:pool_base, :] = new_pool

    gq = proj_ref[:, OFF_GQ:OFF_GQ + 128] * (GLA_DK ** -0.5)
    gk = proj_ref[:, OFF_GK:OFF_GK + 128]
    gv = proj_ref[:, OFF_GV:OFF_GV + C_MIX]
    gr = proj_ref[:, OFF_GR:OFF_GR + C_MIX]
    lr = proj_ref[:, OFF_LR:OFF_LR + 128]
    logit = _dot(lr, wgate_ref[...]) + gbias_ref[...]
    log_alpha = _log_sigmoid(logit) * (1.0 / GLA_TAU)
    o = _linear_attention(gq, gk, gv, log_alpha, gla_st, GLA_H, GLA_DK, chunk, sub)
    o_gla = _segment_rms(o, GLA_DV, gnorm_ref[...]) * _silu(gr)

    rq = proj_ref[:, OFF_RQ:OFF_RQ + C_MIX]
    rf = proj_ref[:, OFF_RF:OFF_RF + C_MIX]
    ri = proj_ref[:, OFF_RI:OFF_RI + C_MIX]
    rg = proj_ref[:, OFF_RG:OFF_RG + C_MIX]
    lbl = lbl_ref[...]
    lmax = jnp.max(lbl, axis=0, keepdims=True)
    lexp = jnp.exp(lbl - lmax)
    lsum = jnp.sum(lexp, axis=0, keepdims=True)
    lb = jnp.zeros((1, C_MIX), F32)
    for m in range(1, layer + 1):
        lb = lb + lexp[m:m + 1] / lsum
    a_ = jnp.log(lb)
    b_ = jnp.log(1.0 - lb) + _log_sigmoid(rf)
    log_f = jnp.maximum(a_, b_) + jnp.log(1.0 + jnp.exp(-jnp.abs(a_ - b_)))
    hk = 1.0 - jnp.exp(log_f)
    o = _linear_attention(_silu(rq), hk, ri, log_f, hg_st, HG_H, HG_DK, chunk, sub)
    o_hg = _segment_rms(o, HG_DV, hnorm_ref[...]) * _silu(rg)

    conv_scr[conv_base:conv_base + chunk, :] = proj_ref[:, OFF_XBC:OFF_XBC + SSM_CONV_DIM]
    conv = None
    for w in range(SSM_CONV):
        start = conv_base - (SSM_CONV - 1) + w
        term = conv_scr[start:start + chunk, :] * convw_ref[w:w + 1, :]
        conv = term if conv is None else conv + term
    conv = _silu(conv + convbias_ref[...])
    new_conv = conv_scr[conv_base + chunk - (SSM_CONV - 1):conv_base + chunk, :]
    conv_scr[conv_base - (SSM_CONV - 1):conv_base, :] = new_conv

    xs = conv[:, 0:C_MIX]
    dt = _softplus(proj_ref[:, OFF_DTX:OFF_DTX + C_MIX] + dtb_ref[...])
    a_neg = -jnp.exp(alog_ref[...])
    la = dt * a_neg
    xdt = xs * dt
    cum = _dot_exact_rhs(_tril(chunk).astype(F32), la)
    cum_t = cum.T
    causal = _tril(chunk)
    att_rows = []
    for h in range(SSM_H):
        g = h // (SSM_H // SSM_G)
        if h % (SSM_H // SSM_G) == 0:
            bm = conv[:, C_MIX + SSM_N * g:C_MIX + SSM_N * (g + 1)]
            cm = conv[:, C_MIX + SSM_G * SSM_N + SSM_N * g:C_MIX + SSM_G * SSM_N + SSM_N * (g + 1)]
            cb = _dot_nt(cm, bm)
        col = cum[:, SSM_P * h:SSM_P * h + 1]
        row = cum_t[SSM_P * h:SSM_P * h + 1, :]
        att_rows.append(cb * jnp.exp(jnp.where(causal, col - row, -jnp.inf)))
    y = _select_heads(_dot(jnp.concatenate(att_rows, axis=0), xdt), SSM_H, chunk, SSM_P)
    cum_last = cum[chunk - 1:chunk, :]
    decay_in = jnp.exp(cum)
    xw = xdt * jnp.exp(cum_last - cum)
    decay_state = jnp.exp(cum_last)
    y_state = []
    for g in range(SSM_G):
        lanes = slice(SSM_N * g, SSM_N * (g + 1))
        bm = conv[:, C_MIX + SSM_N * g:C_MIX + SSM_N * (g + 1)]
        cm = conv[:, C_MIX + SSM_G * SSM_N + SSM_N * g:C_MIX + SSM_G * SSM_N + SSM_N * (g + 1)]
        st = ssm_st[g]
        y_state.append(_dot(cm, st))
        ssm_st[g] = st * decay_state[:, lanes] + _dot_tn(bm, xw[:, lanes])
    y = y + jnp.concatenate(y_state, axis=1) * decay_in + dskip_ref[...] * xs
    y = y * _silu(proj_ref[:, OFF_SZ:OFF_SZ + C_MIX])
    o_ssm = jnp.concatenate(
        [_rms(y[:, 128 * g:128 * (g + 1)], snorm_ref[:, 128 * g:128 * (g + 1)]) for g in range(SSM_G)], axis=1)

    cat_ref[:, 0:C_MIX] = o_pool
    cat_ref[:, C_MIX:2 * C_MIX] = o_gla
    cat_ref[:, 2 * C_MIX:3 * C_MIX] = o_hg
    cat_ref[:, 3 * C_MIX:4 * C_MIX] = o_ssm

    @pl.when(c == n_c - 1)
    def _():
        npool_ref[0] = new_pool
        nconv_ref[0] = new_conv
        ngla_ref[0] = _store_state_t(gla_st[...], GLA_H, GLA_DK)
        nhg_ref[0] = _store_state_t(hg_st[...], HG_H, HG_DK)
        for g in range(SSM_G):
            nssm_ref[0, g] = ssm_st[g].T


def _mixer_call(proj, cat_prev, states, params, *, n_seq, seq_len, chunk, row0, pos0, layer):
    rows = proj.shape[0]
    n_c = seq_len // chunk
    blk0 = row0 // chunk
    pool_b, gla_s, hg_s, ssm_s, conv_b = states
    row_map = lambda b, c: (blk0 + b * n_c + c, 0)
    seq3 = lambda b, c: (b, 0, 0)
    seq4 = lambda b, c: (b, 0, 0, 0)
    const2 = lambda b, c: (0, 0)
    state_specs = [pl.BlockSpec((1, POOL_PAST, C_MIX), seq3), pl.BlockSpec((1, GLA_H * GLA_DK, GLA_DV), seq3),
                   pl.BlockSpec((1, HG_H * HG_DK, HG_DV), seq3), pl.BlockSpec((1, SSM_G, 128, SSM_N), seq4),
                   pl.BlockSpec((1, SSM_CONV - 1, SSM_CONV_DIM), seq3)]
    param_specs = [pl.BlockSpec(p.shape, const2) for p in params]
    in_specs = [pl.BlockSpec((chunk, W_PROJ), row_map)] + state_specs + param_specs
    args = [proj, pool_b, gla_s, hg_s, ssm_s, conv_b] + list(params)
    aliases = {}
    if cat_prev is not None:
        in_specs.append(pl.BlockSpec(memory_space=pl.ANY))
        args.append(cat_prev)
        aliases = {len(args) - 1: 0}
    else:
        in_specs.append(pl.BlockSpec((1, 128), const2))
        args.append(jnp.zeros((1, 128), F32))
    out_shape = [jax.ShapeDtypeStruct((rows, D_MODEL), F32),
                 jax.ShapeDtypeStruct((n_seq, POOL_PAST, C_MIX), F32),
                 jax.ShapeDtypeStruct((n_seq, GLA_H * GLA_DK, GLA_DV), F32),
                 jax.ShapeDtypeStruct((n_seq, HG_H * HG_DK, HG_DV), F32),
                 jax.ShapeDtypeStruct((n_seq, SSM_G, 128, SSM_N), F32),
                 jax.ShapeDtypeStruct((n_seq, SSM_CONV - 1, SSM_CONV_DIM), F32)]
    out_specs = [pl.BlockSpec((chunk, D_MODEL), row_map)] + state_specs
    scratch = [pltpu.VMEM((4, 24 + chunk, C_MIX), F32), pltpu.VMEM((SUBLANES + chunk, SSM_CONV_DIM), F32),
               pltpu.VMEM((GLA_H * GLA_DV, GLA_H * GLA_DK), F32), pltpu.VMEM((HG_H * HG_DV, HG_H * HG_DK), F32),
               pltpu.VMEM((SSM_G, SSM_N, 128), F32)]
    return pl.pallas_call(
        functools.partial(_mixer_kernel, chunk=chunk, pos0=pos0, layer=layer),
        grid=(n_seq, n_c),
        in_specs=in_specs,
        out_specs=out_specs,
        out_shape=out_shape,
        scratch_shapes=scratch,
        input_output_aliases=aliases,
        compiler_params=pltpu.CompilerParams(dimension_semantics=("arbitrary", "arbitrary"),
                                             vmem_limit_bytes=VMEM_LIMIT_BYTES),
        name=f"mixer_chunk{chunk}",
    )(*args)


def _arrange_w_in(w):
    lr = w[:, 1024:1040]
    sdt = w[:, 3088:3092]
    tail = jnp.concatenate([lr, sdt, jnp.zeros((D_MODEL, 128 - GLA_RANK - SSM_H), w.dtype)], axis=1)
    return jnp.concatenate([w[:, :1024], w[:, 1040:3088], tail, jnp.repeat(sdt, SSM_P, axis=1)], axis=1)


def _pool_block_diag(pw):
    out = jnp.zeros((C_MIX, C_MIX), pw.dtype)
    for g in range(len(POOL_WINDOWS)):
        out = out.at[POOL_GC * g:POOL_GC * (g + 1), POOL_GC * g:POOL_GC * (g + 1)].set(pw[g])
    return out


def kernel(x_prompt, x_sample, state_pool, state_gla, state_hgrn, state_ssm, state_conv, ffn1_norm, ffn1_w_gate, ffn1_w_up, ffn1_w_down, mix_norm, w_in, pool_w, pool_scale, gla_w_gate, gla_gate_bias, gla_norm, hgrn_lb_logits, hgrn_norm, ssm_conv_w, ssm_conv_b, ssm_dt_bias, ssm_A_log, ssm_D, ssm_norm, w_out, ffn2_norm, ffn2_w_gate, ffn2_w_up, ffn2_w_down, final_norm):
    bp, tp, _ = x_prompt.shape
    bs, ts, _ = x_sample.shape
    rows_p, rows_s = bp * tp, bs * ts
    x = jnp.concatenate([x_prompt.reshape(rows_p, D_MODEL), x_sample.reshape(rows_s, D_MODEL)], axis=0)
    row = lambda v: v.reshape(1, -1).astype(F32)
    rep = lambda v, n: jnp.repeat(v, n).reshape(1, -1).astype(F32)

    prompt_states, sample_states = [], []
    for l in range(DEPTH):
        x1, proj = _pre_call(x, row(ffn1_norm[l]), ffn1_w_gate[l].astype(BF16), ffn1_w_up[l].astype(BF16),
                             ffn1_w_down[l].astype(BF16), row(mix_norm[l]), _arrange_w_in(w_in[l]).astype(BF16))
        wgate = jnp.zeros((128, GLA_H * GLA_DK), F32).at[:GLA_RANK].set(gla_w_gate[l]).astype(BF16)
        params = [_pool_block_diag(pool_w[l]).astype(BF16), row(pool_scale[l]), wgate, row(gla_gate_bias[l]),
                  row(jnp.tile(gla_norm[l], GLA_H)), hgrn_lb_logits.astype(F32), row(jnp.tile(hgrn_norm[l], HG_H)),
                  ssm_conv_w[l].astype(F32), row(ssm_conv_b[l]), rep(ssm_dt_bias[l], SSM_P),
                  rep(ssm_A_log[l], SSM_P), rep(ssm_D[l], SSM_P), row(ssm_norm[l])]
        zeros_p = (jnp.zeros((bp, POOL_PAST, C_MIX), F32), jnp.zeros((bp, GLA_H * GLA_DK, GLA_DV), F32),
                   jnp.zeros((bp, HG_H * HG_DK, HG_DV), F32), jnp.zeros((bp, SSM_G, 128, SSM_N), F32),
                   jnp.zeros((bp, SSM_CONV - 1, SSM_CONV_DIM), F32))
        outs_p = _mixer_call(proj, None, zeros_p, params, n_seq=bp, seq_len=tp, chunk=PROMPT_CHUNK,
                             row0=0, pos0=0, layer=l)
        carried = (state_pool[l], state_gla[l].reshape(bs, GLA_H * GLA_DK, GLA_DV),
                   state_hgrn[l].reshape(bs, HG_H * HG_DK, HG_DV), state_ssm[l].reshape(bs, SSM_G, 128, SSM_N),
                   state_conv[l])
        outs_s = _mixer_call(proj, outs_p[0], carried, params, n_seq=bs, seq_len=ts, chunk=ts,
                             row0=rows_p, pos0=PAST_LEN, layer=l)
        prompt_states.append(outs_p[1:])
        sample_states.append(outs_s[1:])
        x = _post_call(x1, outs_s[0], w_out[l].astype(BF16), row(ffn2_norm[l]), ffn2_w_gate[l].astype(BF16),
                       ffn2_w_up[l].astype(BF16), ffn2_w_down[l].astype(BF16), row(final_norm),
                       final=(l == DEPTH - 1))

    def stack(states, i, shape):
        return jnp.stack([s[i] for s in states]).reshape(shape)

    outs = [x[:rows_p].reshape(bp, tp, D_MODEL), x[rows_p:].reshape(bs, ts, D_MODEL)]
    for i, tail in enumerate([(POOL_PAST, C_MIX), (GLA_H, GLA_DK, GLA_DV), (HG_H, HG_DK, HG_DV),
                              (SSM_H, SSM_P, SSM_N), (SSM_CONV - 1, SSM_CONV_DIM)]):
        outs.append(stack(prompt_states, i, (DEPTH, bp) + tail))
        outs.append(stack(sample_states, i, (DEPTH, bs) + tail))
    return tuple(outs)
```

```python
import functools

import jax
import jax.numpy as jnp
from jax import lax
from jax.experimental import pallas as pl
from jax.experimental.pallas import tpu as pltpu

F32 = jnp.float32
BF16 = jnp.bfloat16

D_MODEL = 1024
D_FF = 2816
DEPTH = 2
EPS = 1e-6
PAST_LEN = 16384

C_MIX = 256
POOL_WINDOWS = (2, 4, 8, 16)
POOL_GC = 64
POOL_PAST = 15
GLA_H, GLA_DK, GLA_DV, GLA_RANK, GLA_TAU = 4, 32, 64, 16, 16.0
HG_H, HG_DK, HG_DV = 4, 64, 64
SSM_H, SSM_P, SSM_G, SSM_N, SSM_CONV = 4, 64, 2, 128, 4
SSM_CONV_DIM = 768
N_IN = 3092

OFF_PX, OFF_GQ, OFF_GK, OFF_GV, OFF_GR = 0, 256, 384, 512, 768
OFF_RQ, OFF_RF, OFF_RI, OFF_RG = 1024, 1280, 1536, 1792
OFF_SZ, OFF_XBC = 2048, 2304
OFF_LR = 3072
OFF_DTX = 3200
W_PROJ = 3456

PT_POOL_SCALE, PT_GBIAS, PT_GNORM, PT_HNORM, PT_CONVB, PT_DTB, PT_ALOG, PT_DSKIP, PT_SNORM = range(9)
PT_CONVW = 9
PT_LBL = 13
PT_ROWS = 16

VMEM_LIMIT_BYTES = 56 * 1024 * 1024
SUBLANES = 8

ROW_TILE = 512
FF_CHUNKS = (0, 768, 1536, 2304, D_FF)
PROMPT_CHUNK = 64
SUB_BLOCK = 16


def _sigmoid(x):
    return 1.0 / (1.0 + jnp.exp(-x))


def _silu(x):
    return x * _sigmoid(x)


def _log_sigmoid(x):
    return jnp.minimum(x, 0.0) - jnp.log(1.0 + jnp.exp(-jnp.abs(x)))


def _softplus(x):
    return jnp.maximum(x, 0.0) + jnp.log(1.0 + jnp.exp(-jnp.abs(x)))


def _rms(x, g):
    ms = jnp.mean(x * x, axis=-1, keepdims=True)
    return x * lax.rsqrt(ms + EPS) * g


def _dot(a, b):
    return jnp.dot(a.astype(BF16), b.astype(BF16), preferred_element_type=F32)


def _dot_nt(a, b):
    return lax.dot_general(a.astype(BF16), b.astype(BF16), (((1,), (1,)), ((), ())),
                           preferred_element_type=F32)


def _dot_tn(a, b):
    return lax.dot_general(a.astype(BF16), b.astype(BF16), (((0,), (0,)), ((), ())),
                           preferred_element_type=F32)


def _split3(x):
    hi = x.astype(BF16)
    r1 = x - hi.astype(F32)
    mid = r1.astype(BF16)
    lo = (r1 - mid.astype(F32)).astype(BF16)
    return hi, mid, lo


def _dot_exact_rhs(a01, x):
    hi, mid, lo = _split3(x)
    a = a01.astype(BF16)
    return (jnp.dot(a, hi, preferred_element_type=F32) + jnp.dot(a, mid, preferred_element_type=F32)
            + jnp.dot(a, lo, preferred_element_type=F32))


def _dot_exact_lhs(x, b01):
    hi, mid, lo = _split3(x)
    b = b01.astype(BF16)
    return (jnp.dot(hi, b, preferred_element_type=F32) + jnp.dot(mid, b, preferred_element_type=F32)
            + jnp.dot(lo, b, preferred_element_type=F32))


def _swiglu_half(x, g, wg_ref, wu_ref, wd_ref):
    h = _rms(x, g).astype(BF16)
    acc = None
    for lo, hi in zip(FF_CHUNKS[:-1], FF_CHUNKS[1:]):
        sl = slice(lo, hi)
        gate = jnp.dot(h, wg_ref[:, sl], preferred_element_type=F32)
        up = jnp.dot(h, wu_ref[:, sl], preferred_element_type=F32)
        act = (_silu(gate) * up).astype(BF16)
        d = jnp.dot(act, wd_ref[sl, :], preferred_element_type=F32)
        acc = d if acc is None else acc + d
    return x + 0.5 * acc


def _pre_kernel(*refs, n_first):
    if n_first is None:
        x = refs[0][...]
        refs = refs[1:]
    else:
        x = jnp.where(pl.program_id(0) < n_first, refs[0][...], refs[1][...])
        refs = refs[2:]
    g1_ref, wg_ref, wu_ref, wd_ref, gm_ref, win_ref, x1_ref, proj_ref = refs
    x1 = _swiglu_half(x, g1_ref[...], wg_ref, wu_ref, wd_ref)
    x1_ref[...] = x1
    h = _rms(x1, gm_ref[...]).astype(BF16)
    proj_ref[...] = jnp.dot(h, win_ref[...], preferred_element_type=F32)


def _post_kernel(x_ref, cat_ref, wout_ref, g2_ref, wg_ref, wu_ref, wd_ref, gf_ref, *o_refs, n_first):
    x2 = x_ref[...] + jnp.dot(cat_ref[...].astype(BF16), wout_ref[...], preferred_element_type=F32)
    x3 = _swiglu_half(x2, g2_ref[...], wg_ref, wu_ref, wd_ref)
    if n_first is None:
        o_refs[0][...] = x3
    else:
        y = _rms(x3, gf_ref[...])
        i = pl.program_id(0)

        @pl.when(i < n_first)
        def _():
            o_refs[0][...] = y

        @pl.when(i >= n_first)
        def _():
            o_refs[1][...] = y


def _layer_spec(shape, layer):
    nd = len(shape)
    return pl.BlockSpec((None,) + tuple(shape), lambda *_: (layer,) + (0,) * nd, pipeline_mode=pl.Buffered(1))


def _dense_params():
    return pltpu.CompilerParams(dimension_semantics=("arbitrary",), vmem_limit_bytes=VMEM_LIMIT_BYTES)


def _row_spec(width):
    return pl.BlockSpec((ROW_TILE, width), lambda i: (i, 0))


def _split_specs(n_first, width):
    return [pl.BlockSpec((ROW_TILE, width), lambda i: (jnp.minimum(i, n_first - 1), 0)),
            pl.BlockSpec((ROW_TILE, width), lambda i: (jnp.maximum(i - n_first, 0), 0))]


def _pre_call(xs, g1, wg, wu, wd, gm, win, layer):
    rows = sum(x.shape[0] for x in xs)
    n_first = xs[0].shape[0] // ROW_TILE if len(xs) == 2 else None
    x_specs = _split_specs(n_first, D_MODEL) if len(xs) == 2 else [_row_spec(D_MODEL)]
    return pl.pallas_call(
        functools.partial(_pre_kernel, n_first=n_first),
        grid=(rows // ROW_TILE,),
        in_specs=x_specs + [_layer_spec((1, D_MODEL), layer), _layer_spec((D_MODEL, D_FF), layer),
                            _layer_spec((D_MODEL, D_FF), layer), _layer_spec((D_FF, D_MODEL), layer),
                            _layer_spec((1, D_MODEL), layer), _layer_spec((D_MODEL, W_PROJ), layer)],
        out_specs=[_row_spec(D_MODEL), _row_spec(W_PROJ)],
        out_shape=[jax.ShapeDtypeStruct((rows, D_MODEL), F32), jax.ShapeDtypeStruct((rows, W_PROJ), F32)],
        compiler_params=_dense_params(),
        name="pre_ffn_inproj",
    )(*xs, g1, wg, wu, wd, gm, win)


def _post_call(x, cat, wout, g2, wg, wu, wd, gf, layer, split_rows):
    rows = x.shape[0]
    if split_rows is None:
        n_first = None
        out_specs = _row_spec(D_MODEL)
        out_shape = jax.ShapeDtypeStruct((rows, D_MODEL), F32)
    else:
        n_first = split_rows[0] // ROW_TILE
        out_specs = _split_specs(n_first, D_MODEL)
        out_shape = [jax.ShapeDtypeStruct((r, D_MODEL), F32) for r in split_rows]
    return pl.pallas_call(
        functools.partial(_post_kernel, n_first=n_first),
        grid=(rows // ROW_TILE,),
        in_specs=[_row_spec(D_MODEL), _row_spec(D_MODEL), _layer_spec((D_MODEL, D_MODEL), layer),
                  _layer_spec((1, D_MODEL), layer), _layer_spec((D_MODEL, D_FF), layer),
                  _layer_spec((D_MODEL, D_FF), layer), _layer_spec((D_FF, D_MODEL), layer),
                  pl.BlockSpec((1, D_MODEL), lambda i: (0, 0))],
        out_specs=out_specs,
        out_shape=out_shape,
        compiler_params=_dense_params(),
        name="post_outproj_ffn",
    )(x, cat, wout, g2, wg, wu, wd, gf)


def _lane_group(shape, width):
    return lax.broadcasted_iota(jnp.int32, shape, len(shape) - 1) // width


def _block_diag_mask(rows, row_w, cols, col_w):
    r = lax.broadcasted_iota(jnp.int32, (rows, cols), 0) // row_w
    c = lax.broadcasted_iota(jnp.int32, (rows, cols), 1) // col_w
    return r == c


def _tril(n):
    r = lax.broadcasted_iota(jnp.int32, (n, n), 0)
    c = lax.broadcasted_iota(jnp.int32, (n, n), 1)
    return r >= c


def _select_heads(stacked, n_heads, rows, head_w):
    grp = _lane_group((rows, n_heads * head_w), head_w)
    out = jnp.zeros((rows, n_heads * head_w), F32)
    for h in range(n_heads):
        out = jnp.where(grp == h, stacked[h * rows:(h + 1) * rows], out)
    return out


def _linear_attention(q, k, v, la, st_ref, n_heads, dk, chunk, sub):
    nk = n_heads * dk
    nv = n_heads * 64
    n_sub = chunk // sub
    b = _dot_exact_rhs(_tril(chunk).astype(F32), la)
    refs = [None] + [b[sub * i - 1:sub * i, :] for i in range(1, n_sub)]
    c_loc = jnp.concatenate([b[0:sub]] + [b[sub * i:sub * (i + 1)] - refs[i] for i in range(1, n_sub)], axis=0) \
        if n_sub > 1 else b
    st = st_ref[...]
    o = _dot_nt(q * jnp.exp(b), st)

    qt = q * jnp.exp(c_loc)
    kgrp = _lane_group((sub, nk), dk)
    outs = []
    for i in range(n_sub):
        rows = slice(sub * i, sub * (i + 1))
        o_i = o[rows]
        if i > 0:
            prev = slice(0, sub * i)
            kt = k[prev] * jnp.exp(refs[i] - b[prev])
            q_stack = jnp.concatenate([jnp.where(kgrp == h, qt[rows], 0.0) for h in range(n_heads)], axis=0)
            att = _dot_nt(q_stack, kt)
            o_i = o_i + _select_heads(_dot(att, v[prev]), n_heads, sub, 64)
        c_i = c_loc[rows]
        q_i = q[rows]
        row_id = lax.broadcasted_iota(jnp.int32, (sub, nk), 0)
        e_list = []
        for j in range(sub):
            c_j = c_loc[sub * i + j:sub * i + j + 1, :]
            k_j = k[sub * i + j:sub * i + j + 1, :]
            dec = jnp.exp(jnp.where(row_id >= j, c_i - c_j, -jnp.inf))
            e_list.append(dec * q_i * k_j)
        e_all = jnp.concatenate(e_list, axis=0)
        expand = _block_diag_mask(nk, dk, nv, 64).astype(F32)
        r_d = _dot(e_all, expand)
        for j in range(sub):
            o_i = o_i + r_d[j * sub:(j + 1) * sub] * v[sub * i + j:sub * i + j + 1, :]
        outs.append(o_i)
    o = jnp.concatenate(outs, axis=0) if n_sub > 1 else outs[0]

    b_last = b[chunk - 1:chunk, :]
    k_end = k * jnp.exp(b_last - b)
    upd = _dot_tn(v, k_end)
    st_ref[...] = st * jnp.exp(b_last) + jnp.where(_block_diag_mask(nv, 64, nk, dk), upd, 0.0)
    return o


def _load_state_t(s, n_heads, dk):
    st = s.T
    tiled = jnp.concatenate([st] * n_heads, axis=0)
    return jnp.where(_block_diag_mask(n_heads * 64, 64, n_heads * dk, dk), tiled, 0.0)


def _store_state_t(st, n_heads, dk):
    acc = st[0:64]
    for h in range(1, n_heads):
        acc = acc + st[64 * h:64 * (h + 1)]
    return acc.T


def _segment_rms(o, width, gain):
    n = o.shape[-1]
    seg = _block_diag_mask(n, width, n, width).astype(F32)
    ms = _dot_exact_lhs(o * o, seg) * (1.0 / width)
    return o * lax.rsqrt(ms + EPS) * gain


def _mixer_kernel(*refs, chunk, pos0, layer, carried, n_alias):
    proj_ref = refs[0]
    n_in = 1 + (5 if carried else 0)
    poolw_ref, wgate_ref, ptab_ref = refs[n_in:n_in + 3]
    n_in += 3 + n_alias
    cat_ref, npool_ref, ngla_ref, nhg_ref, nssm_ref, nconv_ref = refs[n_in:n_in + 6]
    pool_scr, conv_scr, gla_st, hg_st, ssm_st = refs[n_in + 6:]
    prow = lambda r, w: ptab_ref[r:r + 1, 0:w]
    c = pl.program_id(1)
    n_c = pl.num_programs(1)
    sub = min(SUB_BLOCK, chunk)
    pool_base = 24
    conv_base = SUBLANES

    @pl.when(c == 0)
    def _():
        pool_scr[...] = jnp.zeros(pool_scr.shape, F32)
        conv_scr[...] = jnp.zeros(conv_scr.shape, F32)
        if carried:
            poolb_ref, glas_ref, hgs_ref, ssms_ref, convb_ref = refs[1:6]
            pool_scr[0, pool_base - POOL_PAST:pool_base, :] = poolb_ref[0]
            conv_scr[conv_base - (SSM_CONV - 1):conv_base, :] = convb_ref[0]
            gla_st[...] = _load_state_t(glas_ref[0], GLA_H, GLA_DK)
            hg_st[...] = _load_state_t(hgs_ref[0], HG_H, HG_DK)
            for g in range(SSM_G):
                ssm_st[g] = ssms_ref[0, g].T
        else:
            gla_st[...] = jnp.zeros(gla_st.shape, F32)
            hg_st[...] = jnp.zeros(hg_st.shape, F32)
            ssm_st[...] = jnp.zeros(ssm_st.shape, F32)

    xp = proj_ref[:, OFF_PX:OFF_PX + C_MIX]
    pool_scr[0, pool_base:pool_base + chunk, :] = xp
    n_ext = pool_base - SUBLANES + chunk
    for s, shift in enumerate((1, 2, 4)):
        cur = pool_scr[s, SUBLANES:SUBLANES + n_ext, :]
        prev = pool_scr[s, SUBLANES - shift:SUBLANES - shift + n_ext, :]
        pool_scr[s + 1, SUBLANES:SUBLANES + n_ext, :] = cur + prev
    s2 = pool_scr[1, pool_base:pool_base + chunk, :]
    s4 = pool_scr[2, pool_base:pool_base + chunk, :]
    s8 = pool_scr[3, pool_base:pool_base + chunk, :]
    s16 = s8 + pool_scr[3, pool_base - 8:pool_base - 8 + chunk, :]
    grp = _lane_group((chunk, C_MIX), POOL_GC)
    win_sum = jnp.where(grp == 0, s2, jnp.where(grp == 1, s4, jnp.where(grp == 2, s8, s16)))
    win = jnp.where(grp == 0, 2, jnp.where(grp == 1, 4, jnp.where(grp == 2, 8, 16)))
    pos = pos0 + c * chunk + lax.broadcasted_iota(jnp.int32, (chunk, C_MIX), 0)
    cnt = jnp.minimum(pos + 1, win).astype(F32)
    d = win_sum / cnt - xp
    o_pool = _dot(d, poolw_ref[...]) * prow(PT_POOL_SCALE, C_MIX)
    new_pool = pool_scr[0, pool_base + chunk - POOL_PAST:pool_base + chunk, :]
    pool_scr[0, pool_base - POOL_PAST:pool_base, :] = new_pool

    gq = proj_ref[:, OFF_GQ:OFF_GQ + 128] * (GLA_DK ** -0.5)
    gk = proj_ref[:, OFF_GK:OFF_GK + 128]
    gv = proj_ref[:, OFF_GV:OFF_GV + C_MIX]
    gr = proj_ref[:, OFF_GR:OFF_GR + C_MIX]
    lr = proj_ref[:, OFF_LR:OFF_LR + 128]
    logit = _dot(lr, wgate_ref[...]) + prow(PT_GBIAS, 128)
    log_alpha = _log_sigmoid(logit) * (1.0 / GLA_TAU)
    o = _linear_attention(gq, gk, gv, log_alpha, gla_st, GLA_H, GLA_DK, chunk, sub)
    o_gla = _segment_rms(o, GLA_DV, prow(PT_GNORM, C_MIX)) * _silu(gr)

    rq = proj_ref[:, OFF_RQ:OFF_RQ + C_MIX]
    rf = proj_ref[:, OFF_RF:OFF_RF + C_MIX]
    ri = proj_ref[:, OFF_RI:OFF_RI + C_MIX]
    rg = proj_ref[:, OFF_RG:OFF_RG + C_MIX]
    lbl = ptab_ref[PT_LBL:PT_LBL + DEPTH, 0:C_MIX]
    lmax = jnp.max(lbl, axis=0, keepdims=True)
    lexp = jnp.exp(lbl - lmax)
    lsum = jnp.sum(lexp, axis=0, keepdims=True)
    lb = jnp.zeros((1, C_MIX), F32)
    for m in range(1, layer + 1):
        lb = lb + lexp[m:m + 1] / lsum
    a_ = jnp.log(lb)
    b_ = jnp.log(1.0 - lb) + _log_sigmoid(rf)
    log_f = jnp.maximum(a_, b_) + jnp.log(1.0 + jnp.exp(-jnp.abs(a_ - b_)))
    hk = 1.0 - jnp.exp(log_f)
    o = _linear_attention(_silu(rq), hk, ri, log_f, hg_st, HG_H, HG_DK, chunk, sub)
    o_hg = _segment_rms(o, HG_DV, prow(PT_HNORM, C_MIX)) * _silu(rg)

    conv_scr[conv_base:conv_base + chunk, :] = proj_ref[:, OFF_XBC:OFF_XBC + SSM_CONV_DIM]
    conv = None
    for w in range(SSM_CONV):
        start = conv_base - (SSM_CONV - 1) + w
        term = conv_scr[start:start + chunk, :] * prow(PT_CONVW + w, SSM_CONV_DIM)
        conv = term if conv is None else conv + term
    conv = _silu(conv + prow(PT_CONVB, SSM_CONV_DIM))
    new_conv = conv_scr[conv_base + chunk - (SSM_CONV - 1):conv_base + chunk, :]
    conv_scr[conv_base - (SSM_CONV - 1):conv_base, :] = new_conv

    xs = conv[:, 0:C_MIX]
    dt = _softplus(proj_ref[:, OFF_DTX:OFF_DTX + C_MIX] + prow(PT_DTB, C_MIX))
    a_neg = -jnp.exp(prow(PT_ALOG, C_MIX))
    la = dt * a_neg
    xdt = xs * dt
    cum = _dot_exact_rhs(_tril(chunk).astype(F32), la)
    cum_t = cum.T
    causal = _tril(chunk)
    att_rows = []
    for h in range(SSM_H):
        g = h // (SSM_H // SSM_G)
        if h % (SSM_H // SSM_G) == 0:
            bm = conv[:, C_MIX + SSM_N * g:C_MIX + SSM_N * (g + 1)]
            cm = conv[:, C_MIX + SSM_G * SSM_N + SSM_N * g:C_MIX + SSM_G * SSM_N + SSM_N * (g + 1)]
            cb = _dot_nt(cm, bm)
        col = cum[:, SSM_P * h:SSM_P * h + 1]
        row = cum_t[SSM_P * h:SSM_P * h + 1, :]
        att_rows.append(cb * jnp.exp(jnp.where(causal, col - row, -jnp.inf)))
    y = _select_heads(_dot(jnp.concatenate(att_rows, axis=0), xdt), SSM_H, chunk, SSM_P)
    cum_last = cum[chunk - 1:chunk, :]
    decay_in = jnp.exp(cum)
    xw = xdt * jnp.exp(cum_last - cum)
    decay_state = jnp.exp(cum_last)
    y_state = []
    for g in range(SSM_G):
        lanes = slice(SSM_N * g, SSM_N * (g + 1))
        bm = conv[:, C_MIX + SSM_N * g:C_MIX + SSM_N * (g + 1)]
        cm = conv[:, C_MIX + SSM_G * SSM_N + SSM_N * g:C_MIX + SSM_G * SSM_N + SSM_N * (g + 1)]
        st = ssm_st[g]
        y_state.append(_dot(cm, st))
        ssm_st[g] = st * decay_state[:, lanes] + _dot_tn(bm, xw[:, lanes])
    y = y + jnp.concatenate(y_state, axis=1) * decay_in + prow(PT_DSKIP, C_MIX) * xs
    y = y * _silu(proj_ref[:, OFF_SZ:OFF_SZ + C_MIX])
    o_ssm = jnp.concatenate(
        [_rms(y[:, 128 * g:128 * (g + 1)], ptab_ref[PT_SNORM:PT_SNORM + 1, 128 * g:128 * (g + 1)])
         for g in range(SSM_G)], axis=1)

    cat_ref[:, 0:C_MIX] = o_pool
    cat_ref[:, C_MIX:2 * C_MIX] = o_gla
    cat_ref[:, 2 * C_MIX:3 * C_MIX] = o_hg
    cat_ref[:, 3 * C_MIX:4 * C_MIX] = o_ssm

    @pl.when(c == n_c - 1)
    def _():
        npool_ref[0] = new_pool
        nconv_ref[0] = new_conv
        ngla_ref[0] = _store_state_t(gla_st[...], GLA_H, GLA_DK)
        nhg_ref[0] = _store_state_t(hg_st[...], HG_H, HG_DK)
        for g in range(SSM_G):
            nssm_ref[0, g] = ssm_st[g].T


def _mixer_call(proj, cat_prev, carried, prev_states, mats, *, n_seq, seq_len, chunk, row0, pos0, layer):
    rows = proj.shape[0]
    n_c = seq_len // chunk
    blk0 = row0 // chunk
    row_map = lambda b, c: (blk0 + b * n_c + c, 0)
    state_tails = [(POOL_PAST, C_MIX), (GLA_H * GLA_DK, GLA_DV), (HG_H * HG_DK, HG_DV), (SSM_G, 128, SSM_N),
                   (SSM_CONV - 1, SSM_CONV_DIM)]

    def state_spec(tail):
        nd = len(tail)
        return pl.BlockSpec((None, 1) + tail, lambda b, c: (layer, b) + (0,) * nd)

    def mat_spec(m):
        nd = m.ndim - 1
        return pl.BlockSpec((None,) + m.shape[1:], lambda b, c: (layer,) + (0,) * nd)

    state_specs = [state_spec(t) for t in state_tails]
    in_specs = [pl.BlockSpec((chunk, W_PROJ), row_map)]
    args = [proj]
    if carried is not None:
        in_specs += state_specs
        args += list(carried)
    in_specs += [mat_spec(m) for m in mats]
    args += list(mats)
    aliases = {}
    n_alias = 0
    if cat_prev is not None:
        aliases[len(args)] = 0
        args.append(cat_prev)
        n_alias += 1
    if prev_states is not None:
        for i, s in enumerate(prev_states):
            aliases[len(args)] = 1 + i
            args.append(s)
            n_alias += 1
    in_specs += [pl.BlockSpec(memory_space=pl.ANY)] * n_alias
    out_shape = [jax.ShapeDtypeStruct((rows, D_MODEL), F32)] + [
        jax.ShapeDtypeStruct((DEPTH, n_seq) + t, F32) for t in state_tails]
    out_specs = [pl.BlockSpec((chunk, D_MODEL), row_map)] + state_specs
    scratch = [pltpu.VMEM((4, 24 + chunk, C_MIX), F32), pltpu.VMEM((SUBLANES + chunk, SSM_CONV_DIM), F32),
               pltpu.VMEM((GLA_H * GLA_DV, GLA_H * GLA_DK), F32), pltpu.VMEM((HG_H * HG_DV, HG_H * HG_DK), F32),
               pltpu.VMEM((SSM_G, SSM_N, 128), F32)]
    return pl.pallas_call(
        functools.partial(_mixer_kernel, chunk=chunk, pos0=pos0, layer=layer, carried=carried is not None,
                          n_alias=n_alias),
        grid=(n_seq, n_c),
        in_specs=in_specs,
        out_specs=out_specs,
        out_shape=out_shape,
        scratch_shapes=scratch,
        input_output_aliases=aliases,
        compiler_params=pltpu.CompilerParams(dimension_semantics=("arbitrary", "arbitrary"),
                                             vmem_limit_bytes=VMEM_LIMIT_BYTES),
        name=f"mixer_chunk{chunk}",
    )(*args)


def _arrange_w_in(w):
    lr = w[..., 1024:1040]
    sdt = w[..., 3088:3092]
    pad = jnp.zeros(w.shape[:-1] + (128 - GLA_RANK - SSM_H,), w.dtype)
    return jnp.concatenate([w[..., :1024], w[..., 1040:3088], lr, sdt, pad, jnp.repeat(sdt, SSM_P, axis=-1)], axis=-1)


def _pool_block_diag(pw):
    eye = jnp.eye(len(POOL_WINDOWS), dtype=pw.dtype)
    return jnp.einsum('lgcd,gh->lgchd', pw, eye).reshape(pw.shape[0], C_MIX, C_MIX)


def _param_table(pool_scale, gla_gate_bias, gla_norm, hgrn_norm, ssm_conv_b, ssm_dt_bias, ssm_A_log, ssm_D,
                 ssm_norm, ssm_conv_w, hgrn_lb_logits):
    def rows(v):
        v = v.astype(F32).reshape(DEPTH, -1, v.shape[-1])
        return jnp.pad(v, ((0, 0), (0, 0), (0, SSM_CONV_DIM - v.shape[-1])))
    lbl = jnp.broadcast_to(hgrn_lb_logits[None], (DEPTH,) + hgrn_lb_logits.shape)
    parts = [rows(pool_scale), rows(gla_gate_bias), rows(jnp.tile(gla_norm, (1, GLA_H))),
             rows(jnp.tile(hgrn_norm, (1, HG_H))), rows(ssm_conv_b), rows(jnp.repeat(ssm_dt_bias, SSM_P, axis=-1)),
             rows(jnp.repeat(ssm_A_log, SSM_P, axis=-1)), rows(jnp.repeat(ssm_D, SSM_P, axis=-1)), rows(ssm_norm),
             rows(ssm_conv_w), rows(lbl)]
    tab = jnp.concatenate(parts, axis=1)
    return jnp.pad(tab, ((0, 0), (0, PT_ROWS - tab.shape[1]), (0, 0)))


def kernel(x_prompt, x_sample, state_pool, state_gla, state_hgrn, state_ssm, state_conv, ffn1_norm, ffn1_w_gate, ffn1_w_up, ffn1_w_down, mix_norm, w_in, pool_w, pool_scale, gla_w_gate, gla_gate_bias, gla_norm, hgrn_lb_logits, hgrn_norm, ssm_conv_w, ssm_conv_b, ssm_dt_bias, ssm_A_log, ssm_D, ssm_norm, w_out, ffn2_norm, ffn2_w_gate, ffn2_w_up, ffn2_w_down, final_norm):
    bp, tp, _ = x_prompt.shape
    bs, ts, _ = x_sample.shape
    rows_p, rows_s = bp * tp, bs * ts
    gain = lambda v: v.reshape(DEPTH, 1, D_MODEL).astype(F32)
    bf = lambda w: w.astype(BF16)

    wgate = jnp.pad(gla_w_gate, ((0, 0), (0, 128 - GLA_RANK), (0, 0)))
    mats = [bf(_pool_block_diag(pool_w)), bf(wgate),
            _param_table(pool_scale, gla_gate_bias, gla_norm, hgrn_norm, ssm_conv_b, ssm_dt_bias, ssm_A_log, ssm_D,
                         ssm_norm, ssm_conv_w, hgrn_lb_logits)]
    carried = (state_pool, state_gla.reshape(DEPTH, bs, GLA_H * GLA_DK, GLA_DV),
               state_hgrn.reshape(DEPTH, bs, HG_H * HG_DK, HG_DV), state_ssm.reshape(DEPTH, bs, SSM_G, 128, SSM_N),
               state_conv)
    ffn1 = (gain(ffn1_norm), bf(ffn1_w_gate), bf(ffn1_w_up), bf(ffn1_w_down))
    ffn2 = (gain(ffn2_norm), bf(ffn2_w_gate), bf(ffn2_w_up), bf(ffn2_w_down))
    gm, win, wout = gain(mix_norm), bf(_arrange_w_in(w_in)), bf(w_out)
    gf = final_norm.reshape(1, D_MODEL).astype(F32)

    xs = [x_prompt.reshape(rows_p, D_MODEL), x_sample.reshape(rows_s, D_MODEL)]
    states_p = states_s = None
    for l in range(DEPTH):
        x1, proj = _pre_call(xs, *ffn1, gm, win, l)
        outs_p = _mixer_call(proj, None, None, states_p, mats, n_seq=bp, seq_len=tp, chunk=PROMPT_CHUNK,
                             row0=0, pos0=0, layer=l)
        outs_s = _mixer_call(proj, outs_p[0], carried, states_s, mats, n_seq=bs, seq_len=ts, chunk=ts,
                             row0=rows_p, pos0=PAST_LEN, layer=l)
        states_p, states_s = outs_p[1:], outs_s[1:]
        last = l == DEPTH - 1
        y = _post_call(x1, outs_s[0], wout, *ffn2, gf, l, (rows_p, rows_s) if last else None)
        xs = [y]

    outs = [y[0].reshape(bp, tp, D_MODEL), y[1].reshape(bs, ts, D_MODEL)]
    for i, tail in enumerate([(POOL_PAST, C_MIX), (GLA_H, GLA_DK, GLA_DV), (HG_H, HG_DK, HG_DV),
                              (SSM_H, SSM_P, SSM_N), (SSM_CONV - 1, SSM_CONV_DIM)]):
        outs.append(states_p[i].reshape((DEPTH, bp) + tail))
        outs.append(states_s[i].reshape((DEPTH, bs) + tail))
    return tuple(outs)
```

```python
import functools

import jax
import jax.numpy as jnp
from jax import lax
from jax.experimental import pallas as pl
from jax.experimental.pallas import tpu as pltpu

F32 = jnp.float32
BF16 = jnp.bfloat16

D_MODEL = 1024
D_FF = 2816
DEPTH = 2
EPS = 1e-6
PAST_LEN = 16384

C_MIX = 256
POOL_WINDOWS = (2, 4, 8, 16)
POOL_GC = 64
POOL_PAST = 15
GLA_H, GLA_DK, GLA_DV, GLA_RANK, GLA_TAU = 4, 32, 64, 16, 16.0
HG_H, HG_DK, HG_DV = 4, 64, 64
SSM_H, SSM_P, SSM_G, SSM_N, SSM_CONV = 4, 64, 2, 128, 4
SSM_CONV_DIM = 768
N_IN = 3092

OFF_PX, OFF_GQ, OFF_GK, OFF_GV, OFF_GR = 0, 256, 384, 512, 768
OFF_RQ, OFF_RF, OFF_RI, OFF_RG = 1024, 1280, 1536, 1792
OFF_SZ, OFF_XBC = 2048, 2304
OFF_LR = 3072
OFF_DTX = 3200
W_PROJ = 3456

PT_POOL_SCALE, PT_GBIAS, PT_GNORM, PT_HNORM, PT_CONVB, PT_DTB, PT_ALOG, PT_DSKIP, PT_SNORM = range(9)
PT_CONVW = 9
PT_LBL = 13
PT_ROWS = 16

VMEM_LIMIT_BYTES = 56 * 1024 * 1024
SUBLANES = 8

ROW_TILE = 512
FF_CHUNKS = (0, 768, 1536, 2304, D_FF)
PROMPT_CHUNK = 64
PROMPT_GROUP = 2
SAMPLE_GROUP = 4
SUB_BLOCK = 16
MAX_FACTORED_SPAN = 60.0
POOL_BASE = 24


def _sigmoid(x):
    return 1.0 / (1.0 + jnp.exp(-x))


def _silu(x):
    return x * _sigmoid(x)


def _log_sigmoid(x):
    return jnp.minimum(x, 0.0) - jnp.log(1.0 + jnp.exp(-jnp.abs(x)))


def _softplus(x):
    return jnp.maximum(x, 0.0) + jnp.log(1.0 + jnp.exp(-jnp.abs(x)))


def _rms(x, g):
    ms = jnp.mean(x * x, axis=-1, keepdims=True)
    return x * lax.rsqrt(ms + EPS) * g


def _dot(a, b):
    return jnp.dot(a.astype(BF16), b.astype(BF16), preferred_element_type=F32)


def _dot_nt(a, b):
    return lax.dot_general(a.astype(BF16), b.astype(BF16), (((1,), (1,)), ((), ())),
                           preferred_element_type=F32)


def _dot_tn(a, b):
    return lax.dot_general(a.astype(BF16), b.astype(BF16), (((0,), (0,)), ((), ())),
                           preferred_element_type=F32)


def _split3(x):
    hi = x.astype(BF16)
    r1 = x - hi.astype(F32)
    mid = r1.astype(BF16)
    lo = (r1 - mid.astype(F32)).astype(BF16)
    return hi, mid, lo


def _dot_exact_rhs(a01, x):
    hi, mid, lo = _split3(x)
    a = a01.astype(BF16)
    return (jnp.dot(a, hi, preferred_element_type=F32) + jnp.dot(a, mid, preferred_element_type=F32)
            + jnp.dot(a, lo, preferred_element_type=F32))


def _dot_exact_lhs(x, b01):
    hi, mid, lo = _split3(x)
    b = b01.astype(BF16)
    return (jnp.dot(hi, b, preferred_element_type=F32) + jnp.dot(mid, b, preferred_element_type=F32)
            + jnp.dot(lo, b, preferred_element_type=F32))


def _swiglu_half(x, g, wg_ref, wu_ref, wd_ref):
    h = _rms(x, g).astype(BF16)
    acc = None
    for lo, hi in zip(FF_CHUNKS[:-1], FF_CHUNKS[1:]):
        sl = slice(lo, hi)
        gate = jnp.dot(h, wg_ref[:, sl], preferred_element_type=F32)
        up = jnp.dot(h, wu_ref[:, sl], preferred_element_type=F32)
        act = (_silu(gate) * up).astype(BF16)
        d = jnp.dot(act, wd_ref[sl, :], preferred_element_type=F32)
        acc = d if acc is None else acc + d
    return x + 0.5 * acc


def _pre_kernel(*refs, n_first, split_in):
    if split_in:
        x = jnp.where(pl.program_id(0) < n_first, refs[0][...].reshape(ROW_TILE, D_MODEL), refs[1][...])
        refs = refs[2:]
    else:
        x = refs[0][...]
        refs = refs[1:]
    g1_ref, wg_ref, wu_ref, wd_ref, gm_ref, win_ref, x1_ref, proj_ref = refs
    x1 = _swiglu_half(x, g1_ref[...], wg_ref, wu_ref, wd_ref)
    x1_ref[...] = x1
    h = _rms(x1, gm_ref[...]).astype(BF16)
    proj_ref[...] = jnp.dot(h, win_ref[...], preferred_element_type=F32)


def _post_kernel(x_ref, cat_ref, wout_ref, g2_ref, wg_ref, wu_ref, wd_ref, gf_ref, *o_refs, n_first, final):
    x2 = x_ref[...] + jnp.dot(cat_ref[...].astype(BF16), wout_ref[...], preferred_element_type=F32)
    x3 = _swiglu_half(x2, g2_ref[...], wg_ref, wu_ref, wd_ref)
    if final:
        y = _rms(x3, gf_ref[...])
        i = pl.program_id(0)

        @pl.when(i < n_first)
        def _():
            o_refs[0][...] = y.reshape(o_refs[0].shape)

        @pl.when(i >= n_first)
        def _():
            o_refs[1][...] = y
    else:
        o_refs[0][...] = x3


def _layer_spec(shape, layer):
    nd = len(shape)
    return pl.BlockSpec((None,) + tuple(shape), lambda *_: (layer,) + (0,) * nd, pipeline_mode=pl.Buffered(1))


def _dense_params():
    return pltpu.CompilerParams(dimension_semantics=("arbitrary",), vmem_limit_bytes=VMEM_LIMIT_BYTES)


def _row_spec(width):
    return pl.BlockSpec((ROW_TILE, width), lambda i: (i, 0))


def _token_specs(n_first, n_prompt_seq):
    return [pl.BlockSpec((n_prompt_seq, PROMPT_CHUNK, D_MODEL), lambda i: (0, jnp.minimum(i, n_first - 1), 0)),
            pl.BlockSpec((ROW_TILE, D_MODEL), lambda i: (jnp.maximum(i - n_first, 0), 0))]


def _pre_call(xs, g1, wg, wu, wd, gm, win, layer, rows, n_first):
    split_in = len(xs) == 2
    x_specs = _token_specs(n_first, xs[0].shape[0]) if split_in else [_row_spec(D_MODEL)]
    return pl.pallas_call(
        functools.partial(_pre_kernel, n_first=n_first, split_in=split_in),
        grid=(rows // ROW_TILE,),
        in_specs=x_specs + [_layer_spec((1, D_MODEL), layer), _layer_spec((D_MODEL, D_FF), layer),
                            _layer_spec((D_MODEL, D_FF), layer), _layer_spec((D_FF, D_MODEL), layer),
                            _layer_spec((1, D_MODEL), layer), _layer_spec((D_MODEL, W_PROJ), layer)],
        out_specs=[_row_spec(D_MODEL), _row_spec(W_PROJ)],
        out_shape=[jax.ShapeDtypeStruct((rows, D_MODEL), F32), jax.ShapeDtypeStruct((rows, W_PROJ), F32)],
        compiler_params=_dense_params(),
        name="pre_ffn_inproj",
    )(*xs, g1, wg, wu, wd, gm, win)


def _post_call(x, cat, wout, g2, wg, wu, wd, gf, layer, n_first, final_shapes):
    rows = x.shape[0]
    if final_shapes is not None:
        out_specs = _token_specs(n_first, final_shapes[0][0])
        out_shape = [jax.ShapeDtypeStruct(s, F32) for s in final_shapes]
    else:
        out_specs = [_row_spec(D_MODEL)]
        out_shape = [jax.ShapeDtypeStruct((rows, D_MODEL), F32)]
    return pl.pallas_call(
        functools.partial(_post_kernel, n_first=n_first, final=final_shapes is not None),
        grid=(rows // ROW_TILE,),
        in_specs=[_row_spec(D_MODEL), _row_spec(D_MODEL), _layer_spec((D_MODEL, D_MODEL), layer),
                  _layer_spec((1, D_MODEL), layer), _layer_spec((D_MODEL, D_FF), layer),
                  _layer_spec((D_MODEL, D_FF), layer), _layer_spec((D_FF, D_MODEL), layer),
                  pl.BlockSpec((1, D_MODEL), lambda i: (0, 0))],
        out_specs=out_specs,
        out_shape=out_shape,
        compiler_params=_dense_params(),
        name="post_outproj_ffn",
    )(x, cat, wout, g2, wg, wu, wd, gf)


def _lane_group(shape, width):
    return lax.broadcasted_iota(jnp.int32, shape, len(shape) - 1) // width


def _block_diag_mask(rows, row_w, cols, col_w):
    r = lax.broadcasted_iota(jnp.int32, (rows, cols), 0) // row_w
    c = lax.broadcasted_iota(jnp.int32, (rows, cols), 1) // col_w
    return r == c


def _tril(n):
    r = lax.broadcasted_iota(jnp.int32, (n, n), 0)
    c = lax.broadcasted_iota(jnp.int32, (n, n), 1)
    return r >= c


def _select_heads(stacked, n_heads, rows, head_w):
    grp = _lane_group((rows, n_heads * head_w), head_w)
    out = jnp.zeros((rows, n_heads * head_w), F32)
    for h in range(n_heads):
        out = jnp.where(grp == h, stacked[h * rows:(h + 1) * rows], out)
    return out


def _cum_log_decay(la):
    return _dot_exact_rhs(_tril(la.shape[0]).astype(F32), la)


def _decay_span(b):
    mid = b.shape[0] // 2 - 1
    return jnp.max(jnp.abs(b - b[mid:mid + 1, :]))


def _la_factored(q, k, v, b, st, n_heads, dk):
    chunk = q.shape[0]
    nk, nv = n_heads * dk, n_heads * 64
    mid = chunk // 2 - 1
    m = b[mid:mid + 1, :]
    qt = q * jnp.exp(b - m)
    kt = k * jnp.exp(m - b)
    k_bd = jnp.where(_block_diag_mask(n_heads * chunk, chunk, nk, dk), jnp.concatenate([kt] * n_heads, axis=0), 0.0)
    att = _dot_nt(qt, k_bd)
    row = lax.broadcasted_iota(jnp.int32, att.shape, 0)
    col = lax.broadcasted_iota(jnp.int32, att.shape, 1) % chunk
    att = jnp.where(row >= col, att, 0.0)
    v_bd = jnp.where(_block_diag_mask(n_heads * chunk, chunk, nv, 64), jnp.concatenate([v] * n_heads, axis=0), 0.0)
    o = _dot(att, v_bd) + _dot_nt(qt * jnp.exp(m), st)
    b_last = b[chunk - 1:chunk, :]
    upd = _dot_tn(v, kt * jnp.exp(b_last - m))
    st_new = st * jnp.exp(b_last) + jnp.where(_block_diag_mask(nv, 64, nk, dk), upd, 0.0)
    return o, st_new


def _la_direct(q, k, v, b, st, n_heads, dk, sub):
    chunk = q.shape[0]
    nk = n_heads * dk
    nv = n_heads * 64
    n_sub = chunk // sub
    refs = [None] + [b[sub * i - 1:sub * i, :] for i in range(1, n_sub)]
    c_loc = jnp.concatenate([b[0:sub]] + [b[sub * i:sub * (i + 1)] - refs[i] for i in range(1, n_sub)], axis=0) \
        if n_sub > 1 else b
    o = _dot_nt(q * jnp.exp(b), st)

    qt = q * jnp.exp(c_loc)
    kgrp = _lane_group((sub, nk), dk)
    outs = []
    for i in range(n_sub):
        rows = slice(sub * i, sub * (i + 1))
        o_i = o[rows]
        if i > 0:
            prev = slice(0, sub * i)
            kt = k[prev] * jnp.exp(refs[i] - b[prev])
            q_stack = jnp.concatenate([jnp.where(kgrp == h, qt[rows], 0.0) for h in range(n_heads)], axis=0)
            att = _dot_nt(q_stack, kt)
            o_i = o_i + _select_heads(_dot(att, v[prev]), n_heads, sub, 64)
        c_i = c_loc[rows]
        q_i = q[rows]
        row_id = lax.broadcasted_iota(jnp.int32, (sub, nk), 0)
        e_list = []
        for j in range(sub):
            c_j = c_loc[sub * i + j:sub * i + j + 1, :]
            k_j = k[sub * i + j:sub * i + j + 1, :]
            dec = jnp.exp(jnp.where(row_id >= j, c_i - c_j, -jnp.inf))
            e_list.append(dec * q_i * k_j)
        e_all = jnp.concatenate(e_list, axis=0)
        expand = _block_diag_mask(nk, dk, nv, 64).astype(F32)
        r_d = _dot(e_all, expand)
        for j in range(sub):
            o_i = o_i + r_d[j * sub:(j + 1) * sub] * v[sub * i + j:sub * i + j + 1, :]
        outs.append(o_i)
    o = jnp.concatenate(outs, axis=0) if n_sub > 1 else outs[0]

    b_last = b[chunk - 1:chunk, :]
    k_end = k * jnp.exp(b_last - b)
    upd = _dot_tn(v, k_end)
    st_new = st * jnp.exp(b_last) + jnp.where(_block_diag_mask(nv, 64, nk, dk), upd, 0.0)
    return o, st_new


def _load_state_t(s, n_heads, dk):
    st = s.T
    tiled = jnp.concatenate([st] * n_heads, axis=0)
    return jnp.where(_block_diag_mask(n_heads * 64, 64, n_heads * dk, dk), tiled, 0.0)


def _store_state_t(st, n_heads, dk):
    acc = st[0:64]
    for h in range(1, n_heads):
        acc = acc + st[64 * h:64 * (h + 1)]
    return acc.T


def _segment_rms(o, width, gain):
    n = o.shape[-1]
    seg = _block_diag_mask(n, width, n, width).astype(F32)
    ms = _dot_exact_lhs(o * o, seg) * (1.0 / width)
    return o * lax.rsqrt(ms + EPS) * gain


def _pool_mixer(proj_ref, pool_scr, poolw_ref, scale, chunk, pos_first):
    xp = proj_ref[:, OFF_PX:OFF_PX + C_MIX]
    pool_scr[0, POOL_BASE:POOL_BASE + chunk, :] = xp
    n_ext = POOL_BASE - SUBLANES + chunk
    for s, shift in enumerate((1, 2, 4)):
        cur = pool_scr[s, SUBLANES:SUBLANES + n_ext, :]
        prev = pool_scr[s, SUBLANES - shift:SUBLANES - shift + n_ext, :]
        pool_scr[s + 1, SUBLANES:SUBLANES + n_ext, :] = cur + prev
    s2 = pool_scr[1, POOL_BASE:POOL_BASE + chunk, :]
    s4 = pool_scr[2, POOL_BASE:POOL_BASE + chunk, :]
    s8 = pool_scr[3, POOL_BASE:POOL_BASE + chunk, :]
    s16 = s8 + pool_scr[3, POOL_BASE - 8:POOL_BASE - 8 + chunk, :]
    grp = _lane_group((chunk, C_MIX), POOL_GC)
    win_sum = jnp.where(grp == 0, s2, jnp.where(grp == 1, s4, jnp.where(grp == 2, s8, s16)))
    win = jnp.where(grp == 0, 2, jnp.where(grp == 1, 4, jnp.where(grp == 2, 8, 16)))
    pos = pos_first + lax.broadcasted_iota(jnp.int32, (chunk, C_MIX), 0)
    cnt = jnp.minimum(pos + 1, win).astype(F32)
    d = win_sum / cnt - xp
    o_pool = _dot(d, poolw_ref[...]) * scale
    new_pool = pool_scr[0, POOL_BASE + chunk - POOL_PAST:POOL_BASE + chunk, :]
    pool_scr[0, POOL_BASE - POOL_PAST:POOL_BASE, :] = new_pool
    return o_pool, new_pool


def _ssd_mixer(proj_ref, conv_scr, ssm_st, prow, snorm_of, chunk):
    conv_base = SUBLANES
    conv_scr[conv_base:conv_base + chunk, :] = proj_ref[:, OFF_XBC:OFF_XBC + SSM_CONV_DIM]
    conv = None
    for w in range(SSM_CONV):
        start = conv_base - (SSM_CONV - 1) + w
        term = conv_scr[start:start + chunk, :] * prow(PT_CONVW + w, SSM_CONV_DIM)
        conv = term if conv is None else conv + term
    conv = _silu(conv + prow(PT_CONVB, SSM_CONV_DIM))
    new_conv = conv_scr[conv_base + chunk - (SSM_CONV - 1):conv_base + chunk, :]
    conv_scr[conv_base - (SSM_CONV - 1):conv_base, :] = new_conv

    xs = conv[:, 0:C_MIX]
    dt = _softplus(proj_ref[:, OFF_DTX:OFF_DTX + C_MIX] + prow(PT_DTB, C_MIX))
    a_neg = -jnp.exp(prow(PT_ALOG, C_MIX))
    la = dt * a_neg
    xdt = xs * dt
    cum = _cum_log_decay(la)
    heads_per_group = SSM_H // SSM_G
    b_of = lambda g: conv[:, C_MIX + SSM_N * g:C_MIX + SSM_N * (g + 1)]
    c_of = lambda g: conv[:, C_MIX + SSM_G * SSM_N + SSM_N * g:C_MIX + SSM_G * SSM_N + SSM_N * (g + 1)]
    if chunk == SSM_P:
        row = lax.broadcasted_iota(jnp.int32, (chunk, C_MIX), 0)
        col = lax.broadcasted_iota(jnp.int32, (chunk, C_MIX), 1) % chunk
        cum_j = jnp.sum(jnp.where(row == col, cum, 0.0), axis=0, keepdims=True)
        dec = jnp.exp(jnp.where(row >= col, cum - cum_j, -jnp.inf))
        cb = jnp.concatenate([_dot_nt(c_of(g), jnp.concatenate([b_of(g)] * heads_per_group, axis=0))
                              for g in range(SSM_G)], axis=1)
        x_bd = jnp.where(_block_diag_mask(SSM_H * chunk, chunk, C_MIX, SSM_P),
                         jnp.concatenate([xdt] * SSM_H, axis=0), 0.0)
        y = _dot(cb * dec, x_bd)
    else:
        cum_t = cum.T
        causal = _tril(chunk)
        att_rows = []
        for h in range(SSM_H):
            if h % heads_per_group == 0:
                cb = _dot_nt(c_of(h // heads_per_group), b_of(h // heads_per_group))
            col = cum[:, SSM_P * h:SSM_P * h + 1]
            row = cum_t[SSM_P * h:SSM_P * h + 1, :]
            att_rows.append(cb * jnp.exp(jnp.where(causal, col - row, -jnp.inf)))
        y = _select_heads(_dot(jnp.concatenate(att_rows, axis=0), xdt), SSM_H, chunk, SSM_P)
    cum_last = cum[chunk - 1:chunk, :]
    decay_in = jnp.exp(cum)
    xw = xdt * jnp.exp(cum_last - cum)
    decay_state = jnp.exp(cum_last)
    y_state = []
    for g in range(SSM_G):
        lanes = slice(SSM_N * g, SSM_N * (g + 1))
        st = ssm_st[g]
        y_state.append(_dot(c_of(g), st))
        ssm_st[g] = st * decay_state[:, lanes] + _dot_tn(b_of(g), xw[:, lanes])
    y = y + jnp.concatenate(y_state, axis=1) * decay_in + prow(PT_DSKIP, C_MIX) * xs
    y = y * _silu(proj_ref[:, OFF_SZ:OFF_SZ + C_MIX])
    o_ssm = jnp.concatenate([_rms(y[:, 128 * g:128 * (g + 1)], snorm_of(g)) for g in range(SSM_G)], axis=1)
    return o_ssm, new_conv


def _mixer_kernel(*refs, chunk, group, pos0, layer, carried, n_alias):
    proj_all = refs[0]
    n_in = 1 + (5 if carried else 0)
    poolw_ref, wgate_ref, ptab_ref = refs[n_in:n_in + 3]
    n_in += 3 + n_alias
    cat_all, npool_ref, ngla_ref, nhg_ref, nssm_ref, nconv_ref = refs[n_in:n_in + 6]
    pool_scr, conv_scr, gla_st, hg_st, ssm_st = refs[n_in + 6:]
    prow = lambda r, w: ptab_ref[r:r + 1, 0:w]
    snorm_of = lambda g: ptab_ref[PT_SNORM:PT_SNORM + 1, 128 * g:128 * (g + 1)]
    c = pl.program_id(1)
    n_c = pl.num_programs(1)
    sub = min(SUB_BLOCK, chunk)
    conv_base = SUBLANES

    @pl.when(c == 0)
    def _():
        pool_scr[...] = jnp.zeros(pool_scr.shape, F32)
        conv_scr[...] = jnp.zeros(conv_scr.shape, F32)
        if carried:
            poolb_ref, glas_ref, hgs_ref, ssms_ref, convb_ref = refs[1:6]
            for s in range(group):
                pool_scr[s, 0, POOL_BASE - POOL_PAST:POOL_BASE, :] = poolb_ref[s]
                conv_scr[s, conv_base - (SSM_CONV - 1):conv_base, :] = convb_ref[s]
                gla_st[s] = _load_state_t(glas_ref[s], GLA_H, GLA_DK)
                hg_st[s] = _load_state_t(hgs_ref[s], HG_H, HG_DK)
                for g in range(SSM_G):
                    ssm_st[s, g] = ssms_ref[s, g].T
        else:
            gla_st[...] = jnp.zeros(gla_st.shape, F32)
            hg_st[...] = jnp.zeros(hg_st.shape, F32)
            ssm_st[...] = jnp.zeros(ssm_st.shape, F32)

    lbl = ptab_ref[PT_LBL:PT_LBL + DEPTH, 0:C_MIX]
    lexp = jnp.exp(lbl - jnp.max(lbl, axis=0, keepdims=True))
    lsum = jnp.sum(lexp, axis=0, keepdims=True)
    lb = jnp.zeros((1, C_MIX), F32)
    for m in range(1, layer + 1):
        lb = lb + lexp[m:m + 1] / lsum
    log_lb, log_1m_lb = jnp.log(lb), jnp.log(1.0 - lb)

    new_pools, new_convs, gates, operands, spans = [], [], [], [], []
    for s in range(group):
        proj_ref = proj_all.at[pl.ds(s * chunk, chunk)]
        cat_ref = cat_all.at[pl.ds(s * chunk, chunk)]
        o_pool, new_pool = _pool_mixer(proj_ref, pool_scr.at[s], poolw_ref, prow(PT_POOL_SCALE, C_MIX), chunk,
                                       pos0 + c * chunk)
        o_ssm, new_conv = _ssd_mixer(proj_ref, conv_scr.at[s], ssm_st.at[s], prow, snorm_of, chunk)
        cat_ref[:, 0:C_MIX] = o_pool
        cat_ref[:, 3 * C_MIX:4 * C_MIX] = o_ssm
        new_pools.append(new_pool)
        new_convs.append(new_conv)

        gq = proj_ref[:, OFF_GQ:OFF_GQ + 128] * (GLA_DK ** -0.5)
        gk = proj_ref[:, OFF_GK:OFF_GK + 128]
        gv = proj_ref[:, OFF_GV:OFF_GV + C_MIX]
        logit = _dot(proj_ref[:, OFF_LR:OFF_LR + 128], wgate_ref[...]) + prow(PT_GBIAS, 128)
        b_gla = _cum_log_decay(_log_sigmoid(logit) * (1.0 / GLA_TAU))
        t2 = log_1m_lb + _log_sigmoid(proj_ref[:, OFF_RF:OFF_RF + C_MIX])
        log_f = jnp.maximum(log_lb, t2) + jnp.log(1.0 + jnp.exp(-jnp.abs(log_lb - t2)))
        hk = 1.0 - jnp.exp(log_f)
        hq = _silu(proj_ref[:, OFF_RQ:OFF_RQ + C_MIX])
        ri = proj_ref[:, OFF_RI:OFF_RI + C_MIX]
        b_hg = _cum_log_decay(log_f)
        operands += [gq, gk, gv, b_gla, gla_st[s], hq, hk, ri, b_hg, hg_st[s]]
        spans += [_decay_span(b_gla), _decay_span(b_hg)]

    narrow = functools.reduce(jnp.maximum, spans) <= MAX_FACTORED_SPAN
    per_seq = len(operands) // group

    def run(la_gla, la_hg, *ops):
        outs = ()
        for s in range(group):
            gq, gk, gv, b_gla, st_g, hq, hk, ri, b_hg, st_h = ops[s * per_seq:(s + 1) * per_seq]
            outs += la_gla(gq, gk, gv, b_gla, st_g) + la_hg(hq, hk, ri, b_hg, st_h)
        return outs

    factored = functools.partial(run, functools.partial(_la_factored, n_heads=GLA_H, dk=GLA_DK),
                                 functools.partial(_la_factored, n_heads=HG_H, dk=HG_DK))
    direct = functools.partial(run, functools.partial(_la_direct, n_heads=GLA_H, dk=GLA_DK, sub=sub),
                               functools.partial(_la_direct, n_heads=HG_H, dk=HG_DK, sub=sub))
    results = lax.cond(narrow, factored, direct, *operands)

    for s in range(group):
        proj_ref = proj_all.at[pl.ds(s * chunk, chunk)]
        cat_ref = cat_all.at[pl.ds(s * chunk, chunk)]
        o_g, st_g, o_h, st_h = results[4 * s:4 * s + 4]
        gla_st[s] = st_g
        hg_st[s] = st_h
        cat_ref[:, C_MIX:2 * C_MIX] = (_segment_rms(o_g, GLA_DV, prow(PT_GNORM, C_MIX))
                                       * _silu(proj_ref[:, OFF_GR:OFF_GR + C_MIX]))
        cat_ref[:, 2 * C_MIX:3 * C_MIX] = (_segment_rms(o_h, HG_DV, prow(PT_HNORM, C_MIX))
                                           * _silu(proj_ref[:, OFF_RG:OFF_RG + C_MIX]))

    @pl.when(c == n_c - 1)
    def _():
        for s in range(group):
            npool_ref[s] = new_pools[s]
            nconv_ref[s] = new_convs[s]
            ngla_ref[s] = _store_state_t(gla_st[s], GLA_H, GLA_DK)
            nhg_ref[s] = _store_state_t(hg_st[s], HG_H, HG_DK)
            for g in range(SSM_G):
                nssm_ref[s, g] = ssm_st[s, g].T


def _mixer_call(proj, cat_prev, carried, prev_states, mats, *, n_seq, seq_len, row0, chunk_major, chunk, group,
                pos0, layer):
    rows = proj.shape[0]
    n_c, n_b, blk = seq_len // chunk, n_seq // group, group * chunk
    assert chunk_major or n_c == 1
    row_map = lambda b, c: (row0 // blk + c * n_b + b, 0)
    state_tails = [(POOL_PAST, C_MIX), (GLA_H * GLA_DK, GLA_DV), (HG_H * HG_DK, HG_DV), (SSM_G, 128, SSM_N),
                   (SSM_CONV - 1, SSM_CONV_DIM)]

    def state_spec(tail):
        nd = len(tail)
        return pl.BlockSpec((None, group) + tail, lambda b, c: (layer, b) + (0,) * nd)

    def mat_spec(m):
        nd = m.ndim - 1
        return pl.BlockSpec((None,) + m.shape[1:], lambda b, c: (layer,) + (0,) * nd)

    state_specs = [state_spec(t) for t in state_tails]
    in_specs = [pl.BlockSpec((blk, W_PROJ), row_map)]
    args = [proj]
    if carried is not None:
        in_specs += state_specs
        args += list(carried)
    in_specs += [mat_spec(m) for m in mats]
    args += list(mats)
    aliases = {}
    if cat_prev is not None:
        aliases[len(args)] = 0
        args.append(cat_prev)
    if prev_states is not None:
        for i, s in enumerate(prev_states):
            aliases[len(args)] = 1 + i
            args.append(s)
    in_specs += [pl.BlockSpec(memory_space=pl.ANY)] * len(aliases)
    out_shape = [jax.ShapeDtypeStruct((rows, D_MODEL), F32)] + [
        jax.ShapeDtypeStruct((DEPTH, n_seq) + t, F32) for t in state_tails]
    out_specs = [pl.BlockSpec((blk, D_MODEL), row_map)] + state_specs
    scratch = [pltpu.VMEM((group, 4, POOL_BASE + chunk, C_MIX), F32),
               pltpu.VMEM((group, SUBLANES + chunk, SSM_CONV_DIM), F32),
               pltpu.VMEM((group, GLA_H * GLA_DV, GLA_H * GLA_DK), F32),
               pltpu.VMEM((group, HG_H * HG_DV, HG_H * HG_DK), F32),
               pltpu.VMEM((group, SSM_G, SSM_N, 128), F32)]
    return pl.pallas_call(
        functools.partial(_mixer_kernel, chunk=chunk, group=group, pos0=pos0, layer=layer,
                          carried=carried is not None, n_alias=len(aliases)),
        grid=(n_b, n_c),
        in_specs=in_specs,
        out_specs=out_specs,
        out_shape=out_shape,
        scratch_shapes=scratch,
        input_output_aliases=aliases,
        compiler_params=pltpu.CompilerParams(dimension_semantics=("arbitrary", "arbitrary"),
                                             vmem_limit_bytes=VMEM_LIMIT_BYTES),
        name=f"mixer_chunk{chunk}",
    )(*args)


def _arrange_w_in(w):
    lr = w[..., 1024:1040]
    sdt = w[..., 3088:3092]
    pad = jnp.zeros(w.shape[:-1] + (128 - GLA_RANK - SSM_H,), w.dtype)
    return jnp.concatenate([w[..., :1024], w[..., 1040:3088], lr, sdt, pad, jnp.repeat(sdt, SSM_P, axis=-1)], axis=-1)


def _pool_block_diag(pw):
    eye = jnp.eye(len(POOL_WINDOWS), dtype=pw.dtype)
    return jnp.einsum('lgcd,gh->lgchd', pw, eye).reshape(pw.shape[0], C_MIX, C_MIX)


def _param_table(pool_scale, gla_gate_bias, gla_norm, hgrn_norm, ssm_conv_b, ssm_dt_bias, ssm_A_log, ssm_D,
                 ssm_norm, ssm_conv_w, hgrn_lb_logits):
    def rows(v):
        v = v.astype(F32).reshape(DEPTH, -1, v.shape[-1])
        return jnp.pad(v, ((0, 0), (0, 0), (0, SSM_CONV_DIM - v.shape[-1])))
    lbl = jnp.broadcast_to(hgrn_lb_logits[None], (DEPTH,) + hgrn_lb_logits.shape)
    parts = [rows(pool_scale), rows(gla_gate_bias), rows(jnp.tile(gla_norm, (1, GLA_H))),
             rows(jnp.tile(hgrn_norm, (1, HG_H))), rows(ssm_conv_b), rows(jnp.repeat(ssm_dt_bias, SSM_P, axis=-1)),
             rows(jnp.repeat(ssm_A_log, SSM_P, axis=-1)), rows(jnp.repeat(ssm_D, SSM_P, axis=-1)), rows(ssm_norm),
             rows(ssm_conv_w), rows(lbl)]
    tab = jnp.concatenate(parts, axis=1)
    return jnp.pad(tab, ((0, 0), (0, PT_ROWS - tab.shape[1]), (0, 0)))


def kernel(x_prompt, x_sample, state_pool, state_gla, state_hgrn, state_ssm, state_conv, ffn1_norm, ffn1_w_gate, ffn1_w_up, ffn1_w_down, mix_norm, w_in, pool_w, pool_scale, gla_w_gate, gla_gate_bias, gla_norm, hgrn_lb_logits, hgrn_norm, ssm_conv_w, ssm_conv_b, ssm_dt_bias, ssm_A_log, ssm_D, ssm_norm, w_out, ffn2_norm, ffn2_w_gate, ffn2_w_up, ffn2_w_down, final_norm):
    bp, tp, _ = x_prompt.shape
    bs, ts, _ = x_sample.shape
    rows_p, rows_s = bp * tp, bs * ts
    gain = lambda v: v.reshape(DEPTH, 1, D_MODEL).astype(F32)
    bf = lambda w: w.astype(BF16)

    wgate = jnp.pad(gla_w_gate, ((0, 0), (0, 128 - GLA_RANK), (0, 0)))
    mats = [bf(_pool_block_diag(pool_w)), bf(wgate),
            _param_table(pool_scale, gla_gate_bias, gla_norm, hgrn_norm, ssm_conv_b, ssm_dt_bias, ssm_A_log, ssm_D,
                         ssm_norm, ssm_conv_w, hgrn_lb_logits)]
    carried = (state_pool, state_gla.reshape(DEPTH, bs, GLA_H * GLA_DK, GLA_DV),
               state_hgrn.reshape(DEPTH, bs, HG_H * HG_DK, HG_DV), state_ssm.reshape(DEPTH, bs, SSM_G, 128, SSM_N),
               state_conv)
    ffn1 = (gain(ffn1_norm), bf(ffn1_w_gate), bf(ffn1_w_up), bf(ffn1_w_down))
    ffn2 = (gain(ffn2_norm), bf(ffn2_w_gate), bf(ffn2_w_up), bf(ffn2_w_down))
    gm, win, wout = gain(mix_norm), bf(_arrange_w_in(w_in)), bf(w_out)
    gf = final_norm.reshape(1, D_MODEL).astype(F32)

    assert bp * PROMPT_CHUNK == ROW_TILE and rows_s % ROW_TILE == 0
    n_first = rows_p // ROW_TILE
    xs = [x_prompt, x_sample.reshape(rows_s, D_MODEL)]
    states_p = states_s = None
    for l in range(DEPTH):
        x1, proj = _pre_call(xs, *ffn1, gm, win, l, rows_p + rows_s, n_first)
        outs_p = _mixer_call(proj, None, None, states_p, mats, n_seq=bp, seq_len=tp, row0=0, chunk_major=True,
                             chunk=PROMPT_CHUNK, group=PROMPT_GROUP, pos0=0, layer=l)
        outs_s = _mixer_call(proj, outs_p[0], carried, states_s, mats, n_seq=bs, seq_len=ts, row0=rows_p,
                             chunk_major=False, chunk=ts, group=SAMPLE_GROUP, pos0=PAST_LEN, layer=l)
        states_p, states_s = outs_p[1:], outs_s[1:]
        final_shapes = ((bp, tp, D_MODEL), (rows_s, D_MODEL)) if l == DEPTH - 1 else None
        xs = _post_call(x1, outs_s[0], wout, *ffn2, gf, l, n_first, final_shapes)

    outs = [xs[0], xs[1].reshape(bs, ts, D_MODEL)]
    for i, tail in enumerate([(POOL_PAST, C_MIX), (GLA_H, GLA_DK, GLA_DV), (HG_H, HG_DK, HG_DV),
                              (SSM_H, SSM_P, SSM_N), (SSM_CONV - 1, SSM_CONV_DIM)]):
        outs.append(states_p[i].reshape((DEPTH, bp) + tail))
        outs.append(states_s[i].reshape((DEPTH, bs) + tail))
    return tuple(outs)
```

```python
import functools

import jax
import jax.numpy as jnp
from jax import lax
from jax.experimental import pallas as pl
from jax.experimental.pallas import tpu as pltpu

F32 = jnp.float32
BF16 = jnp.bfloat16

D_MODEL = 1024
D_FF = 2816
DEPTH = 2
EPS = 1e-6
PAST_LEN = 16384

C_MIX = 256
POOL_WINDOWS = (2, 4, 8, 16)
POOL_GC = 64
POOL_PAST = 15
GLA_H, GLA_DK, GLA_DV, GLA_RANK, GLA_TAU = 4, 32, 64, 16, 16.0
HG_H, HG_DK, HG_DV = 4, 64, 64
SSM_H, SSM_P, SSM_G, SSM_N, SSM_CONV = 4, 64, 2, 128, 4
SSM_CONV_DIM = 768
N_IN = 3092

OFF_PX, OFF_GQ, OFF_GK, OFF_GV, OFF_GR = 0, 256, 384, 512, 768
OFF_RQ, OFF_RF, OFF_RI, OFF_RG = 1024, 1280, 1536, 1792
OFF_SZ, OFF_XBC = 2048, 2304
OFF_LR = 3072
OFF_DTX = 3200
W_PROJ = 3456

PT_POOL_SCALE, PT_GBIAS, PT_GNORM, PT_HNORM, PT_CONVB, PT_DTB, PT_ALOG, PT_DSKIP, PT_SNORM = range(9)
PT_CONVW = 9
PT_LBL = 13
PT_ROWS = 16

VMEM_LIMIT_BYTES = 56 * 1024 * 1024
SUBLANES = 8

ROW_TILE = 512
FF_CHUNKS = (0, 768, 1536, 2304, D_FF)
PROMPT_CHUNK = 64
PROMPT_GROUP = 4
SAMPLE_GROUP = 8
SUB_BLOCK = 16
MAX_FACTORED_SPAN = 60.0
POOL_BASE = 24


def _sigmoid(x):
    return 1.0 / (1.0 + jnp.exp(-x))


def _silu(x):
    return x * _sigmoid(x)


def _log_sigmoid(x):
    return jnp.minimum(x, 0.0) - jnp.log(1.0 + jnp.exp(-jnp.abs(x)))


def _softplus(x):
    return jnp.maximum(x, 0.0) + jnp.log(1.0 + jnp.exp(-jnp.abs(x)))


def _rms(x, g):
    ms = jnp.mean(x * x, axis=-1, keepdims=True)
    return x * lax.rsqrt(ms + EPS) * g


def _dot(a, b):
    return jnp.dot(a.astype(BF16), b.astype(BF16), preferred_element_type=F32)


def _dot_nt(a, b):
    return lax.dot_general(a.astype(BF16), b.astype(BF16), (((1,), (1,)), ((), ())),
                           preferred_element_type=F32)


def _dot_tn(a, b):
    return lax.dot_general(a.astype(BF16), b.astype(BF16), (((0,), (0,)), ((), ())),
                           preferred_element_type=F32)


def _split3(x):
    hi = x.astype(BF16)
    r1 = x - hi.astype(F32)
    mid = r1.astype(BF16)
    lo = (r1 - mid.astype(F32)).astype(BF16)
    return hi, mid, lo


def _dot_exact_rhs(a01, x):
    hi, mid, lo = _split3(x)
    a = a01.astype(BF16)
    return (jnp.dot(a, hi, preferred_element_type=F32) + jnp.dot(a, mid, preferred_element_type=F32)
            + jnp.dot(a, lo, preferred_element_type=F32))


def _dot_exact_lhs(x, b01):
    hi, mid, lo = _split3(x)
    b = b01.astype(BF16)
    return (jnp.dot(hi, b, preferred_element_type=F32) + jnp.dot(mid, b, preferred_element_type=F32)
            + jnp.dot(lo, b, preferred_element_type=F32))


def _swiglu_half(x, g, wg_ref, wu_ref, wd_ref):
    h = _rms(x, g).astype(BF16)
    acc = None
    for lo, hi in zip(FF_CHUNKS[:-1], FF_CHUNKS[1:]):
        sl = slice(lo, hi)
        gate = jnp.dot(h, wg_ref[:, sl], preferred_element_type=F32)
        up = jnp.dot(h, wu_ref[:, sl], preferred_element_type=F32)
        act = (_silu(gate) * up).astype(BF16)
        d = jnp.dot(act, wd_ref[sl, :], preferred_element_type=F32)
        acc = d if acc is None else acc + d
    return x + 0.5 * acc


def _pre_kernel(*refs, n_first, split_in, layer):
    if split_in:
        x = jnp.where(pl.program_id(0) < n_first, refs[0][...].reshape(ROW_TILE, D_MODEL), refs[1][...])
        refs = refs[2:]
    else:
        x = refs[0][...]
        refs = refs[1:]
    g1_ref, wg_ref, wu_ref, wd_ref, gm_ref, win_ref, wgate_ref, ptab_ref, x1_ref, feat_ref = refs
    x1 = _swiglu_half(x, g1_ref[...], wg_ref, wu_ref, wd_ref)
    x1_ref[...] = x1
    h = _rms(x1, gm_ref[...]).astype(BF16)
    _mixer_features(h, win_ref, wgate_ref, ptab_ref, feat_ref, layer)


def _hgrn_lower_bound(ptab_ref, layer):
    lbl = ptab_ref[PT_LBL:PT_LBL + DEPTH, 0:C_MIX]
    lexp = jnp.exp(lbl - jnp.max(lbl, axis=0, keepdims=True))
    lsum = jnp.sum(lexp, axis=0, keepdims=True)
    lb = jnp.zeros((1, C_MIX), F32)
    for m in range(1, layer + 1):
        lb = lb + lexp[m:m + 1] / lsum
    return lb


def _mixer_features(h, win_ref, wgate_ref, ptab_ref, feat_ref, layer):
    prow = lambda r, w: ptab_ref[r:r + 1, 0:w]
    proj = lambda lo, hi: jnp.dot(h, win_ref[:, lo:hi], preferred_element_type=F32)
    a = proj(OFF_PX, OFF_RQ)
    feat_ref[:, OFF_PX:OFF_GQ] = a[:, OFF_PX:OFF_GQ]
    feat_ref[:, OFF_GQ:OFF_GK] = a[:, OFF_GQ:OFF_GK] * (GLA_DK ** -0.5)
    feat_ref[:, OFF_GK:OFF_GR] = a[:, OFF_GK:OFF_GR]
    feat_ref[:, OFF_GR:OFF_RQ] = _silu(a[:, OFF_GR:OFF_RQ])
    b = proj(OFF_RQ, OFF_XBC)
    feat_ref[:, OFF_RQ:OFF_RF] = _silu(b[:, 0:C_MIX])
    lb = _hgrn_lower_bound(ptab_ref, layer)
    log_lb = jnp.log(lb)
    t2 = jnp.log(1.0 - lb) + _log_sigmoid(b[:, C_MIX:2 * C_MIX])
    feat_ref[:, OFF_RF:OFF_RI] = jnp.maximum(log_lb, t2) + jnp.log(1.0 + jnp.exp(-jnp.abs(log_lb - t2)))
    feat_ref[:, OFF_RI:OFF_RG] = b[:, 2 * C_MIX:3 * C_MIX]
    feat_ref[:, OFF_RG:OFF_XBC] = _silu(b[:, 3 * C_MIX:5 * C_MIX])
    feat_ref[:, OFF_XBC:OFF_LR] = proj(OFF_XBC, OFF_LR)
    d = proj(OFF_LR, W_PROJ)
    logit = _dot(d[:, 0:128], wgate_ref[...]) + prow(PT_GBIAS, 128)
    feat_ref[:, OFF_LR:OFF_DTX] = _log_sigmoid(logit) * (1.0 / GLA_TAU)
    feat_ref[:, OFF_DTX:W_PROJ] = _softplus(d[:, 128:128 + C_MIX] + prow(PT_DTB, C_MIX))


def _segment_rms(o, width, gain):
    n = o.shape[-1]
    seg = (lax.broadcasted_iota(jnp.int32, (n, n), 0) // width
           == lax.broadcasted_iota(jnp.int32, (n, n), 1) // width).astype(BF16)
    sq = o * o
    hi = sq.astype(BF16)
    lo = (sq - hi.astype(F32)).astype(BF16)
    ms = (jnp.dot(hi, seg, preferred_element_type=F32) + jnp.dot(lo, seg, preferred_element_type=F32)) * (1.0 / width)
    return o * lax.rsqrt(ms + EPS) * gain


def _mixer_outputs(cat_ref, ggla_ref, ghg_ref, gz_ref, ptab_ref):
    prow = lambda r, w: ptab_ref[r:r + 1, 0:w]
    o_gla = _segment_rms(cat_ref[:, C_MIX:2 * C_MIX], GLA_DV, prow(PT_GNORM, C_MIX)) * ggla_ref[...]
    o_hg = _segment_rms(cat_ref[:, 2 * C_MIX:3 * C_MIX], HG_DV, prow(PT_HNORM, C_MIX)) * ghg_ref[...]
    y = cat_ref[:, 3 * C_MIX:4 * C_MIX] * gz_ref[...]
    half = C_MIX // SSM_G
    o_ssm = [_rms(y[:, half * g:half * (g + 1)], ptab_ref[PT_SNORM:PT_SNORM + 1, half * g:half * (g + 1)])
             for g in range(SSM_G)]
    return jnp.concatenate([cat_ref[:, 0:C_MIX], o_gla, o_hg] + o_ssm, axis=1)


def _post_kernel(x_ref, cat_ref, ggla_ref, ghg_ref, gz_ref, ptab_ref, wout_ref, g2_ref, wg_ref, wu_ref, wd_ref,
                 gf_ref, *o_refs, n_first, final):
    mixed = _mixer_outputs(cat_ref, ggla_ref, ghg_ref, gz_ref, ptab_ref)
    x2 = x_ref[...] + jnp.dot(mixed.astype(BF16), wout_ref[...], preferred_element_type=F32)
    x3 = _swiglu_half(x2, g2_ref[...], wg_ref, wu_ref, wd_ref)
    if final:
        y = _rms(x3, gf_ref[...])
        i = pl.program_id(0)

        @pl.when(i < n_first)
        def _():
            o_refs[0][...] = y.reshape(o_refs[0].shape)

        @pl.when(i >= n_first)
        def _():
            o_refs[1][...] = y
    else:
        o_refs[0][...] = x3


def _layer_spec(shape, layer):
    nd = len(shape)
    return pl.BlockSpec((None,) + tuple(shape), lambda *_: (layer,) + (0,) * nd, pipeline_mode=pl.Buffered(1))


def _dense_params():
    return pltpu.CompilerParams(dimension_semantics=("arbitrary",), vmem_limit_bytes=VMEM_LIMIT_BYTES)


def _row_spec(width):
    return pl.BlockSpec((ROW_TILE, width), lambda i: (i, 0))


def _token_specs(n_first, n_prompt_seq):
    return [pl.BlockSpec((n_prompt_seq, PROMPT_CHUNK, D_MODEL), lambda i: (0, jnp.minimum(i, n_first - 1), 0)),
            pl.BlockSpec((ROW_TILE, D_MODEL), lambda i: (jnp.maximum(i - n_first, 0), 0))]


def _pre_call(xs, g1, wg, wu, wd, gm, win, wgate, ptab, layer, rows, n_first):
    split_in = len(xs) == 2
    x_specs = _token_specs(n_first, xs[0].shape[0]) if split_in else [_row_spec(D_MODEL)]
    return pl.pallas_call(
        functools.partial(_pre_kernel, n_first=n_first, split_in=split_in, layer=layer),
        grid=(rows // ROW_TILE,),
        in_specs=x_specs + [_layer_spec((1, D_MODEL), layer), _layer_spec((D_MODEL, D_FF), layer),
                            _layer_spec((D_MODEL, D_FF), layer), _layer_spec((D_FF, D_MODEL), layer),
                            _layer_spec((1, D_MODEL), layer), _layer_spec((D_MODEL, W_PROJ), layer),
                            _layer_spec(wgate.shape[1:], layer), _layer_spec(ptab.shape[1:], layer)],
        out_specs=[_row_spec(D_MODEL), _row_spec(W_PROJ)],
        out_shape=[jax.ShapeDtypeStruct((rows, D_MODEL), F32), jax.ShapeDtypeStruct((rows, W_PROJ), F32)],
        compiler_params=_dense_params(),
        name="pre_ffn_inproj",
    )(*xs, g1, wg, wu, wd, gm, win, wgate, ptab)


def _post_call(x, cat, feat, ptab, wout, g2, wg, wu, wd, gf, layer, n_first, final_shapes):
    rows = x.shape[0]
    if final_shapes is not None:
        out_specs = _token_specs(n_first, final_shapes[0][0])
        out_shape = [jax.ShapeDtypeStruct(s, F32) for s in final_shapes]
    else:
        out_specs = [_row_spec(D_MODEL)]
        out_shape = [jax.ShapeDtypeStruct((rows, D_MODEL), F32)]
    gate_spec = lambda off: pl.BlockSpec((ROW_TILE, C_MIX), lambda i: (i, off // C_MIX))
    return pl.pallas_call(
        functools.partial(_post_kernel, n_first=n_first, final=final_shapes is not None),
        grid=(rows // ROW_TILE,),
        in_specs=[_row_spec(D_MODEL), _row_spec(D_MODEL), gate_spec(OFF_GR), gate_spec(OFF_RG), gate_spec(OFF_SZ),
                  _layer_spec(ptab.shape[1:], layer), _layer_spec((D_MODEL, D_MODEL), layer),
                  _layer_spec((1, D_MODEL), layer), _layer_spec((D_MODEL, D_FF), layer),
                  _layer_spec((D_MODEL, D_FF), layer), _layer_spec((D_FF, D_MODEL), layer),
                  pl.BlockSpec((1, D_MODEL), lambda i: (0, 0))],
        out_specs=out_specs,
        out_shape=out_shape,
        compiler_params=_dense_params(),
        name="post_outproj_ffn",
    )(x, cat, feat, feat, feat, ptab, wout, g2, wg, wu, wd, gf)


def _lane_group(shape, width):
    return lax.broadcasted_iota(jnp.int32, shape, len(shape) - 1) // width


def _block_diag_mask(rows, row_w, cols, col_w):
    r = lax.broadcasted_iota(jnp.int32, (rows, cols), 0) // row_w
    c = lax.broadcasted_iota(jnp.int32, (rows, cols), 1) // col_w
    return r == c


def _tril(n):
    r = lax.broadcasted_iota(jnp.int32, (n, n), 0)
    c = lax.broadcasted_iota(jnp.int32, (n, n), 1)
    return r >= c


def _select_heads(stacked, n_heads, rows, head_w):
    grp = _lane_group((rows, n_heads * head_w), head_w)
    out = jnp.zeros((rows, n_heads * head_w), F32)
    for h in range(n_heads):
        out = jnp.where(grp == h, stacked[h * rows:(h + 1) * rows], out)
    return out


def _cum_log_decay(la):
    return _dot_exact_rhs(_tril(la.shape[0]).astype(F32), la)


def _decay_span(b):
    mid = b.shape[0] // 2 - 1
    return jnp.max(jnp.abs(b - b[mid:mid + 1, :]))


def _la_factored(q, k, v, b, st, n_heads, dk):
    chunk = q.shape[0]
    nk, nv = n_heads * dk, n_heads * 64
    mid = chunk // 2 - 1
    m = b[mid:mid + 1, :]
    qt = q * jnp.exp(b - m)
    kt = k * jnp.exp(m - b)
    k_bd = jnp.where(_block_diag_mask(n_heads * chunk, chunk, nk, dk), jnp.concatenate([kt] * n_heads, axis=0), 0.0)
    att = _dot_nt(qt, k_bd)
    row = lax.broadcasted_iota(jnp.int32, att.shape, 0)
    col = lax.broadcasted_iota(jnp.int32, att.shape, 1) % chunk
    att = jnp.where(row >= col, att, 0.0)
    v_bd = jnp.where(_block_diag_mask(n_heads * chunk, chunk, nv, 64), jnp.concatenate([v] * n_heads, axis=0), 0.0)
    o = _dot(att, v_bd) + _dot_nt(qt * jnp.exp(m), st)
    b_last = b[chunk - 1:chunk, :]
    upd = _dot_tn(v, kt * jnp.exp(b_last - m))
    st_new = st * jnp.exp(b_last) + jnp.where(_block_diag_mask(nv, 64, nk, dk), upd, 0.0)
    return o, st_new


def _la_direct(q, k, v, b, st, n_heads, dk, sub):
    chunk = q.shape[0]
    nk = n_heads * dk
    nv = n_heads * 64
    n_sub = chunk // sub
    refs = [None] + [b[sub * i - 1:sub * i, :] for i in range(1, n_sub)]
    c_loc = jnp.concatenate([b[0:sub]] + [b[sub * i:sub * (i + 1)] - refs[i] for i in range(1, n_sub)], axis=0) \
        if n_sub > 1 else b
    o = _dot_nt(q * jnp.exp(b), st)

    qt = q * jnp.exp(c_loc)
    kgrp = _lane_group((sub, nk), dk)
    outs = []
    for i in range(n_sub):
        rows = slice(sub * i, sub * (i + 1))
        o_i = o[rows]
        if i > 0:
            prev = slice(0, sub * i)
            kt = k[prev] * jnp.exp(refs[i] - b[prev])
            q_stack = jnp.concatenate([jnp.where(kgrp == h, qt[rows], 0.0) for h in range(n_heads)], axis=0)
            att = _dot_nt(q_stack, kt)
            o_i = o_i + _select_heads(_dot(att, v[prev]), n_heads, sub, 64)
        c_i = c_loc[rows]
        q_i = q[rows]
        row_id = lax.broadcasted_iota(jnp.int32, (sub, nk), 0)
        e_list = []
        for j in range(sub):
            c_j = c_loc[sub * i + j:sub * i + j + 1, :]
            k_j = k[sub * i + j:sub * i + j + 1, :]
            dec = jnp.exp(jnp.where(row_id >= j, c_i - c_j, -jnp.inf))
            e_list.append(dec * q_i * k_j)
        e_all = jnp.concatenate(e_list, axis=0)
        expand = _block_diag_mask(nk, dk, nv, 64).astype(F32)
        r_d = _dot(e_all, expand)
        for j in range(sub):
            o_i = o_i + r_d[j * sub:(j + 1) * sub] * v[sub * i + j:sub * i + j + 1, :]
        outs.append(o_i)
    o = jnp.concatenate(outs, axis=0) if n_sub > 1 else outs[0]

    b_last = b[chunk - 1:chunk, :]
    k_end = k * jnp.exp(b_last - b)
    upd = _dot_tn(v, k_end)
    st_new = st * jnp.exp(b_last) + jnp.where(_block_diag_mask(nv, 64, nk, dk), upd, 0.0)
    return o, st_new


def _load_state_t(s, n_heads, dk):
    st = s.T
    tiled = jnp.concatenate([st] * n_heads, axis=0)
    return jnp.where(_block_diag_mask(n_heads * 64, 64, n_heads * dk, dk), tiled, 0.0)


def _store_state_t(st, n_heads, dk):
    acc = st[0:64]
    for h in range(1, n_heads):
        acc = acc + st[64 * h:64 * (h + 1)]
    return acc.T


def _pool_mixer(proj_ref, pool_scr, poolw_ref, scale, chunk, pos_first):
    xp = proj_ref[:, OFF_PX:OFF_PX + C_MIX]
    pool_scr[0, POOL_BASE:POOL_BASE + chunk, :] = xp
    n_ext = POOL_BASE - SUBLANES + chunk
    for s, shift in enumerate((1, 2, 4)):
        cur = pool_scr[s, SUBLANES:SUBLANES + n_ext, :]
        prev = pool_scr[s, SUBLANES - shift:SUBLANES - shift + n_ext, :]
        pool_scr[s + 1, SUBLANES:SUBLANES + n_ext, :] = cur + prev
    s2 = pool_scr[1, POOL_BASE:POOL_BASE + chunk, :]
    s4 = pool_scr[2, POOL_BASE:POOL_BASE + chunk, :]
    s8 = pool_scr[3, POOL_BASE:POOL_BASE + chunk, :]
    s16 = s8 + pool_scr[3, POOL_BASE - 8:POOL_BASE - 8 + chunk, :]
    grp = _lane_group((chunk, C_MIX), POOL_GC)
    win_sum = jnp.where(grp == 0, s2, jnp.where(grp == 1, s4, jnp.where(grp == 2, s8, s16)))
    win = jnp.where(grp == 0, 2, jnp.where(grp == 1, 4, jnp.where(grp == 2, 8, 16)))
    pos = pos_first + lax.broadcasted_iota(jnp.int32, (chunk, C_MIX), 0)
    cnt = jnp.minimum(pos + 1, win).astype(F32)
    d = win_sum / cnt - xp
    o_pool = _dot(d, poolw_ref[...]) * scale
    new_pool = pool_scr[0, POOL_BASE + chunk - POOL_PAST:POOL_BASE + chunk, :]
    pool_scr[0, POOL_BASE - POOL_PAST:POOL_BASE, :] = new_pool
    return o_pool, new_pool


def _ssd_mixer(proj_ref, conv_scr, ssm_st, prow, chunk):
    conv_base = SUBLANES
    conv_scr[conv_base:conv_base + chunk, :] = proj_ref[:, OFF_XBC:OFF_XBC + SSM_CONV_DIM]
    conv = None
    for w in range(SSM_CONV):
        start = conv_base - (SSM_CONV - 1) + w
        term = conv_scr[start:start + chunk, :] * prow(PT_CONVW + w, SSM_CONV_DIM)
        conv = term if conv is None else conv + term
    conv = _silu(conv + prow(PT_CONVB, SSM_CONV_DIM))
    new_conv = conv_scr[conv_base + chunk - (SSM_CONV - 1):conv_base + chunk, :]
    conv_scr[conv_base - (SSM_CONV - 1):conv_base, :] = new_conv

    xs = conv[:, 0:C_MIX]
    dt = proj_ref[:, OFF_DTX:OFF_DTX + C_MIX]
    a_neg = -jnp.exp(prow(PT_ALOG, C_MIX))
    la = dt * a_neg
    xdt = xs * dt
    cum = _cum_log_decay(la)
    heads_per_group = SSM_H // SSM_G
    b_of = lambda g: conv[:, C_MIX + SSM_N * g:C_MIX + SSM_N * (g + 1)]
    c_of = lambda g: conv[:, C_MIX + SSM_G * SSM_N + SSM_N * g:C_MIX + SSM_G * SSM_N + SSM_N * (g + 1)]
    if chunk == SSM_P:
        row = lax.broadcasted_iota(jnp.int32, (chunk, C_MIX), 0)
        col = lax.broadcasted_iota(jnp.int32, (chunk, C_MIX), 1) % chunk
        cum_j = jnp.sum(jnp.where(row == col, cum, 0.0), axis=0, keepdims=True)
        dec = jnp.exp(jnp.where(row >= col, cum - cum_j, -jnp.inf))
        cb = jnp.concatenate([_dot_nt(c_of(g), jnp.concatenate([b_of(g)] * heads_per_group, axis=0))
                              for g in range(SSM_G)], axis=1)
        x_bd = jnp.where(_block_diag_mask(SSM_H * chunk, chunk, C_MIX, SSM_P),
                         jnp.concatenate([xdt] * SSM_H, axis=0), 0.0)
        y = _dot(cb * dec, x_bd)
    else:
        cum_t = cum.T
        causal = _tril(chunk)
        att_rows = []
        for h in range(SSM_H):
            if h % heads_per_group == 0:
                cb = _dot_nt(c_of(h // heads_per_group), b_of(h // heads_per_group))
            col = cum[:, SSM_P * h:SSM_P * h + 1]
            row = cum_t[SSM_P * h:SSM_P * h + 1, :]
            att_rows.append(cb * jnp.exp(jnp.where(causal, col - row, -jnp.inf)))
        y = _select_heads(_dot(jnp.concatenate(att_rows, axis=0), xdt), SSM_H, chunk, SSM_P)
    cum_last = cum[chunk - 1:chunk, :]
    decay_in = jnp.exp(cum)
    xw = xdt * jnp.exp(cum_last - cum)
    decay_state = jnp.exp(cum_last)
    y_state = []
    for g in range(SSM_G):
        lanes = slice(SSM_N * g, SSM_N * (g + 1))
        st = ssm_st[g]
        y_state.append(_dot(c_of(g), st))
        ssm_st[g] = st * decay_state[:, lanes] + _dot_tn(b_of(g), xw[:, lanes])
    y = y + jnp.concatenate(y_state, axis=1) * decay_in + prow(PT_DSKIP, C_MIX) * xs
    return y, new_conv


def _mixer_kernel(*refs, chunk, group, pos0, layer, carried, n_alias):
    feat_all = refs[0]
    n_in = 1 + (5 if carried else 0)
    poolw_ref, ptab_ref = refs[n_in:n_in + 2]
    n_in += 2 + n_alias
    cat_all, npool_ref, ngla_ref, nhg_ref, nssm_ref, nconv_ref = refs[n_in:n_in + 6]
    pool_scr, conv_scr, gla_st, hg_st, ssm_st = refs[n_in + 6:]
    prow = lambda r, w: ptab_ref[r:r + 1, 0:w]
    c = pl.program_id(1)
    n_c = pl.num_programs(1)
    sub = min(SUB_BLOCK, chunk)
    conv_base = SUBLANES

    @pl.when(c == 0)
    def _():
        pool_scr[...] = jnp.zeros(pool_scr.shape, F32)
        conv_scr[...] = jnp.zeros(conv_scr.shape, F32)
        if carried:
            poolb_ref, glas_ref, hgs_ref, ssms_ref, convb_ref = refs[1:6]
            for s in range(group):
                pool_scr[s, 0, POOL_BASE - POOL_PAST:POOL_BASE, :] = poolb_ref[s]
                conv_scr[s, conv_base - (SSM_CONV - 1):conv_base, :] = convb_ref[s]
                gla_st[s] = _load_state_t(glas_ref[s], GLA_H, GLA_DK)
                hg_st[s] = _load_state_t(hgs_ref[s], HG_H, HG_DK)
                for g in range(SSM_G):
                    ssm_st[s, g] = ssms_ref[s, g].T
        else:
            gla_st[...] = jnp.zeros(gla_st.shape, F32)
            hg_st[...] = jnp.zeros(hg_st.shape, F32)
            ssm_st[...] = jnp.zeros(ssm_st.shape, F32)

    seq_rows = lambda ref, s: ref.at[pl.ds(s * chunk, chunk)]
    new_pools, new_convs, cum_decays, spans = [], [], [], []
    for s in range(group):
        feat_ref, cat_ref = seq_rows(feat_all, s), seq_rows(cat_all, s)
        o_pool, new_pool = _pool_mixer(feat_ref, pool_scr.at[s], poolw_ref, prow(PT_POOL_SCALE, C_MIX), chunk,
                                       pos0 + c * chunk)
        y_ssm, new_conv = _ssd_mixer(feat_ref, conv_scr.at[s], ssm_st.at[s], prow, chunk)
        cat_ref[:, 0:C_MIX] = o_pool
        cat_ref[:, 3 * C_MIX:4 * C_MIX] = y_ssm
        new_pools.append(new_pool)
        new_convs.append(new_conv)
        b_gla = _cum_log_decay(feat_ref[:, OFF_LR:OFF_LR + GLA_H * GLA_DK])
        b_hg = _cum_log_decay(feat_ref[:, OFF_RF:OFF_RF + C_MIX])
        cum_decays += [b_gla, b_hg]
        spans += [_decay_span(b_gla), _decay_span(b_hg)]

    narrow = functools.reduce(jnp.maximum, spans) <= MAX_FACTORED_SPAN

    def run(la_gla, la_hg, *cums):
        for s in range(group):
            feat_ref, cat_ref = seq_rows(feat_all, s), seq_rows(cat_all, s)
            o_g, st_g = la_gla(feat_ref[:, OFF_GQ:OFF_GK], feat_ref[:, OFF_GK:OFF_GV], feat_ref[:, OFF_GV:OFF_GR],
                               cums[2 * s], gla_st[s])
            gla_st[s] = st_g
            cat_ref[:, C_MIX:2 * C_MIX] = o_g
            hk = 1.0 - jnp.exp(feat_ref[:, OFF_RF:OFF_RI])
            o_h, st_h = la_hg(feat_ref[:, OFF_RQ:OFF_RF], hk, feat_ref[:, OFF_RI:OFF_RG], cums[2 * s + 1], hg_st[s])
            hg_st[s] = st_h
            cat_ref[:, 2 * C_MIX:3 * C_MIX] = o_h

    factored = functools.partial(run, functools.partial(_la_factored, n_heads=GLA_H, dk=GLA_DK),
                                 functools.partial(_la_factored, n_heads=HG_H, dk=HG_DK))
    direct = functools.partial(run, functools.partial(_la_direct, n_heads=GLA_H, dk=GLA_DK, sub=sub),
                               functools.partial(_la_direct, n_heads=HG_H, dk=HG_DK, sub=sub))
    lax.cond(narrow, factored, direct, *cum_decays)

    @pl.when(c == n_c - 1)
    def _():
        for s in range(group):
            npool_ref[s] = new_pools[s]
            nconv_ref[s] = new_convs[s]
            ngla_ref[s] = _store_state_t(gla_st[s], GLA_H, GLA_DK)
            nhg_ref[s] = _store_state_t(hg_st[s], HG_H, HG_DK)
            for g in range(SSM_G):
                nssm_ref[s, g] = ssm_st[s, g].T


def _mixer_call(proj, cat_prev, carried, prev_states, mats, *, n_seq, seq_len, row0, chunk_major, chunk, group,
                pos0, layer):
    rows = proj.shape[0]
    n_c, n_b, blk = seq_len // chunk, n_seq // group, group * chunk
    assert chunk_major or n_c == 1
    row_map = lambda b, c: (row0 // blk + c * n_b + b, 0)
    state_tails = [(POOL_PAST, C_MIX), (GLA_H * GLA_DK, GLA_DV), (HG_H * HG_DK, HG_DV), (SSM_G, 128, SSM_N),
                   (SSM_CONV - 1, SSM_CONV_DIM)]

    def state_spec(tail):
        nd = len(tail)
        return pl.BlockSpec((None, group) + tail, lambda b, c: (layer, b) + (0,) * nd)

    def mat_spec(m):
        nd = m.ndim - 1
        return pl.BlockSpec((None,) + m.shape[1:], lambda b, c: (layer,) + (0,) * nd)

    state_specs = [state_spec(t) for t in state_tails]
    in_specs = [pl.BlockSpec((blk, W_PROJ), row_map)]
    args = [proj]
    if carried is not None:
        in_specs += state_specs
        args += list(carried)
    in_specs += [mat_spec(m) for m in mats]
    args += list(mats)
    aliases = {}
    if cat_prev is not None:
        aliases[len(args)] = 0
        args.append(cat_prev)
    if prev_states is not None:
        for i, s in enumerate(prev_states):
            aliases[len(args)] = 1 + i
            args.append(s)
    in_specs += [pl.BlockSpec(memory_space=pl.ANY)] * len(aliases)
    out_shape = [jax.ShapeDtypeStruct((rows, D_MODEL), F32)] + [
        jax.ShapeDtypeStruct((DEPTH, n_seq) + t, F32) for t in state_tails]
    out_specs = [pl.BlockSpec((blk, D_MODEL), row_map)] + state_specs
    scratch = [pltpu.VMEM((group, 4, POOL_BASE + chunk, C_MIX), F32),
               pltpu.VMEM((group, SUBLANES + chunk, SSM_CONV_DIM), F32),
               pltpu.VMEM((group, GLA_H * GLA_DV, GLA_H * GLA_DK), F32),
               pltpu.VMEM((group, HG_H * HG_DV, HG_H * HG_DK), F32),
               pltpu.VMEM((group, SSM_G, SSM_N, 128), F32)]
    return pl.pallas_call(
        functools.partial(_mixer_kernel, chunk=chunk, group=group, pos0=pos0, layer=layer,
                          carried=carried is not None, n_alias=len(aliases)),
        grid=(n_b, n_c),
        in_specs=in_specs,
        out_specs=out_specs,
        out_shape=out_shape,
        scratch_shapes=scratch,
        input_output_aliases=aliases,
        compiler_params=pltpu.CompilerParams(dimension_semantics=("arbitrary", "arbitrary"),
                                             vmem_limit_bytes=VMEM_LIMIT_BYTES),
        name=f"mixer_chunk{chunk}",
    )(*args)


def _arrange_w_in(w):
    lr = w[..., 1024:1040]
    sdt = w[..., 3088:3092]
    pad = jnp.zeros(w.shape[:-1] + (128 - GLA_RANK - SSM_H,), w.dtype)
    return jnp.concatenate([w[..., :1024], w[..., 1040:3088], lr, sdt, pad, jnp.repeat(sdt, SSM_P, axis=-1)], axis=-1)


def _pool_block_diag(pw):
    eye = jnp.eye(len(POOL_WINDOWS), dtype=pw.dtype)
    return jnp.einsum('lgcd,gh->lgchd', pw, eye).reshape(pw.shape[0], C_MIX, C_MIX)


def _param_table(pool_scale, gla_gate_bias, gla_norm, hgrn_norm, ssm_conv_b, ssm_dt_bias, ssm_A_log, ssm_D,
                 ssm_norm, ssm_conv_w, hgrn_lb_logits):
    def rows(v):
        v = v.astype(F32).reshape(DEPTH, -1, v.shape[-1])
        return jnp.pad(v, ((0, 0), (0, 0), (0, SSM_CONV_DIM - v.shape[-1])))
    lbl = jnp.broadcast_to(hgrn_lb_logits[None], (DEPTH,) + hgrn_lb_logits.shape)
    parts = [rows(pool_scale), rows(gla_gate_bias), rows(jnp.tile(gla_norm, (1, GLA_H))),
             rows(jnp.tile(hgrn_norm, (1, HG_H))), rows(ssm_conv_b), rows(jnp.repeat(ssm_dt_bias, SSM_P, axis=-1)),
             rows(jnp.repeat(ssm_A_log, SSM_P, axis=-1)), rows(jnp.repeat(ssm_D, SSM_P, axis=-1)), rows(ssm_norm),
             rows(ssm_conv_w), rows(lbl)]
    tab = jnp.concatenate(parts, axis=1)
    return jnp.pad(tab, ((0, 0), (0, PT_ROWS - tab.shape[1]), (0, 0)))


def kernel(x_prompt, x_sample, state_pool, state_gla, state_hgrn, state_ssm, state_conv, ffn1_norm, ffn1_w_gate, ffn1_w_up, ffn1_w_down, mix_norm, w_in, pool_w, pool_scale, gla_w_gate, gla_gate_bias, gla_norm, hgrn_lb_logits, hgrn_norm, ssm_conv_w, ssm_conv_b, ssm_dt_bias, ssm_A_log, ssm_D, ssm_norm, w_out, ffn2_norm, ffn2_w_gate, ffn2_w_up, ffn2_w_down, final_norm):
    bp, tp, _ = x_prompt.shape
    bs, ts, _ = x_sample.shape
    rows_p, rows_s = bp * tp, bs * ts
    gain = lambda v: v.reshape(DEPTH, 1, D_MODEL).astype(F32)
    bf = lambda w: w.astype(BF16)

    wgate = bf(jnp.pad(gla_w_gate, ((0, 0), (0, 128 - GLA_RANK), (0, 0))))
    ptab = _param_table(pool_scale, gla_gate_bias, gla_norm, hgrn_norm, ssm_conv_b, ssm_dt_bias, ssm_A_log, ssm_D,
                        ssm_norm, ssm_conv_w, hgrn_lb_logits)
    mats = [bf(_pool_block_diag(pool_w)), ptab]
    carried = (state_pool, state_gla.reshape(DEPTH, bs, GLA_H * GLA_DK, GLA_DV),
               state_hgrn.reshape(DEPTH, bs, HG_H * HG_DK, HG_DV), state_ssm.reshape(DEPTH, bs, SSM_G, 128, SSM_N),
               state_conv)
    ffn1 = (gain(ffn1_norm), bf(ffn1_w_gate), bf(ffn1_w_up), bf(ffn1_w_down))
    ffn2 = (gain(ffn2_norm), bf(ffn2_w_gate), bf(ffn2_w_up), bf(ffn2_w_down))
    gm, win, wout = gain(mix_norm), bf(_arrange_w_in(w_in)), bf(w_out)
    gf = final_norm.reshape(1, D_MODEL).astype(F32)

    assert bp * PROMPT_CHUNK == ROW_TILE and rows_s % ROW_TILE == 0
    n_first = rows_p // ROW_TILE
    xs = [x_prompt, x_sample.reshape(rows_s, D_MODEL)]
    states_p = states_s = None
    for l in range(DEPTH):
        x1, feat = _pre_call(xs, *ffn1, gm, win, wgate, ptab, l, rows_p + rows_s, n_first)
        outs_p = _mixer_call(feat, None, None, states_p, mats, n_seq=bp, seq_len=tp, row0=0, chunk_major=True,
                             chunk=PROMPT_CHUNK, group=PROMPT_GROUP, pos0=0, layer=l)
        outs_s = _mixer_call(feat, outs_p[0], carried, states_s, mats, n_seq=bs, seq_len=ts, row0=rows_p,
                             chunk_major=False, chunk=ts, group=SAMPLE_GROUP, pos0=PAST_LEN, layer=l)
        states_p, states_s = outs_p[1:], outs_s[1:]
        final_shapes = ((bp, tp, D_MODEL), (rows_s, D_MODEL)) if l == DEPTH - 1 else None
        xs = _post_call(x1, outs_s[0], feat, ptab, wout, *ffn2, gf, l, n_first, final_shapes)

    outs = [xs[0], xs[1].reshape(bs, ts, D_MODEL)]
    for i, tail in enumerate([(POOL_PAST, C_MIX), (GLA_H, GLA_DK, GLA_DV), (HG_H, HG_DK, HG_DV),
                              (SSM_H, SSM_P, SSM_N), (SSM_CONV - 1, SSM_CONV_DIM)]):
        outs.append(states_p[i].reshape((DEPTH, bp) + tail))
        outs.append(states_s[i].reshape((DEPTH, bs) + tail))
    return tuple(outs)
```

```python
import functools

import jax
import jax.numpy as jnp
from jax import lax
from jax.experimental import pallas as pl
from jax.experimental.pallas import tpu as pltpu

F32 = jnp.float32
BF16 = jnp.bfloat16

D_MODEL = 1024
D_FF = 2816
DEPTH = 2
EPS = 1e-6
PAST_LEN = 16384

C_MIX = 256
POOL_WINDOWS = (2, 4, 8, 16)
POOL_GC = 64
POOL_PAST = 15
GLA_H, GLA_DK, GLA_DV, GLA_RANK, GLA_TAU = 4, 32, 64, 16, 16.0
HG_H, HG_DK, HG_DV = 4, 64, 64
SSM_H, SSM_P, SSM_G, SSM_N, SSM_CONV = 4, 64, 2, 128, 4
SSM_CONV_DIM = 768
N_IN = 3092

OFF_PX, OFF_GQ, OFF_GK, OFF_GV, OFF_GR = 0, 256, 384, 512, 768
OFF_RQ, OFF_RF, OFF_RI, OFF_RG = 1024, 1280, 1536, 1792
OFF_SZ, OFF_XBC = 2048, 2304
OFF_LR = 3072
OFF_DTX = 3200
W_PROJ = 3456

PT_POOL_SCALE, PT_GBIAS, PT_GNORM, PT_HNORM, PT_CONVB, PT_DTB, PT_ALOG, PT_DSKIP, PT_SNORM = range(9)
PT_CONVW = 9
PT_LBL = 13
PT_ROWS = 16

VMEM_LIMIT_BYTES = 56 * 1024 * 1024
SUBLANES = 8
LANES = 128
HEADS_PER_STEP = LANES // GLA_DV

ROW_TILE = 512
FF_CHUNKS = (0, 768, 1536, 2304, D_FF)
PROMPT_CHUNK = 64
PROMPT_GROUP = 4
SAMPLE_GROUP = 8
SUB_BLOCK = 16
MAX_FACTORED_SPAN = 60.0
POOL_BASE = 24


def _sigmoid(x):
    return 1.0 / (1.0 + jnp.exp(-x))


def _silu(x):
    return x * _sigmoid(x)


def _log_sigmoid(x):
    return jnp.minimum(x, 0.0) - jnp.log(1.0 + jnp.exp(-jnp.abs(x)))


def _softplus(x):
    return jnp.maximum(x, 0.0) + jnp.log(1.0 + jnp.exp(-jnp.abs(x)))


def _rms(x, g):
    ms = jnp.mean(x * x, axis=-1, keepdims=True)
    return x * lax.rsqrt(ms + EPS) * g


def _dot(a, b):
    return jnp.dot(a.astype(BF16), b.astype(BF16), preferred_element_type=F32)


def _dot_nt(a, b):
    return lax.dot_general(a.astype(BF16), b.astype(BF16), (((1,), (1,)), ((), ())),
                           preferred_element_type=F32)


def _dot_tn(a, b):
    return lax.dot_general(a.astype(BF16), b.astype(BF16), (((0,), (0,)), ((), ())),
                           preferred_element_type=F32)


def _split3(x):
    hi = x.astype(BF16)
    r1 = x - hi.astype(F32)
    mid = r1.astype(BF16)
    lo = (r1 - mid.astype(F32)).astype(BF16)
    return hi, mid, lo


def _dot_exact_rhs(a01, x):
    hi, mid, lo = _split3(x)
    a = a01.astype(BF16)
    return (jnp.dot(a, hi, preferred_element_type=F32) + jnp.dot(a, mid, preferred_element_type=F32)
            + jnp.dot(a, lo, preferred_element_type=F32))


def _dot_exact_lhs(x, b01):
    hi, mid, lo = _split3(x)
    b = b01.astype(BF16)
    return (jnp.dot(hi, b, preferred_element_type=F32) + jnp.dot(mid, b, preferred_element_type=F32)
            + jnp.dot(lo, b, preferred_element_type=F32))


def _swiglu_half(x, g, wg_ref, wu_ref, wd_ref):
    h = _rms(x, g).astype(BF16)
    acc = None
    for lo, hi in zip(FF_CHUNKS[:-1], FF_CHUNKS[1:]):
        sl = slice(lo, hi)
        gate = jnp.dot(h, wg_ref[:, sl], preferred_element_type=F32)
        up = jnp.dot(h, wu_ref[:, sl], preferred_element_type=F32)
        act = (_silu(gate) * up).astype(BF16)
        d = jnp.dot(act, wd_ref[sl, :], preferred_element_type=F32)
        acc = d if acc is None else acc + d
    return x + 0.5 * acc


def _pre_kernel(*refs, n_first, split_in, layer):
    if split_in:
        x = jnp.where(pl.program_id(0) < n_first, refs[0][...].reshape(ROW_TILE, D_MODEL), refs[1][...])
        refs = refs[2:]
    else:
        x = refs[0][...]
        refs = refs[1:]
    g1_ref, wg_ref, wu_ref, wd_ref, gm_ref, win_ref, wgate_ref, ptab_ref, x1_ref, feat_ref = refs
    x1 = _swiglu_half(x, g1_ref[...], wg_ref, wu_ref, wd_ref)
    x1_ref[...] = x1
    h = _rms(x1, gm_ref[...]).astype(BF16)
    _mixer_features(h, win_ref, wgate_ref, ptab_ref, feat_ref, layer)


def _hgrn_lower_bound(ptab_ref, layer):
    lbl = ptab_ref[PT_LBL:PT_LBL + DEPTH, 0:C_MIX]
    lexp = jnp.exp(lbl - jnp.max(lbl, axis=0, keepdims=True))
    lsum = jnp.sum(lexp, axis=0, keepdims=True)
    lb = jnp.zeros((1, C_MIX), F32)
    for m in range(1, layer + 1):
        lb = lb + lexp[m:m + 1] / lsum
    return lb


def _mixer_features(h, win_ref, wgate_ref, ptab_ref, feat_ref, layer):
    prow = lambda r, w: ptab_ref[r:r + 1, 0:w]
    proj = lambda lo, hi: jnp.dot(h, win_ref[:, lo:hi], preferred_element_type=F32)
    a = proj(OFF_PX, OFF_RQ)
    feat_ref[:, OFF_PX:OFF_GQ] = a[:, OFF_PX:OFF_GQ]
    feat_ref[:, OFF_GQ:OFF_GK] = a[:, OFF_GQ:OFF_GK] * (GLA_DK ** -0.5)
    feat_ref[:, OFF_GK:OFF_GR] = a[:, OFF_GK:OFF_GR]
    feat_ref[:, OFF_GR:OFF_RQ] = _silu(a[:, OFF_GR:OFF_RQ])
    b = proj(OFF_RQ, OFF_XBC)
    feat_ref[:, OFF_RQ:OFF_RF] = _silu(b[:, 0:C_MIX])
    lb = _hgrn_lower_bound(ptab_ref, layer)
    log_lb = jnp.log(lb)
    t2 = jnp.log(1.0 - lb) + _log_sigmoid(b[:, C_MIX:2 * C_MIX])
    feat_ref[:, OFF_RF:OFF_RI] = jnp.maximum(log_lb, t2) + jnp.log(1.0 + jnp.exp(-jnp.abs(log_lb - t2)))
    feat_ref[:, OFF_RI:OFF_RG] = b[:, 2 * C_MIX:3 * C_MIX]
    feat_ref[:, OFF_RG:OFF_XBC] = _silu(b[:, 3 * C_MIX:5 * C_MIX])
    feat_ref[:, OFF_XBC:OFF_LR] = proj(OFF_XBC, OFF_LR)
    d = proj(OFF_LR, W_PROJ)
    logit = _dot(d[:, 0:128], wgate_ref[...]) + prow(PT_GBIAS, 128)
    feat_ref[:, OFF_LR:OFF_DTX] = _log_sigmoid(logit) * (1.0 / GLA_TAU)
    feat_ref[:, OFF_DTX:W_PROJ] = _softplus(d[:, 128:128 + C_MIX] + prow(PT_DTB, C_MIX))


def _segment_rms(o, width, gain):
    n = o.shape[-1]
    seg = (lax.broadcasted_iota(jnp.int32, (n, n), 0) // width
           == lax.broadcasted_iota(jnp.int32, (n, n), 1) // width).astype(BF16)
    sq = o * o
    hi = sq.astype(BF16)
    lo = (sq - hi.astype(F32)).astype(BF16)
    ms = (jnp.dot(hi, seg, preferred_element_type=F32) + jnp.dot(lo, seg, preferred_element_type=F32)) * (1.0 / width)
    return o * lax.rsqrt(ms + EPS) * gain


def _mixer_outputs(cat_ref, ggla_ref, ghg_ref, gz_ref, ptab_ref):
    prow = lambda r, w: ptab_ref[r:r + 1, 0:w]
    o_gla = _segment_rms(cat_ref[:, C_MIX:2 * C_MIX], GLA_DV, prow(PT_GNORM, C_MIX)) * ggla_ref[...]
    o_hg = _segment_rms(cat_ref[:, 2 * C_MIX:3 * C_MIX], HG_DV, prow(PT_HNORM, C_MIX)) * ghg_ref[...]
    y = cat_ref[:, 3 * C_MIX:4 * C_MIX] * gz_ref[...]
    half = C_MIX // SSM_G
    o_ssm = [_rms(y[:, half * g:half * (g + 1)], ptab_ref[PT_SNORM:PT_SNORM + 1, half * g:half * (g + 1)])
             for g in range(SSM_G)]
    return jnp.concatenate([cat_ref[:, 0:C_MIX], o_gla, o_hg] + o_ssm, axis=1)


def _post_kernel(x_ref, cat_ref, ggla_ref, ghg_ref, gz_ref, ptab_ref, wout_ref, g2_ref, wg_ref, wu_ref, wd_ref,
                 gf_ref, *o_refs, n_first, final):
    mixed = _mixer_outputs(cat_ref, ggla_ref, ghg_ref, gz_ref, ptab_ref)
    x2 = x_ref[...] + jnp.dot(mixed.astype(BF16), wout_ref[...], preferred_element_type=F32)
    x3 = _swiglu_half(x2, g2_ref[...], wg_ref, wu_ref, wd_ref)
    if final:
        y = _rms(x3, gf_ref[...])
        i = pl.program_id(0)

        @pl.when(i < n_first)
        def _():
            o_refs[0][...] = y.reshape(o_refs[0].shape)

        @pl.when(i >= n_first)
        def _():
            o_refs[1][...] = y
    else:
        o_refs[0][...] = x3


def _layer_spec(shape, layer):
    nd = len(shape)
    return pl.BlockSpec((None,) + tuple(shape), lambda *_: (layer,) + (0,) * nd, pipeline_mode=pl.Buffered(1))


def _dense_params():
    return pltpu.CompilerParams(dimension_semantics=("arbitrary",), vmem_limit_bytes=VMEM_LIMIT_BYTES)


def _row_spec(width):
    return pl.BlockSpec((ROW_TILE, width), lambda i: (i, 0))


def _token_specs(n_first, n_prompt_seq):
    return [pl.BlockSpec((n_prompt_seq, PROMPT_CHUNK, D_MODEL), lambda i: (0, jnp.minimum(i, n_first - 1), 0)),
            pl.BlockSpec((ROW_TILE, D_MODEL), lambda i: (jnp.maximum(i - n_first, 0), 0))]


def _pre_call(xs, g1, wg, wu, wd, gm, win, wgate, ptab, layer, rows, n_first):
    split_in = len(xs) == 2
    x_specs = _token_specs(n_first, xs[0].shape[0]) if split_in else [_row_spec(D_MODEL)]
    return pl.pallas_call(
        functools.partial(_pre_kernel, n_first=n_first, split_in=split_in, layer=layer),
        grid=(rows // ROW_TILE,),
        in_specs=x_specs + [_layer_spec((1, D_MODEL), layer), _layer_spec((D_MODEL, D_FF), layer),
                            _layer_spec((D_MODEL, D_FF), layer), _layer_spec((D_FF, D_MODEL), layer),
                            _layer_spec((1, D_MODEL), layer), _layer_spec((D_MODEL, W_PROJ), layer),
                            _layer_spec(wgate.shape[1:], layer), _layer_spec(ptab.shape[1:], layer)],
        out_specs=[_row_spec(D_MODEL), _row_spec(W_PROJ)],
        out_shape=[jax.ShapeDtypeStruct((rows, D_MODEL), F32), jax.ShapeDtypeStruct((rows, W_PROJ), F32)],
        compiler_params=_dense_params(),
        name="pre_ffn_inproj",
    )(*xs, g1, wg, wu, wd, gm, win, wgate, ptab)


def _post_call(x, cat, feat, ptab, wout, g2, wg, wu, wd, gf, layer, n_first, final_shapes):
    rows = x.shape[0]
    if final_shapes is not None:
        out_specs = _token_specs(n_first, final_shapes[0][0])
        out_shape = [jax.ShapeDtypeStruct(s, F32) for s in final_shapes]
    else:
        out_specs = [_row_spec(D_MODEL)]
        out_shape = [jax.ShapeDtypeStruct((rows, D_MODEL), F32)]
    gate_spec = lambda off: pl.BlockSpec((ROW_TILE, C_MIX), lambda i: (i, off // C_MIX))
    return pl.pallas_call(
        functools.partial(_post_kernel, n_first=n_first, final=final_shapes is not None),
        grid=(rows // ROW_TILE,),
        in_specs=[_row_spec(D_MODEL), _row_spec(D_MODEL), gate_spec(OFF_GR), gate_spec(OFF_RG), gate_spec(OFF_SZ),
                  _layer_spec(ptab.shape[1:], layer), _layer_spec((D_MODEL, D_MODEL), layer),
                  _layer_spec((1, D_MODEL), layer), _layer_spec((D_MODEL, D_FF), layer),
                  _layer_spec((D_MODEL, D_FF), layer), _layer_spec((D_FF, D_MODEL), layer),
                  pl.BlockSpec((1, D_MODEL), lambda i: (0, 0))],
        out_specs=out_specs,
        out_shape=out_shape,
        compiler_params=_dense_params(),
        name="post_outproj_ffn",
    )(x, cat, feat, feat, feat, ptab, wout, g2, wg, wu, wd, gf)


def _lane_group(shape, width):
    return lax.broadcasted_iota(jnp.int32, shape, len(shape) - 1) // width


def _block_diag_mask(rows, row_w, cols, col_w):
    r = lax.broadcasted_iota(jnp.int32, (rows, cols), 0) // row_w
    c = lax.broadcasted_iota(jnp.int32, (rows, cols), 1) // col_w
    return r == c


def _tril(n):
    r = lax.broadcasted_iota(jnp.int32, (n, n), 0)
    c = lax.broadcasted_iota(jnp.int32, (n, n), 1)
    return r >= c


def _select_heads(stacked, n_heads, rows, head_w):
    grp = _lane_group((rows, n_heads * head_w), head_w)
    out = jnp.zeros((rows, n_heads * head_w), F32)
    for h in range(n_heads):
        out = jnp.where(grp == h, stacked[h * rows:(h + 1) * rows], out)
    return out


def _cum_log_decay(la):
    return _dot_exact_rhs(_tril(la.shape[0]).astype(F32), la)


def _decay_span(b):
    mid = b.shape[0] // 2 - 1
    return jnp.max(jnp.abs(b - b[mid:mid + 1, :]))


def _la_factored(q, k, v, b, st, n_heads, dk):
    chunk = q.shape[0]
    nk, nv = n_heads * dk, n_heads * 64
    mid = chunk // 2 - 1
    m = b[mid:mid + 1, :]
    qt = q * jnp.exp(b - m)
    kt = k * jnp.exp(m - b)
    k_bd = jnp.where(_block_diag_mask(n_heads * chunk, chunk, nk, dk), jnp.concatenate([kt] * n_heads, axis=0), 0.0)
    att = _dot_nt(qt, k_bd)
    row = lax.broadcasted_iota(jnp.int32, att.shape, 0)
    col = lax.broadcasted_iota(jnp.int32, att.shape, 1) % chunk
    att = jnp.where(row >= col, att, 0.0)
    v_bd = jnp.where(_block_diag_mask(n_heads * chunk, chunk, nv, 64), jnp.concatenate([v] * n_heads, axis=0), 0.0)
    o = _dot(att, v_bd) + _dot_nt(qt * jnp.exp(m), st)
    b_last = b[chunk - 1:chunk, :]
    upd = _dot_tn(v, kt * jnp.exp(b_last - m))
    st_new = st * jnp.exp(b_last) + jnp.where(_block_diag_mask(nv, 64, nk, dk), upd, 0.0)
    return o, st_new


def _la_direct(q, k, v, b, st, n_heads, dk, sub):
    chunk = q.shape[0]
    nk = n_heads * dk
    nv = n_heads * 64
    n_sub = chunk // sub
    refs = [None] + [b[sub * i - 1:sub * i, :] for i in range(1, n_sub)]
    c_loc = jnp.concatenate([b[0:sub]] + [b[sub * i:sub * (i + 1)] - refs[i] for i in range(1, n_sub)], axis=0) \
        if n_sub > 1 else b
    o = _dot_nt(q * jnp.exp(b), st)

    qt = q * jnp.exp(c_loc)
    kgrp = _lane_group((sub, nk), dk)
    outs = []
    for i in range(n_sub):
        rows = slice(sub * i, sub * (i + 1))
        o_i = o[rows]
        if i > 0:
            prev = slice(0, sub * i)
            kt = k[prev] * jnp.exp(refs[i] - b[prev])
            q_stack = jnp.concatenate([jnp.where(kgrp == h, qt[rows], 0.0) for h in range(n_heads)], axis=0)
            att = _dot_nt(q_stack, kt)
            o_i = o_i + _select_heads(_dot(att, v[prev]), n_heads, sub, 64)
        c_i = c_loc[rows]
        q_i = q[rows]
        row_id = lax.broadcasted_iota(jnp.int32, (sub, nk), 0)
        e_list = []
        for j in range(sub):
            c_j = c_loc[sub * i + j:sub * i + j + 1, :]
            k_j = k[sub * i + j:sub * i + j + 1, :]
            dec = jnp.exp(jnp.where(row_id >= j, c_i - c_j, -jnp.inf))
            e_list.append(dec * q_i * k_j)
        e_all = jnp.concatenate(e_list, axis=0)
        expand = _block_diag_mask(nk, dk, nv, 64).astype(F32)
        r_d = _dot(e_all, expand)
        for j in range(sub):
            o_i = o_i + r_d[j * sub:(j + 1) * sub] * v[sub * i + j:sub * i + j + 1, :]
        outs.append(o_i)
    o = jnp.concatenate(outs, axis=0) if n_sub > 1 else outs[0]

    b_last = b[chunk - 1:chunk, :]
    k_end = k * jnp.exp(b_last - b)
    upd = _dot_tn(v, k_end)
    st_new = st * jnp.exp(b_last) + jnp.where(_block_diag_mask(nv, 64, nk, dk), upd, 0.0)
    return o, st_new


def _load_state_t(s, n_heads, dk):
    st = s.T
    tiled = jnp.concatenate([st] * n_heads, axis=0)
    return jnp.where(_block_diag_mask(n_heads * 64, 64, n_heads * dk, dk), tiled, 0.0)


def _store_state_t(st, n_heads, dk):
    acc = st[0:64]
    for h in range(1, n_heads):
        acc = acc + st[64 * h:64 * (h + 1)]
    return acc.T


def _pool_mixer(proj_ref, pool_scr, poolw_ref, scale, chunk, pos_first):
    xp = proj_ref[:, OFF_PX:OFF_PX + C_MIX]
    pool_scr[0, POOL_BASE:POOL_BASE + chunk, :] = xp
    n_ext = POOL_BASE - SUBLANES + chunk
    for s, shift in enumerate((1, 2, 4)):
        cur = pool_scr[s, SUBLANES:SUBLANES + n_ext, :]
        prev = pool_scr[s, SUBLANES - shift:SUBLANES - shift + n_ext, :]
        pool_scr[s + 1, SUBLANES:SUBLANES + n_ext, :] = cur + prev
    s2 = pool_scr[1, POOL_BASE:POOL_BASE + chunk, :]
    s4 = pool_scr[2, POOL_BASE:POOL_BASE + chunk, :]
    s8 = pool_scr[3, POOL_BASE:POOL_BASE + chunk, :]
    s16 = s8 + pool_scr[3, POOL_BASE - 8:POOL_BASE - 8 + chunk, :]
    grp = _lane_group((chunk, C_MIX), POOL_GC)
    win_sum = jnp.where(grp == 0, s2, jnp.where(grp == 1, s4, jnp.where(grp == 2, s8, s16)))
    win = jnp.where(grp == 0, 2, jnp.where(grp == 1, 4, jnp.where(grp == 2, 8, 16)))
    pos = pos_first + lax.broadcasted_iota(jnp.int32, (chunk, C_MIX), 0)
    cnt = jnp.minimum(pos + 1, win).astype(F32)
    d = win_sum / cnt - xp
    o_pool = _dot(d, poolw_ref[...]) * scale
    new_pool = pool_scr[0, POOL_BASE + chunk - POOL_PAST:POOL_BASE + chunk, :]
    pool_scr[0, POOL_BASE - POOL_PAST:POOL_BASE, :] = new_pool
    return o_pool, new_pool


def _ssd_conv(proj_ref, conv_scr, prow, chunk):
    conv_base = SUBLANES
    conv_scr[conv_base:conv_base + chunk, :] = proj_ref[:, OFF_XBC:OFF_XBC + SSM_CONV_DIM]
    conv = None
    for w in range(SSM_CONV):
        start = conv_base - (SSM_CONV - 1) + w
        term = conv_scr[start:start + chunk, :] * prow(PT_CONVW + w, SSM_CONV_DIM)
        conv = term if conv is None else conv + term
    conv = _silu(conv + prow(PT_CONVB, SSM_CONV_DIM))
    new_conv = conv_scr[conv_base + chunk - (SSM_CONV - 1):conv_base + chunk, :]
    conv_scr[conv_base - (SSM_CONV - 1):conv_base, :] = new_conv
    return conv, new_conv


def _ssd_scan(proj_ref, conv, ssm_st, prow, chunk):
    xs = conv[:, 0:C_MIX]
    dt = proj_ref[:, OFF_DTX:OFF_DTX + C_MIX]
    a_neg = -jnp.exp(prow(PT_ALOG, C_MIX))
    la = dt * a_neg
    xdt = xs * dt
    cum = _cum_log_decay(la)
    heads_per_group = SSM_H // SSM_G
    b_of = lambda g: conv[:, C_MIX + SSM_N * g:C_MIX + SSM_N * (g + 1)]
    c_of = lambda g: conv[:, C_MIX + SSM_G * SSM_N + SSM_N * g:C_MIX + SSM_G * SSM_N + SSM_N * (g + 1)]
    if chunk == SSM_P:
        row = lax.broadcasted_iota(jnp.int32, (chunk, C_MIX), 0)
        col = lax.broadcasted_iota(jnp.int32, (chunk, C_MIX), 1) % chunk
        cum_j = jnp.sum(jnp.where(row == col, cum, 0.0), axis=0, keepdims=True)
        dec = jnp.exp(jnp.where(row >= col, cum - cum_j, -jnp.inf))
        cb = jnp.concatenate([_dot_nt(c_of(g), jnp.concatenate([b_of(g)] * heads_per_group, axis=0))
                              for g in range(SSM_G)], axis=1)
        x_bd = jnp.where(_block_diag_mask(SSM_H * chunk, chunk, C_MIX, SSM_P),
                         jnp.concatenate([xdt] * SSM_H, axis=0), 0.0)
        y = _dot(cb * dec, x_bd)
    else:
        cum_t = cum.T
        causal = _tril(chunk)
        att_rows = []
        for h in range(SSM_H):
            if h % heads_per_group == 0:
                cb = _dot_nt(c_of(h // heads_per_group), b_of(h // heads_per_group))
            col = cum[:, SSM_P * h:SSM_P * h + 1]
            row = cum_t[SSM_P * h:SSM_P * h + 1, :]
            att_rows.append(cb * jnp.exp(jnp.where(causal, col - row, -jnp.inf)))
        y = _select_heads(_dot(jnp.concatenate(att_rows, axis=0), xdt), SSM_H, chunk, SSM_P)
    cum_last = cum[chunk - 1:chunk, :]
    decay_in = jnp.exp(cum)
    xw = xdt * jnp.exp(cum_last - cum)
    decay_state = jnp.exp(cum_last)
    y_state = []
    for g in range(SSM_G):
        lanes = slice(SSM_N * g, SSM_N * (g + 1))
        st = ssm_st[g]
        y_state.append(_dot(c_of(g), st))
        ssm_st[g] = st * decay_state[:, lanes] + _dot_tn(b_of(g), xw[:, lanes])
    return y + jnp.concatenate(y_state, axis=1) * decay_in + prow(PT_DSKIP, C_MIX) * xs


def _mixer_kernel(*refs, chunk, group, pos0, layer, carried, n_alias, lin):
    feat_all = refs[0]
    n_state = 5 if lin else 3
    n_in = 1 + (n_state if carried else 0)
    poolw_ref, ptab_ref = refs[n_in:n_in + 2]
    n_in += 2 + n_alias
    cat_all = refs[n_in]
    new_states = refs[n_in + 1:n_in + 1 + n_state]
    scratch = refs[n_in + 1 + n_state:]
    if lin:
        npool_ref, ngla_ref, nhg_ref, nssm_ref, nconv_ref = new_states
        pool_scr, conv_scr, gla_st, hg_st, ssm_st = scratch
    else:
        npool_ref, nssm_ref, nconv_ref = new_states
        pool_scr, conv_scr, ssm_st = scratch
    seq_state = (lambda ref, s: ref.at[s]) if lin else (lambda ref, s: ref.at[:, s])
    prow = lambda r, w: ptab_ref[r:r + 1, 0:w]
    c = pl.program_id(1)
    n_c = pl.num_programs(1)
    sub = min(SUB_BLOCK, chunk)
    conv_base = SUBLANES

    @pl.when(c == 0)
    def _():
        pool_scr[...] = jnp.zeros(pool_scr.shape, F32)
        conv_scr[...] = jnp.zeros(conv_scr.shape, F32)
        if carried:
            carried_refs = refs[1:1 + n_state]
            poolb_ref, ssms_ref, convb_ref = carried_refs[0], carried_refs[-2], carried_refs[-1]
            for s in range(group):
                pool_scr[s, 0, POOL_BASE - POOL_PAST:POOL_BASE, :] = seq_state(poolb_ref, s)[...]
                conv_scr[s, conv_base - (SSM_CONV - 1):conv_base, :] = seq_state(convb_ref, s)[...]
                for g in range(SSM_G):
                    ssm_st[s, g] = ssms_ref[s, g].T
                if lin:
                    gla_st[s] = _load_state_t(carried_refs[1][s], GLA_H, GLA_DK)
                    hg_st[s] = _load_state_t(carried_refs[2][s], HG_H, HG_DK)
        else:
            ssm_st[...] = jnp.zeros(ssm_st.shape, F32)
            if lin:
                gla_st[...] = jnp.zeros(gla_st.shape, F32)
                hg_st[...] = jnp.zeros(hg_st.shape, F32)

    seq_rows = lambda ref, s: ref.at[pl.ds(s * chunk, chunk)]
    new_pools, new_convs, convs, cum_decays, spans = [], [], [], [], []

    def conv_stage(s):
        conv, new_conv = _ssd_conv(seq_rows(feat_all, s), conv_scr.at[s], prow, chunk)
        convs.append(conv)
        new_convs.append(new_conv)

    def decay_stage(s):
        feat_ref = seq_rows(feat_all, s)
        b_gla = _cum_log_decay(feat_ref[:, OFF_LR:OFF_LR + GLA_H * GLA_DK])
        b_hg = _cum_log_decay(feat_ref[:, OFF_RF:OFF_RF + C_MIX])
        cum_decays.extend([b_gla, b_hg])
        spans.extend([_decay_span(b_gla), _decay_span(b_hg)])

    def scan_stage(s):
        seq_rows(cat_all, s)[:, 3 * C_MIX:4 * C_MIX] = _ssd_scan(seq_rows(feat_all, s), convs[s], ssm_st.at[s], prow,
                                                                chunk)

    def pool_stage(s):
        o_pool, new_pool = _pool_mixer(seq_rows(feat_all, s), pool_scr.at[s], poolw_ref, prow(PT_POOL_SCALE, C_MIX),
                                       chunk, pos0 + c * chunk)
        seq_rows(cat_all, s)[:, 0:C_MIX] = o_pool
        new_pools.append(new_pool)

    stages = (conv_stage, decay_stage, scan_stage, pool_stage) if lin else (conv_stage, scan_stage, pool_stage)
    if chunk >= PROMPT_CHUNK:
        for stage in stages:
            for s in range(group):
                stage(s)
    else:
        for s in range(group):
            for stage in stages:
                stage(s)

    if lin:
        _linear_attention_step(feat_all, cat_all, gla_st, hg_st, cum_decays, spans, seq_rows, group, sub)
    else:
        for s in range(group):
            seq_rows(cat_all, s)[:, C_MIX:3 * C_MIX] = jnp.zeros((chunk, 2 * C_MIX), F32)

    @pl.when(c == n_c - 1)
    def _():
        for s in range(group):
            seq_state(npool_ref, s)[...] = new_pools[s]
            seq_state(nconv_ref, s)[...] = new_convs[s]
            for g in range(SSM_G):
                nssm_ref[s, g] = ssm_st[s, g].T
            if lin:
                ngla_ref[s] = _store_state_t(gla_st[s], GLA_H, GLA_DK)
                nhg_ref[s] = _store_state_t(hg_st[s], HG_H, HG_DK)


def _linear_attention_step(feat_all, cat_all, gla_st, hg_st, cum_decays, spans, seq_rows, group, sub):
    narrow = functools.reduce(jnp.maximum, spans) <= MAX_FACTORED_SPAN

    def run(la_gla, la_hg, *cums):
        for s in range(group):
            feat_ref, cat_ref = seq_rows(feat_all, s), seq_rows(cat_all, s)
            o_g, st_g = la_gla(feat_ref[:, OFF_GQ:OFF_GK], feat_ref[:, OFF_GK:OFF_GV], feat_ref[:, OFF_GV:OFF_GR],
                               cums[2 * s], gla_st[s])
            gla_st[s] = st_g
            cat_ref[:, C_MIX:2 * C_MIX] = o_g
            hk = 1.0 - jnp.exp(feat_ref[:, OFF_RF:OFF_RI])
            o_h, st_h = la_hg(feat_ref[:, OFF_RQ:OFF_RF], hk, feat_ref[:, OFF_RI:OFF_RG], cums[2 * s + 1], hg_st[s])
            hg_st[s] = st_h
            cat_ref[:, 2 * C_MIX:3 * C_MIX] = o_h

    factored = functools.partial(run, functools.partial(_la_factored, n_heads=GLA_H, dk=GLA_DK),
                                 functools.partial(_la_factored, n_heads=HG_H, dk=HG_DK))
    direct = functools.partial(run, functools.partial(_la_direct, n_heads=GLA_H, dk=GLA_DK, sub=sub),
                               functools.partial(_la_direct, n_heads=HG_H, dk=HG_DK, sub=sub))
    lax.cond(narrow, factored, direct, *cum_decays)


def _mixer_call(proj, cat_prev, carried, prev_states, mats, *, n_seq, seq_len, row0, chunk_major, chunk, group,
                pos0, layer, lin):
    rows = proj.shape[0]
    n_c, n_b, blk = seq_len // chunk, n_seq // group, group * chunk
    assert chunk_major or n_c == 1
    row_map = lambda b, c: (row0 // blk + c * n_b + b, 0)

    def seq_major(tail):
        nd = len(tail)
        return (DEPTH, n_seq) + tail, pl.BlockSpec((None, group) + tail, lambda b, c: (layer, b) + (0,) * nd)

    def time_major(n_rows, width):
        return (DEPTH, n_rows, n_seq, width), pl.BlockSpec((None, n_rows, group, width), lambda b, c: (layer, 0, b, 0))

    ssm = seq_major((SSM_G, 128, SSM_N))
    if lin:
        states = [seq_major((POOL_PAST, C_MIX)), seq_major((GLA_H * GLA_DK, GLA_DV)), seq_major((HG_H * HG_DK, HG_DV)),
                  ssm, seq_major((SSM_CONV - 1, SSM_CONV_DIM))]
    else:
        states = [time_major(POOL_PAST, C_MIX), ssm, time_major(SSM_CONV - 1, SSM_CONV_DIM)]

    def mat_spec(m):
        nd = m.ndim - 1
        return pl.BlockSpec((None,) + m.shape[1:], lambda b, c: (layer,) + (0,) * nd)

    state_specs = [spec for _, spec in states]
    in_specs = [pl.BlockSpec((blk, W_PROJ), row_map)]
    args = [proj]
    if carried is not None:
        in_specs += state_specs
        args += list(carried)
    in_specs += [mat_spec(m) for m in mats]
    args += list(mats)
    aliases = {}
    if cat_prev is not None:
        aliases[len(args)] = 0
        args.append(cat_prev)
    if prev_states is not None:
        for i, s in enumerate(prev_states):
            aliases[len(args)] = 1 + i
            args.append(s)
    in_specs += [pl.BlockSpec(memory_space=pl.ANY)] * len(aliases)
    out_shape = [jax.ShapeDtypeStruct((rows, D_MODEL), F32)] + [jax.ShapeDtypeStruct(shape, F32) for shape, _ in states]
    out_specs = [pl.BlockSpec((blk, D_MODEL), row_map)] + state_specs
    scratch = [pltpu.VMEM((group, 4, POOL_BASE + chunk, C_MIX), F32),
               pltpu.VMEM((group, SUBLANES + chunk, SSM_CONV_DIM), F32)]
    if lin:
        scratch += [pltpu.VMEM((group, GLA_H * GLA_DV, GLA_H * GLA_DK), F32),
                    pltpu.VMEM((group, HG_H * HG_DV, HG_H * HG_DK), F32)]
    scratch.append(pltpu.VMEM((group, SSM_G, SSM_N, 128), F32))
    return pl.pallas_call(
        functools.partial(_mixer_kernel, chunk=chunk, group=group, pos0=pos0, layer=layer,
                          carried=carried is not None, n_alias=len(aliases), lin=lin),
        grid=(n_b, n_c),
        in_specs=in_specs,
        out_specs=out_specs,
        out_shape=out_shape,
        scratch_shapes=scratch,
        input_output_aliases=aliases,
        compiler_params=pltpu.CompilerParams(dimension_semantics=("arbitrary", "arbitrary"),
                                             vmem_limit_bytes=VMEM_LIMIT_BYTES),
        name=f"mixer_chunk{chunk}",
    )(*args)


def _batched_recurrence_kernel(*refs, n_seq, seq_len, dk, own_keys, n_alias):
    n_in = 5 if own_keys else 4
    q_ref, v_ref, la_ref, s_in = refs[0], refs[n_in - 3], refs[n_in - 2], refs[n_in - 1]
    o_ref, s_out, q_t, k_t, a_t, v_t = refs[n_in + n_alias:]
    key_w = HEADS_PER_STEP * dk
    base = 0 if key_w == q_ref.shape[1] else pl.multiple_of(pl.program_id(0) * key_w, key_w)
    s_out[...] = s_in[...]
    for t in range(seq_len):
        rows = pl.ds(t, n_seq, stride=seq_len)
        a = jnp.exp(la_ref[rows, :])
        k = refs[1][rows, :] if own_keys else 1.0 - a
        q_t[...] = q_ref[rows, :].T
        k_t[...] = k.T
        a_t[...] = a.T
        v_t[...] = v_ref[rows, :].T
        heads = []
        for h in range(HEADS_PER_STEP):
            v_h = v_t[GLA_DV * h:GLA_DV * (h + 1), :]

            def body(i, acc, h=h, v_h=v_h):
                r = h * dk + i
                s = a_t[pl.ds(base + r, 1), :] * s_out[r] + k_t[pl.ds(base + r, 1), :] * v_h
                s_out[r] = s
                return acc + q_t[pl.ds(base + r, 1), :] * s

            heads.append(lax.fori_loop(0, dk, body, jnp.zeros((GLA_DV, n_seq), F32), unroll=4))
        o_ref[rows, :] = jnp.concatenate(heads, axis=0).T


def _batched_recurrence_call(feat, cat, state_t, prev_state, *, layer, row0, n_seq, seq_len, q_off, k_off, v_off,
                             la_off, out_off, n_heads, dk):
    n_rows = n_seq * seq_len
    rb = row0 // n_rows
    key_w = HEADS_PER_STEP * dk
    assert key_w in (LANES // 2, LANES) and n_heads % HEADS_PER_STEP == 0

    def lanes(off, per_step):
        return pl.BlockSpec((n_rows, LANES), lambda i: (rb, off // LANES + (i if per_step else 0)))

    key = lambda off: lanes(off, key_w == LANES)
    in_specs = [key(q_off)] + ([key(k_off)] if k_off is not None else []) + [lanes(v_off, True), key(la_off)]
    args = [feat] * len(in_specs)
    state_spec = pl.BlockSpec((None, key_w, GLA_DV, n_seq), lambda i: (layer, i, 0, 0))
    in_specs.append(state_spec)
    args.append(state_t)
    aliases = {len(args): 0}
    args.append(cat)
    if prev_state is not None:
        aliases[len(args)] = 1
        args.append(prev_state)
    in_specs += [pl.BlockSpec(memory_space=pl.ANY)] * len(aliases)
    return pl.pallas_call(
        functools.partial(_batched_recurrence_kernel, n_seq=n_seq, seq_len=seq_len, dk=dk,
                          own_keys=k_off is not None, n_alias=len(aliases)),
        grid=(n_heads // HEADS_PER_STEP,),
        in_specs=in_specs,
        out_specs=[lanes(out_off, True), state_spec],
        out_shape=[jax.ShapeDtypeStruct(cat.shape, F32), jax.ShapeDtypeStruct(state_t.shape, F32)],
        scratch_shapes=[pltpu.VMEM((LANES, n_seq), F32)] * 4,
        input_output_aliases=aliases,
        compiler_params=pltpu.CompilerParams(dimension_semantics=("arbitrary",), vmem_limit_bytes=VMEM_LIMIT_BYTES),
        name=f"batched_recurrence_dk{dk}",
    )(*args)


def _arrange_w_in(w):
    lr = w[..., 1024:1040]
    sdt = w[..., 3088:3092]
    pad = jnp.zeros(w.shape[:-1] + (128 - GLA_RANK - SSM_H,), w.dtype)
    return jnp.concatenate([w[..., :1024], w[..., 1040:3088], lr, sdt, pad, jnp.repeat(sdt, SSM_P, axis=-1)], axis=-1)


def _pool_block_diag(pw):
    eye = jnp.eye(len(POOL_WINDOWS), dtype=pw.dtype)
    return jnp.einsum('lgcd,gh->lgchd', pw, eye).reshape(pw.shape[0], C_MIX, C_MIX)


def _param_table(pool_scale, gla_gate_bias, gla_norm, hgrn_norm, ssm_conv_b, ssm_dt_bias, ssm_A_log, ssm_D,
                 ssm_norm, ssm_conv_w, hgrn_lb_logits):
    def rows(v):
        v = v.astype(F32).reshape(DEPTH, -1, v.shape[-1])
        return jnp.pad(v, ((0, 0), (0, 0), (0, SSM_CONV_DIM - v.shape[-1])))
    lbl = jnp.broadcast_to(hgrn_lb_logits[None], (DEPTH,) + hgrn_lb_logits.shape)
    parts = [rows(pool_scale), rows(gla_gate_bias), rows(jnp.tile(gla_norm, (1, GLA_H))),
             rows(jnp.tile(hgrn_norm, (1, HG_H))), rows(ssm_conv_b), rows(jnp.repeat(ssm_dt_bias, SSM_P, axis=-1)),
             rows(jnp.repeat(ssm_A_log, SSM_P, axis=-1)), rows(jnp.repeat(ssm_D, SSM_P, axis=-1)), rows(ssm_norm),
             rows(ssm_conv_w), rows(lbl)]
    tab = jnp.concatenate(parts, axis=1)
    return jnp.pad(tab, ((0, 0), (0, PT_ROWS - tab.shape[1]), (0, 0)))


def kernel(x_prompt, x_sample, state_pool, state_gla, state_hgrn, state_ssm, state_conv, ffn1_norm, ffn1_w_gate, ffn1_w_up, ffn1_w_down, mix_norm, w_in, pool_w, pool_scale, gla_w_gate, gla_gate_bias, gla_norm, hgrn_lb_logits, hgrn_norm, ssm_conv_w, ssm_conv_b, ssm_dt_bias, ssm_A_log, ssm_D, ssm_norm, w_out, ffn2_norm, ffn2_w_gate, ffn2_w_up, ffn2_w_down, final_norm):
    bp, tp, _ = x_prompt.shape
    bs, ts, _ = x_sample.shape
    rows_p, rows_s = bp * tp, bs * ts
    gain = lambda v: v.reshape(DEPTH, 1, D_MODEL).astype(F32)
    bf = lambda w: w.astype(BF16)

    wgate = bf(jnp.pad(gla_w_gate, ((0, 0), (0, 128 - GLA_RANK), (0, 0))))
    ptab = _param_table(pool_scale, gla_gate_bias, gla_norm, hgrn_norm, ssm_conv_b, ssm_dt_bias, ssm_A_log, ssm_D,
                        ssm_norm, ssm_conv_w, hgrn_lb_logits)
    mats = [bf(_pool_block_diag(pool_w)), ptab]
    seq_last = lambda s: jnp.transpose(s, (0, 2, 3, 4, 1)).reshape(DEPTH, -1, s.shape[-1], bs)
    carried = (jnp.transpose(state_pool, (0, 2, 1, 3)), state_ssm.reshape(DEPTH, bs, SSM_G, 128, SSM_N),
               jnp.transpose(state_conv, (0, 2, 1, 3)))
    gla_t, hg_t = seq_last(state_gla), seq_last(state_hgrn)
    ffn1 = (gain(ffn1_norm), bf(ffn1_w_gate), bf(ffn1_w_up), bf(ffn1_w_down))
    ffn2 = (gain(ffn2_norm), bf(ffn2_w_gate), bf(ffn2_w_up), bf(ffn2_w_down))
    gm, win, wout = gain(mix_norm), bf(_arrange_w_in(w_in)), bf(w_out)
    gf = final_norm.reshape(1, D_MODEL).astype(F32)

    assert bp * PROMPT_CHUNK == ROW_TILE and rows_s % ROW_TILE == 0
    n_first = rows_p // ROW_TILE
    xs = [x_prompt, x_sample.reshape(rows_s, D_MODEL)]
    states_p = states_s = new_gla_t = new_hg_t = None
    for l in range(DEPTH):
        x1, feat = _pre_call(xs, *ffn1, gm, win, wgate, ptab, l, rows_p + rows_s, n_first)
        outs_p = _mixer_call(feat, None, None, states_p, mats, n_seq=bp, seq_len=tp, row0=0, chunk_major=True,
                             chunk=PROMPT_CHUNK, group=PROMPT_GROUP, pos0=0, layer=l, lin=True)
        outs_s = _mixer_call(feat, outs_p[0], carried, states_s, mats, n_seq=bs, seq_len=ts, row0=rows_p,
                             chunk_major=False, chunk=ts, group=SAMPLE_GROUP, pos0=PAST_LEN, layer=l, lin=False)
        states_p, states_s = outs_p[1:], outs_s[1:]
        sample = dict(layer=l, row0=rows_p, n_seq=bs, seq_len=ts)
        cat, new_gla_t = _batched_recurrence_call(feat, outs_s[0], gla_t, new_gla_t, q_off=OFF_GQ, k_off=OFF_GK,
                                                  v_off=OFF_GV, la_off=OFF_LR, out_off=C_MIX, n_heads=GLA_H,
                                                  dk=GLA_DK, **sample)
        cat, new_hg_t = _batched_recurrence_call(feat, cat, hg_t, new_hg_t, q_off=OFF_RQ, k_off=None, v_off=OFF_RI,
                                                 la_off=OFF_RF, out_off=2 * C_MIX, n_heads=HG_H, dk=HG_DK, **sample)
        final_shapes = ((bp, tp, D_MODEL), (rows_s, D_MODEL)) if l == DEPTH - 1 else None
        xs = _post_call(x1, cat, feat, ptab, wout, *ffn2, gf, l, n_first, final_shapes)

    seq_first = lambda s, h, dk: jnp.transpose(s.reshape(DEPTH, h, dk, s.shape[-2], bs), (0, 4, 1, 2, 3))
    time_minor = lambda s: jnp.transpose(s, (0, 2, 1, 3))
    pool_p, gla_p, hg_p, ssm_p, conv_p = states_p
    pool_s, ssm_s, conv_s = states_s
    return (xs[0], xs[1].reshape(bs, ts, D_MODEL),
            pool_p, time_minor(pool_s),
            gla_p.reshape(DEPTH, bp, GLA_H, GLA_DK, GLA_DV), seq_first(new_gla_t, GLA_H, GLA_DK),
            hg_p.reshape(DEPTH, bp, HG_H, HG_DK, HG_DV), seq_first(new_hg_t, HG_H, HG_DK),
            ssm_p.reshape(DEPTH, bp, SSM_H, SSM_P, SSM_N), ssm_s.reshape(DEPTH, bs, SSM_H, SSM_P, SSM_N),
            conv_p, time_minor(conv_s))
```

```python
import functools

import jax
import jax.numpy as jnp
from jax import lax
from jax.experimental import pallas as pl
from jax.experimental.pallas import tpu as pltpu

F32 = jnp.float32
BF16 = jnp.bfloat16

D_MODEL = 1024
D_FF = 2816
DEPTH = 2
EPS = 1e-6
PAST_LEN = 16384

C_MIX = 256
POOL_WINDOWS = (2, 4, 8, 16)
POOL_GC = 64
POOL_PAST = 15
GLA_H, GLA_DK, GLA_DV, GLA_RANK, GLA_TAU = 4, 32, 64, 16, 16.0
HG_H, HG_DK, HG_DV = 4, 64, 64
SSM_H, SSM_P, SSM_G, SSM_N, SSM_CONV = 4, 64, 2, 128, 4
SSM_CONV_DIM = 768
N_IN = 3092

OFF_PX, OFF_GQ, OFF_GK, OFF_GV, OFF_GR = 0, 256, 384, 512, 768
OFF_RQ, OFF_RF, OFF_RI, OFF_RG = 1024, 1280, 1536, 1792
OFF_SZ, OFF_XBC = 2048, 2304
OFF_LR = 3072
OFF_DTX = 3200
W_PROJ = 3456

PT_POOL_SCALE, PT_GBIAS, PT_GNORM, PT_HNORM, PT_CONVB, PT_DTB, PT_ALOG, PT_DSKIP, PT_SNORM = range(9)
PT_CONVW = 9
PT_LBL = 13
PT_ROWS = 16

VMEM_LIMIT_BYTES = 56 * 1024 * 1024
SUBLANES = 8
LANES = 128
HEADS_PER_STEP = LANES // GLA_DV

ROW_TILE = 512
FF_CHUNKS = (0, 768, 1536, 2304, D_FF)
PROMPT_CHUNK = 64
PROMPT_GROUP = 8
SAMPLE_GROUP = 8
SUB_BLOCK = 16
MAX_FACTORED_SPAN = 60.0
POOL_BASE = 24


def _sigmoid(x):
    return 1.0 / (1.0 + jnp.exp(-x))


def _silu(x):
    return x * _sigmoid(x)


def _log_sigmoid(x):
    return jnp.minimum(x, 0.0) - jnp.log(1.0 + jnp.exp(-jnp.abs(x)))


def _softplus(x):
    return jnp.maximum(x, 0.0) + jnp.log(1.0 + jnp.exp(-jnp.abs(x)))


def _rms(x, g):
    ms = jnp.mean(x * x, axis=-1, keepdims=True)
    return x * lax.rsqrt(ms + EPS) * g


def _dot(a, b):
    return jnp.dot(a.astype(BF16), b.astype(BF16), preferred_element_type=F32)


def _dot_nt(a, b):
    return lax.dot_general(a.astype(BF16), b.astype(BF16), (((1,), (1,)), ((), ())),
                           preferred_element_type=F32)


def _dot_tn(a, b):
    return lax.dot_general(a.astype(BF16), b.astype(BF16), (((0,), (0,)), ((), ())),
                           preferred_element_type=F32)


def _split3(x):
    hi = x.astype(BF16)
    r1 = x - hi.astype(F32)
    mid = r1.astype(BF16)
    lo = (r1 - mid.astype(F32)).astype(BF16)
    return hi, mid, lo


def _dot_exact_rhs(a01, x):
    hi, mid, lo = _split3(x)
    a = a01.astype(BF16)
    return (jnp.dot(a, hi, preferred_element_type=F32) + jnp.dot(a, mid, preferred_element_type=F32)
            + jnp.dot(a, lo, preferred_element_type=F32))


def _dot_exact_lhs(x, b01):
    hi, mid, lo = _split3(x)
    b = b01.astype(BF16)
    return (jnp.dot(hi, b, preferred_element_type=F32) + jnp.dot(mid, b, preferred_element_type=F32)
            + jnp.dot(lo, b, preferred_element_type=F32))


def _swiglu_half(x, g, wg_ref, wu_ref, wd_ref):
    h = _rms(x, g).astype(BF16)
    acc = None
    for lo, hi in zip(FF_CHUNKS[:-1], FF_CHUNKS[1:]):
        sl = slice(lo, hi)
        gate = jnp.dot(h, wg_ref[:, sl], preferred_element_type=F32)
        up = jnp.dot(h, wu_ref[:, sl], preferred_element_type=F32)
        act = (_silu(gate) * up).astype(BF16)
        d = jnp.dot(act, wd_ref[sl, :], preferred_element_type=F32)
        acc = d if acc is None else acc + d
    return x + 0.5 * acc


def _pre_kernel(*refs, n_first, split_in, layer):
    if split_in:
        x = jnp.where(pl.program_id(0) < n_first, refs[0][...].reshape(ROW_TILE, D_MODEL), refs[1][...])
        refs = refs[2:]
    else:
        x = refs[0][...]
        refs = refs[1:]
    g1_ref, wg_ref, wu_ref, wd_ref, gm_ref, win_ref, wgate_ref, ptab_ref, x1_ref, feat_ref = refs
    x1 = _swiglu_half(x, g1_ref[...], wg_ref, wu_ref, wd_ref)
    x1_ref[...] = x1
    h = _rms(x1, gm_ref[...]).astype(BF16)
    _mixer_features(h, win_ref, wgate_ref, ptab_ref, feat_ref, layer)


def _hgrn_lower_bound(ptab_ref, layer):
    lbl = ptab_ref[PT_LBL:PT_LBL + DEPTH, 0:C_MIX]
    lexp = jnp.exp(lbl - jnp.max(lbl, axis=0, keepdims=True))
    lsum = jnp.sum(lexp, axis=0, keepdims=True)
    lb = jnp.zeros((1, C_MIX), F32)
    for m in range(1, layer + 1):
        lb = lb + lexp[m:m + 1] / lsum
    return lb


def _mixer_features(h, win_ref, wgate_ref, ptab_ref, feat_ref, layer):
    prow = lambda r, w: ptab_ref[r:r + 1, 0:w]
    proj = lambda lo, hi: jnp.dot(h, win_ref[:, lo:hi], preferred_element_type=F32)
    d = proj(OFF_LR, W_PROJ)
    logit = _dot(d[:, 0:LANES], wgate_ref[...]) + prow(PT_GBIAS, LANES)
    feat_ref[:, OFF_LR:OFF_DTX] = _log_sigmoid(logit) * (1.0 / GLA_TAU)
    feat_ref[:, OFF_DTX:W_PROJ] = _softplus(d[:, LANES:LANES + C_MIX] + prow(PT_DTB, C_MIX))
    b = proj(OFF_RQ, OFF_XBC)
    feat_ref[:, OFF_RQ:OFF_RF] = _silu(b[:, 0:C_MIX])
    lb = _hgrn_lower_bound(ptab_ref, layer)
    log_lb = jnp.log(lb)
    t2 = jnp.log(1.0 - lb) + _log_sigmoid(b[:, C_MIX:2 * C_MIX])
    feat_ref[:, OFF_RF:OFF_RI] = jnp.maximum(log_lb, t2) + jnp.log(1.0 + jnp.exp(-jnp.abs(log_lb - t2)))
    feat_ref[:, OFF_RI:OFF_RG] = b[:, 2 * C_MIX:3 * C_MIX]
    feat_ref[:, OFF_RG:OFF_XBC] = _silu(b[:, 3 * C_MIX:5 * C_MIX])
    a = proj(OFF_PX, OFF_RQ)
    feat_ref[:, OFF_GR:OFF_RQ] = _silu(a[:, OFF_GR:OFF_RQ])
    feat_ref[:, OFF_GQ:OFF_GK] = a[:, OFF_GQ:OFF_GK] * (GLA_DK ** -0.5)
    feat_ref[:, OFF_PX:OFF_GQ] = a[:, OFF_PX:OFF_GQ]
    feat_ref[:, OFF_GK:OFF_GR] = a[:, OFF_GK:OFF_GR]
    feat_ref[:, OFF_XBC:OFF_LR] = proj(OFF_XBC, OFF_LR)


def _segment_rms(o, width, gain):
    n = o.shape[-1]
    seg = (lax.broadcasted_iota(jnp.int32, (n, n), 0) // width
           == lax.broadcasted_iota(jnp.int32, (n, n), 1) // width).astype(BF16)
    sq = o * o
    hi = sq.astype(BF16)
    lo = (sq - hi.astype(F32)).astype(BF16)
    ms = (jnp.dot(hi, seg, preferred_element_type=F32) + jnp.dot(lo, seg, preferred_element_type=F32)) * (1.0 / width)
    return o * lax.rsqrt(ms + EPS) * gain


def _mixer_outputs(cat_ref, ggla_ref, ghg_ref, gz_ref, ptab_ref):
    prow = lambda r, w: ptab_ref[r:r + 1, 0:w]
    o_gla = _segment_rms(cat_ref[:, C_MIX:2 * C_MIX], GLA_DV, prow(PT_GNORM, C_MIX)) * ggla_ref[...]
    o_hg = _segment_rms(cat_ref[:, 2 * C_MIX:3 * C_MIX], HG_DV, prow(PT_HNORM, C_MIX)) * ghg_ref[...]
    y = cat_ref[:, 3 * C_MIX:4 * C_MIX] * gz_ref[...]
    half = C_MIX // SSM_G
    o_ssm = [_rms(y[:, half * g:half * (g + 1)], ptab_ref[PT_SNORM:PT_SNORM + 1, half * g:half * (g + 1)])
             for g in range(SSM_G)]
    return jnp.concatenate([cat_ref[:, 0:C_MIX], o_gla, o_hg] + o_ssm, axis=1)


def _post_kernel(x_ref, cat_ref, ggla_ref, ghg_ref, gz_ref, ptab_ref, wout_ref, g2_ref, wg_ref, wu_ref, wd_ref,
                 gf_ref, *o_refs, n_first, final):
    mixed = _mixer_outputs(cat_ref, ggla_ref, ghg_ref, gz_ref, ptab_ref)
    x2 = x_ref[...] + jnp.dot(mixed.astype(BF16), wout_ref[...], preferred_element_type=F32)
    x3 = _swiglu_half(x2, g2_ref[...], wg_ref, wu_ref, wd_ref)
    if final:
        y = _rms(x3, gf_ref[...])
        i = pl.program_id(0)

        @pl.when(i < n_first)
        def _():
            o_refs[0][...] = y.reshape(o_refs[0].shape)

        @pl.when(i >= n_first)
        def _():
            o_refs[1][...] = y
    else:
        o_refs[0][...] = x3


def _layer_spec(shape, layer):
    nd = len(shape)
    return pl.BlockSpec((None,) + tuple(shape), lambda *_: (layer,) + (0,) * nd, pipeline_mode=pl.Buffered(1))


def _dense_params():
    return pltpu.CompilerParams(dimension_semantics=("arbitrary",), vmem_limit_bytes=VMEM_LIMIT_BYTES)


def _row_spec(width):
    return pl.BlockSpec((ROW_TILE, width), lambda i: (i, 0))


def _token_specs(n_first, n_prompt_seq):
    return [pl.BlockSpec((n_prompt_seq, PROMPT_CHUNK, D_MODEL), lambda i: (0, jnp.minimum(i, n_first - 1), 0)),
            pl.BlockSpec((ROW_TILE, D_MODEL), lambda i: (jnp.maximum(i - n_first, 0), 0))]


def _pre_call(xs, g1, wg, wu, wd, gm, win, wgate, ptab, layer, rows, n_first):
    split_in = len(xs) == 2
    x_specs = _token_specs(n_first, xs[0].shape[0]) if split_in else [_row_spec(D_MODEL)]
    return pl.pallas_call(
        functools.partial(_pre_kernel, n_first=n_first, split_in=split_in, layer=layer),
        grid=(rows // ROW_TILE,),
        in_specs=x_specs + [_layer_spec((1, D_MODEL), layer), _layer_spec((D_MODEL, D_FF), layer),
                            _layer_spec((D_MODEL, D_FF), layer), _layer_spec((D_FF, D_MODEL), layer),
                            _layer_spec((1, D_MODEL), layer), _layer_spec((D_MODEL, W_PROJ), layer),
                            _layer_spec(wgate.shape[1:], layer), _layer_spec(ptab.shape[1:], layer)],
        out_specs=[_row_spec(D_MODEL), _row_spec(W_PROJ)],
        out_shape=[jax.ShapeDtypeStruct((rows, D_MODEL), F32), jax.ShapeDtypeStruct((rows, W_PROJ), F32)],
        compiler_params=_dense_params(),
        name="pre_ffn_inproj",
    )(*xs, g1, wg, wu, wd, gm, win, wgate, ptab)


def _post_call(x, cat, feat, ptab, wout, g2, wg, wu, wd, gf, layer, n_first, final_shapes):
    rows = x.shape[0]
    if final_shapes is not None:
        out_specs = _token_specs(n_first, final_shapes[0][0])
        out_shape = [jax.ShapeDtypeStruct(s, F32) for s in final_shapes]
    else:
        out_specs = [_row_spec(D_MODEL)]
        out_shape = [jax.ShapeDtypeStruct((rows, D_MODEL), F32)]
    gate_spec = lambda off: pl.BlockSpec((ROW_TILE, C_MIX), lambda i: (i, off // C_MIX))
    return pl.pallas_call(
        functools.partial(_post_kernel, n_first=n_first, final=final_shapes is not None),
        grid=(rows // ROW_TILE,),
        in_specs=[_row_spec(D_MODEL), _row_spec(D_MODEL), gate_spec(OFF_GR), gate_spec(OFF_RG), gate_spec(OFF_SZ),
                  _layer_spec(ptab.shape[1:], layer), _layer_spec((D_MODEL, D_MODEL), layer),
                  _layer_spec((1, D_MODEL), layer), _layer_spec((D_MODEL, D_FF), layer),
                  _layer_spec((D_MODEL, D_FF), layer), _layer_spec((D_FF, D_MODEL), layer),
                  pl.BlockSpec((1, D_MODEL), lambda i: (0, 0))],
        out_specs=out_specs,
        out_shape=out_shape,
        compiler_params=_dense_params(),
        name="post_outproj_ffn",
    )(x, cat, feat, feat, feat, ptab, wout, g2, wg, wu, wd, gf)


def _lane_group(shape, width):
    return lax.broadcasted_iota(jnp.int32, shape, len(shape) - 1) // width


def _block_diag_mask(rows, row_w, cols, col_w):
    r = lax.broadcasted_iota(jnp.int32, (rows, cols), 0) // row_w
    c = lax.broadcasted_iota(jnp.int32, (rows, cols), 1) // col_w
    return r == c


def _tril(n):
    r = lax.broadcasted_iota(jnp.int32, (n, n), 0)
    c = lax.broadcasted_iota(jnp.int32, (n, n), 1)
    return r >= c


def _select_heads(stacked, n_heads, rows, head_w):
    grp = _lane_group((rows, n_heads * head_w), head_w)
    out = jnp.zeros((rows, n_heads * head_w), F32)
    for h in range(n_heads):
        out = jnp.where(grp == h, stacked[h * rows:(h + 1) * rows], out)
    return out


def _cum_log_decay(la):
    return _dot_exact_rhs(_tril(la.shape[0]).astype(F32), la)


def _decay_span(b):
    mid = b.shape[0] // 2 - 1
    return jnp.max(jnp.abs(b - b[mid:mid + 1, :]))


def _la_factored(q, k, v, b, st, n_heads, dk):
    chunk = q.shape[0]
    nk, nv = n_heads * dk, n_heads * 64
    mid = chunk // 2 - 1
    m = b[mid:mid + 1, :]
    qt = q * jnp.exp(b - m)
    kt = k * jnp.exp(m - b)
    k_bd = jnp.where(_block_diag_mask(n_heads * chunk, chunk, nk, dk), jnp.concatenate([kt] * n_heads, axis=0), 0.0)
    att = _dot_nt(qt, k_bd)
    row = lax.broadcasted_iota(jnp.int32, att.shape, 0)
    col = lax.broadcasted_iota(jnp.int32, att.shape, 1) % chunk
    att = jnp.where(row >= col, att, 0.0)
    v_bd = jnp.where(_block_diag_mask(n_heads * chunk, chunk, nv, 64), jnp.concatenate([v] * n_heads, axis=0), 0.0)
    o = _dot(att, v_bd) + _dot_nt(qt * jnp.exp(m), st)
    b_last = b[chunk - 1:chunk, :]
    upd = _dot_tn(v, kt * jnp.exp(b_last - m))
    st_new = st * jnp.exp(b_last) + jnp.where(_block_diag_mask(nv, 64, nk, dk), upd, 0.0)
    return o, st_new


def _la_direct(q, k, v, b, st, n_heads, dk, sub):
    chunk = q.shape[0]
    nk = n_heads * dk
    nv = n_heads * 64
    n_sub = chunk // sub
    refs = [None] + [b[sub * i - 1:sub * i, :] for i in range(1, n_sub)]
    c_loc = jnp.concatenate([b[0:sub]] + [b[sub * i:sub * (i + 1)] - refs[i] for i in range(1, n_sub)], axis=0) \
        if n_sub > 1 else b
    o = _dot_nt(q * jnp.exp(b), st)

    qt = q * jnp.exp(c_loc)
    kgrp = _lane_group((sub, nk), dk)
    outs = []
    for i in range(n_sub):
        rows = slice(sub * i, sub * (i + 1))
        o_i = o[rows]
        if i > 0:
            prev = slice(0, sub * i)
            kt = k[prev] * jnp.exp(refs[i] - b[prev])
            q_stack = jnp.concatenate([jnp.where(kgrp == h, qt[rows], 0.0) for h in range(n_heads)], axis=0)
            att = _dot_nt(q_stack, kt)
            o_i = o_i + _select_heads(_dot(att, v[prev]), n_heads, sub, 64)
        c_i = c_loc[rows]
        q_i = q[rows]
        row_id = lax.broadcasted_iota(jnp.int32, (sub, nk), 0)
        e_list = []
        for j in range(sub):
            c_j = c_loc[sub * i + j:sub * i + j + 1, :]
            k_j = k[sub * i + j:sub * i + j + 1, :]
            dec = jnp.exp(jnp.where(row_id >= j, c_i - c_j, -jnp.inf))
            e_list.append(dec * q_i * k_j)
        e_all = jnp.concatenate(e_list, axis=0)
        expand = _block_diag_mask(nk, dk, nv, 64).astype(F32)
        r_d = _dot(e_all, expand)
        for j in range(sub):
            o_i = o_i + r_d[j * sub:(j + 1) * sub] * v[sub * i + j:sub * i + j + 1, :]
        outs.append(o_i)
    o = jnp.concatenate(outs, axis=0) if n_sub > 1 else outs[0]

    b_last = b[chunk - 1:chunk, :]
    k_end = k * jnp.exp(b_last - b)
    upd = _dot_tn(v, k_end)
    st_new = st * jnp.exp(b_last) + jnp.where(_block_diag_mask(nv, 64, nk, dk), upd, 0.0)
    return o, st_new


def _load_state_t(s, n_heads, dk):
    st = s.T
    tiled = jnp.concatenate([st] * n_heads, axis=0)
    return jnp.where(_block_diag_mask(n_heads * 64, 64, n_heads * dk, dk), tiled, 0.0)


def _store_state_t(st, n_heads, dk):
    acc = st[0:64]
    for h in range(1, n_heads):
        acc = acc + st[64 * h:64 * (h + 1)]
    return acc.T


def _pool_mixer(proj_ref, pool_scr, poolw_ref, scale, chunk, pos_first):
    xp = proj_ref[:, OFF_PX:OFF_PX + C_MIX]
    pool_scr[0, POOL_BASE:POOL_BASE + chunk, :] = xp
    n_ext = POOL_BASE - SUBLANES + chunk
    for s, shift in enumerate((1, 2, 4)):
        cur = pool_scr[s, SUBLANES:SUBLANES + n_ext, :]
        prev = pool_scr[s, SUBLANES - shift:SUBLANES - shift + n_ext, :]
        pool_scr[s + 1, SUBLANES:SUBLANES + n_ext, :] = cur + prev
    s2 = pool_scr[1, POOL_BASE:POOL_BASE + chunk, :]
    s4 = pool_scr[2, POOL_BASE:POOL_BASE + chunk, :]
    s8 = pool_scr[3, POOL_BASE:POOL_BASE + chunk, :]
    s16 = s8 + pool_scr[3, POOL_BASE - 8:POOL_BASE - 8 + chunk, :]
    grp = _lane_group((chunk, C_MIX), POOL_GC)
    win_sum = jnp.where(grp == 0, s2, jnp.where(grp == 1, s4, jnp.where(grp == 2, s8, s16)))
    win = jnp.where(grp == 0, 2, jnp.where(grp == 1, 4, jnp.where(grp == 2, 8, 16)))
    pos = pos_first + lax.broadcasted_iota(jnp.int32, (chunk, C_MIX), 0)
    cnt = jnp.minimum(pos + 1, win).astype(F32)
    d = win_sum / cnt - xp
    o_pool = _dot(d, poolw_ref[...]) * scale
    new_pool = pool_scr[0, POOL_BASE + chunk - POOL_PAST:POOL_BASE + chunk, :]
    pool_scr[0, POOL_BASE - POOL_PAST:POOL_BASE, :] = new_pool
    return o_pool, new_pool


def _ssd_conv(proj_ref, conv_scr, prow, chunk):
    conv_base = SUBLANES
    conv_scr[conv_base:conv_base + chunk, :] = proj_ref[:, OFF_XBC:OFF_XBC + SSM_CONV_DIM]
    conv = None
    for w in range(SSM_CONV):
        start = conv_base - (SSM_CONV - 1) + w
        term = conv_scr[start:start + chunk, :] * prow(PT_CONVW + w, SSM_CONV_DIM)
        conv = term if conv is None else conv + term
    conv = _silu(conv + prow(PT_CONVB, SSM_CONV_DIM))
    new_conv = conv_scr[conv_base + chunk - (SSM_CONV - 1):conv_base + chunk, :]
    conv_scr[conv_base - (SSM_CONV - 1):conv_base, :] = new_conv
    return conv, new_conv


def _ssd_scan(proj_ref, conv, ssm_st, prow, chunk):
    xs = conv[:, 0:C_MIX]
    dt = proj_ref[:, OFF_DTX:OFF_DTX + C_MIX]
    a_neg = -jnp.exp(prow(PT_ALOG, C_MIX))
    la = dt * a_neg
    xdt = xs * dt
    cum = _cum_log_decay(la)
    heads_per_group = SSM_H // SSM_G
    b_of = lambda g: conv[:, C_MIX + SSM_N * g:C_MIX + SSM_N * (g + 1)]
    c_of = lambda g: conv[:, C_MIX + SSM_G * SSM_N + SSM_N * g:C_MIX + SSM_G * SSM_N + SSM_N * (g + 1)]
    if chunk == SSM_P:
        row = lax.broadcasted_iota(jnp.int32, (chunk, C_MIX), 0)
        col = lax.broadcasted_iota(jnp.int32, (chunk, C_MIX), 1) % chunk
        cum_j = jnp.sum(jnp.where(row == col, cum, 0.0), axis=0, keepdims=True)
        dec = jnp.exp(jnp.where(row >= col, cum - cum_j, -jnp.inf))
        cb = jnp.concatenate([_dot_nt(c_of(g), jnp.concatenate([b_of(g)] * heads_per_group, axis=0))
                              for g in range(SSM_G)], axis=1)
        x_bd = jnp.where(_block_diag_mask(SSM_H * chunk, chunk, C_MIX, SSM_P),
                         jnp.concatenate([xdt] * SSM_H, axis=0), 0.0)
        y = _dot(cb * dec, x_bd)
    else:
        cum_t = cum.T
        causal = _tril(chunk)
        att_rows = []
        for h in range(SSM_H):
            if h % heads_per_group == 0:
                cb = _dot_nt(c_of(h // heads_per_group), b_of(h // heads_per_group))
            col = cum[:, SSM_P * h:SSM_P * h + 1]
            row = cum_t[SSM_P * h:SSM_P * h + 1, :]
            att_rows.append(cb * jnp.exp(jnp.where(causal, col - row, -jnp.inf)))
        y = _select_heads(_dot(jnp.concatenate(att_rows, axis=0), xdt), SSM_H, chunk, SSM_P)
    cum_last = cum[chunk - 1:chunk, :]
    decay_in = jnp.exp(cum)
    xw = xdt * jnp.exp(cum_last - cum)
    decay_state = jnp.exp(cum_last)
    y_state = []
    for g in range(SSM_G):
        lanes = slice(SSM_N * g, SSM_N * (g + 1))
        st = ssm_st[g]
        y_state.append(_dot(c_of(g), st))
        ssm_st[g] = st * decay_state[:, lanes] + _dot_tn(b_of(g), xw[:, lanes])
    return y + jnp.concatenate(y_state, axis=1) * decay_in + prow(PT_DSKIP, C_MIX) * xs


def _mixer_kernel(*refs, chunk, group, pos0, layer, carried, n_alias, lin):
    feat_all = refs[0]
    n_state = 5 if lin else 3
    n_in = 1 + (n_state if carried else 0)
    poolw_ref, ptab_ref = refs[n_in:n_in + 2]
    n_in += 2 + n_alias
    cat_all = refs[n_in]
    new_states = refs[n_in + 1:n_in + 1 + n_state]
    scratch = refs[n_in + 1 + n_state:]
    if lin:
        npool_ref, ngla_ref, nhg_ref, nssm_ref, nconv_ref = new_states
        pool_scr, conv_scr, gla_st, hg_st, ssm_st = scratch
    else:
        npool_ref, nssm_ref, nconv_ref = new_states
        pool_scr, conv_scr, ssm_st = scratch
    seq_state = (lambda ref, s: ref.at[s]) if lin else (lambda ref, s: ref.at[:, s])
    prow = lambda r, w: ptab_ref[r:r + 1, 0:w]
    c = pl.program_id(1)
    n_c = pl.num_programs(1)
    sub = min(SUB_BLOCK, chunk)
    conv_base = SUBLANES

    @pl.when(c == 0)
    def _():
        pool_scr[...] = jnp.zeros(pool_scr.shape, F32)
        conv_scr[...] = jnp.zeros(conv_scr.shape, F32)
        if carried:
            carried_refs = refs[1:1 + n_state]
            poolb_ref, ssms_ref, convb_ref = carried_refs[0], carried_refs[-2], carried_refs[-1]
            for s in range(group):
                pool_scr[s, 0, POOL_BASE - POOL_PAST:POOL_BASE, :] = seq_state(poolb_ref, s)[...]
                conv_scr[s, conv_base - (SSM_CONV - 1):conv_base, :] = seq_state(convb_ref, s)[...]
                for g in range(SSM_G):
                    ssm_st[s, g] = ssms_ref[s, g].T
                if lin:
                    gla_st[s] = _load_state_t(carried_refs[1][s], GLA_H, GLA_DK)
                    hg_st[s] = _load_state_t(carried_refs[2][s], HG_H, HG_DK)
        else:
            ssm_st[...] = jnp.zeros(ssm_st.shape, F32)
            if lin:
                gla_st[...] = jnp.zeros(gla_st.shape, F32)
                hg_st[...] = jnp.zeros(hg_st.shape, F32)

    seq_rows = lambda ref, s: ref.at[pl.ds(s * chunk, chunk)]
    new_pools, new_convs, convs, cum_decays, spans = [], [], [], [], []

    def conv_stage(s):
        conv, new_conv = _ssd_conv(seq_rows(feat_all, s), conv_scr.at[s], prow, chunk)
        convs.append(conv)
        new_convs.append(new_conv)

    def decay_stage(s):
        feat_ref = seq_rows(feat_all, s)
        b_gla = _cum_log_decay(feat_ref[:, OFF_LR:OFF_LR + GLA_H * GLA_DK])
        b_hg = _cum_log_decay(feat_ref[:, OFF_RF:OFF_RF + C_MIX])
        cum_decays.extend([b_gla, b_hg])
        spans.extend([_decay_span(b_gla), _decay_span(b_hg)])

    def scan_stage(s):
        seq_rows(cat_all, s)[:, 3 * C_MIX:4 * C_MIX] = _ssd_scan(seq_rows(feat_all, s), convs[s], ssm_st.at[s], prow,
                                                                chunk)

    def pool_stage(s):
        o_pool, new_pool = _pool_mixer(seq_rows(feat_all, s), pool_scr.at[s], poolw_ref, prow(PT_POOL_SCALE, C_MIX),
                                       chunk, pos0 + c * chunk)
        seq_rows(cat_all, s)[:, 0:C_MIX] = o_pool
        new_pools.append(new_pool)

    stages = (conv_stage, decay_stage, scan_stage, pool_stage) if lin else (conv_stage, scan_stage, pool_stage)
    if chunk >= PROMPT_CHUNK:
        for stage in stages:
            for s in range(group):
                stage(s)
    else:
        for s in range(group):
            for stage in stages:
                stage(s)

    if lin:
        _linear_attention_step(feat_all, cat_all, gla_st, hg_st, cum_decays, spans, seq_rows, group, sub)
    else:
        for s in range(group):
            seq_rows(cat_all, s)[:, C_MIX:3 * C_MIX] = jnp.zeros((chunk, 2 * C_MIX), F32)

    @pl.when(c == n_c - 1)
    def _():
        for s in range(group):
            seq_state(npool_ref, s)[...] = new_pools[s]
            seq_state(nconv_ref, s)[...] = new_convs[s]
            for g in range(SSM_G):
                nssm_ref[s, g] = ssm_st[s, g].T
            if lin:
                ngla_ref[s] = _store_state_t(gla_st[s], GLA_H, GLA_DK)
                nhg_ref[s] = _store_state_t(hg_st[s], HG_H, HG_DK)


def _linear_attention_step(feat_all, cat_all, gla_st, hg_st, cum_decays, spans, seq_rows, group, sub):
    narrow = functools.reduce(jnp.maximum, spans) <= MAX_FACTORED_SPAN

    def run(la_gla, la_hg, *cums):
        for s in range(group):
            feat_ref, cat_ref = seq_rows(feat_all, s), seq_rows(cat_all, s)
            o_g, st_g = la_gla(feat_ref[:, OFF_GQ:OFF_GK], feat_ref[:, OFF_GK:OFF_GV], feat_ref[:, OFF_GV:OFF_GR],
                               cums[2 * s], gla_st[s])
            gla_st[s] = st_g
            cat_ref[:, C_MIX:2 * C_MIX] = o_g
            hk = 1.0 - jnp.exp(feat_ref[:, OFF_RF:OFF_RI])
            o_h, st_h = la_hg(feat_ref[:, OFF_RQ:OFF_RF], hk, feat_ref[:, OFF_RI:OFF_RG], cums[2 * s + 1], hg_st[s])
            hg_st[s] = st_h
            cat_ref[:, 2 * C_MIX:3 * C_MIX] = o_h

    factored = functools.partial(run, functools.partial(_la_factored, n_heads=GLA_H, dk=GLA_DK),
                                 functools.partial(_la_factored, n_heads=HG_H, dk=HG_DK))
    direct = functools.partial(run, functools.partial(_la_direct, n_heads=GLA_H, dk=GLA_DK, sub=sub),
                               functools.partial(_la_direct, n_heads=HG_H, dk=HG_DK, sub=sub))
    lax.cond(narrow, factored, direct, *cum_decays)


def _mixer_call(proj, cat_prev, carried, prev_states, mats, *, n_seq, seq_len, row0, chunk_major, chunk, group,
                pos0, layer, lin):
    rows = proj.shape[0]
    n_c, n_b, blk = seq_len // chunk, n_seq // group, group * chunk
    assert chunk_major or n_c == 1
    row_map = lambda b, c: (row0 // blk + c * n_b + b, 0)

    def seq_major(tail):
        nd = len(tail)
        return (DEPTH, n_seq) + tail, pl.BlockSpec((None, group) + tail, lambda b, c: (layer, b) + (0,) * nd)

    def time_major(n_rows, width):
        return (DEPTH, n_rows, n_seq, width), pl.BlockSpec((None, n_rows, group, width), lambda b, c: (layer, 0, b, 0))

    ssm = seq_major((SSM_G, 128, SSM_N))
    if lin:
        states = [seq_major((POOL_PAST, C_MIX)), seq_major((GLA_H * GLA_DK, GLA_DV)), seq_major((HG_H * HG_DK, HG_DV)),
                  ssm, seq_major((SSM_CONV - 1, SSM_CONV_DIM))]
    else:
        states = [time_major(POOL_PAST, C_MIX), ssm, time_major(SSM_CONV - 1, SSM_CONV_DIM)]

    def mat_spec(m):
        nd = m.ndim - 1
        return pl.BlockSpec((None,) + m.shape[1:], lambda b, c: (layer,) + (0,) * nd)

    state_specs = [spec for _, spec in states]
    in_specs = [pl.BlockSpec((blk, W_PROJ), row_map)]
    args = [proj]
    if carried is not None:
        in_specs += state_specs
        args += list(carried)
    in_specs += [mat_spec(m) for m in mats]
    args += list(mats)
    aliases = {}
    if cat_prev is not None:
        aliases[len(args)] = 0
        args.append(cat_prev)
    if prev_states is not None:
        for i, s in enumerate(prev_states):
            aliases[len(args)] = 1 + i
            args.append(s)
    in_specs += [pl.BlockSpec(memory_space=pl.ANY)] * len(aliases)
    out_shape = [jax.ShapeDtypeStruct((rows, D_MODEL), F32)] + [jax.ShapeDtypeStruct(shape, F32) for shape, _ in states]
    out_specs = [pl.BlockSpec((blk, D_MODEL), row_map)] + state_specs
    scratch = [pltpu.VMEM((group, 4, POOL_BASE + chunk, C_MIX), F32),
               pltpu.VMEM((group, SUBLANES + chunk, SSM_CONV_DIM), F32)]
    if lin:
        scratch += [pltpu.VMEM((group, GLA_H * GLA_DV, GLA_H * GLA_DK), F32),
                    pltpu.VMEM((group, HG_H * HG_DV, HG_H * HG_DK), F32)]
    scratch.append(pltpu.VMEM((group, SSM_G, SSM_N, 128), F32))
    return pl.pallas_call(
        functools.partial(_mixer_kernel, chunk=chunk, group=group, pos0=pos0, layer=layer,
                          carried=carried is not None, n_alias=len(aliases), lin=lin),
        grid=(n_b, n_c),
        in_specs=in_specs,
        out_specs=out_specs,
        out_shape=out_shape,
        scratch_shapes=scratch,
        input_output_aliases=aliases,
        compiler_params=pltpu.CompilerParams(dimension_semantics=("arbitrary", "arbitrary"),
                                             vmem_limit_bytes=VMEM_LIMIT_BYTES),
        name=f"mixer_chunk{chunk}",
    )(*args)


def _batched_recurrence_kernel(*refs, n_seq, seq_len, dk, own_keys, n_alias):
    n_in = 5 if own_keys else 4
    q_ref, v_ref, la_ref, s_in = refs[0], refs[n_in - 3], refs[n_in - 2], refs[n_in - 1]
    o_ref, s_out, q_t, k_t, a_t, v_t = refs[n_in + n_alias:]
    key_w = HEADS_PER_STEP * dk
    base = 0 if key_w == q_ref.shape[1] else pl.multiple_of(pl.program_id(0) * key_w, key_w)
    s_out[...] = s_in[...]
    for t in range(seq_len):
        rows = pl.ds(t, n_seq, stride=seq_len)
        a = jnp.exp(la_ref[rows, :])
        k = refs[1][rows, :] if own_keys else 1.0 - a
        q_t[...] = q_ref[rows, :].T
        k_t[...] = k.T
        a_t[...] = a.T
        v_t[...] = v_ref[rows, :].T
        heads = []
        for h in range(HEADS_PER_STEP):
            v_h = v_t[GLA_DV * h:GLA_DV * (h + 1), :]

            def body(i, acc, h=h, v_h=v_h):
                r = h * dk + i
                s = a_t[pl.ds(base + r, 1), :] * s_out[r] + k_t[pl.ds(base + r, 1), :] * v_h
                s_out[r] = s
                return acc + q_t[pl.ds(base + r, 1), :] * s

            heads.append(lax.fori_loop(0, dk, body, jnp.zeros((GLA_DV, n_seq), F32), unroll=4))
        o_ref[rows, :] = jnp.concatenate(heads, axis=0).T


def _batched_recurrence_call(feat, cat, state_t, prev_state, *, layer, row0, n_seq, seq_len, q_off, k_off, v_off,
                             la_off, out_off, n_heads, dk):
    n_rows = n_seq * seq_len
    rb = row0 // n_rows
    key_w = HEADS_PER_STEP * dk
    assert key_w in (LANES // 2, LANES) and n_heads % HEADS_PER_STEP == 0

    def lanes(off, per_step):
        return pl.BlockSpec((n_rows, LANES), lambda i: (rb, off // LANES + (i if per_step else 0)))

    key = lambda off: lanes(off, key_w == LANES)
    in_specs = [key(q_off)] + ([key(k_off)] if k_off is not None else []) + [lanes(v_off, True), key(la_off)]
    args = [feat] * len(in_specs)
    state_spec = pl.BlockSpec((None, key_w, GLA_DV, n_seq), lambda i: (layer, i, 0, 0))
    in_specs.append(state_spec)
    args.append(state_t)
    aliases = {len(args): 0}
    args.append(cat)
    if prev_state is not None:
        aliases[len(args)] = 1
        args.append(prev_state)
    in_specs += [pl.BlockSpec(memory_space=pl.ANY)] * len(aliases)
    return pl.pallas_call(
        functools.partial(_batched_recurrence_kernel, n_seq=n_seq, seq_len=seq_len, dk=dk,
                          own_keys=k_off is not None, n_alias=len(aliases)),
        grid=(n_heads // HEADS_PER_STEP,),
        in_specs=in_specs,
        out_specs=[lanes(out_off, True), state_spec],
        out_shape=[jax.ShapeDtypeStruct(cat.shape, F32), jax.ShapeDtypeStruct(state_t.shape, F32)],
        scratch_shapes=[pltpu.VMEM((LANES, n_seq), F32)] * 4,
        input_output_aliases=aliases,
        compiler_params=pltpu.CompilerParams(dimension_semantics=("arbitrary",), vmem_limit_bytes=VMEM_LIMIT_BYTES),
        name=f"batched_recurrence_dk{dk}",
    )(*args)


W_IN_ROWS = 256
LR_SRC = 1024
DT_SRC = N_IN - SSM_H


def _arrange_w_in_kernel(w_ref, o_ref):
    o_ref[:, 0:LR_SRC] = w_ref[:, 0:LR_SRC].astype(BF16)
    o_ref[:, LR_SRC:OFF_LR] = w_ref[:, LR_SRC + GLA_RANK:DT_SRC].astype(BF16)
    lane = lax.broadcasted_iota(jnp.int32, (W_IN_ROWS, LANES), 1)
    lr = w_ref[:, LR_SRC:LR_SRC + LANES]
    dt = jnp.pad(w_ref[:, OFF_LR:N_IN], ((0, 0), (0, OFF_DTX - N_IN)))
    tail = jnp.where(lane < GLA_RANK, lr, jnp.where(lane < GLA_RANK + SSM_H, dt, 0.0))
    o_ref[:, OFF_LR:OFF_DTX] = tail.astype(BF16)
    head = lax.broadcasted_iota(jnp.int32, (W_IN_ROWS, C_MIX), 1) // SSM_P
    rep = jnp.zeros((W_IN_ROWS, C_MIX), F32)
    for h in range(SSM_H):
        rep = jnp.where(head == h, w_ref[:, DT_SRC + h:DT_SRC + h + 1], rep)
    o_ref[:, OFF_DTX:W_PROJ] = rep.astype(BF16)


def _arrange_w_in(w):
    assert OFF_LR + GLA_RANK == DT_SRC
    return pl.pallas_call(
        _arrange_w_in_kernel,
        grid=(DEPTH, D_MODEL // W_IN_ROWS),
        in_specs=[pl.BlockSpec((None, W_IN_ROWS, N_IN), lambda l, i: (l, i, 0))],
        out_specs=pl.BlockSpec((None, W_IN_ROWS, W_PROJ), lambda l, i: (l, i, 0)),
        out_shape=jax.ShapeDtypeStruct((DEPTH, D_MODEL, W_PROJ), BF16),
        compiler_params=pltpu.CompilerParams(dimension_semantics=("arbitrary", "arbitrary")),
        name="arrange_w_in",
    )(w)


def _pool_block_diag(pw):
    eye = jnp.eye(len(POOL_WINDOWS), dtype=pw.dtype)
    return jnp.einsum('lgcd,gh->lgchd', pw, eye).reshape(pw.shape[0], C_MIX, C_MIX)


def _param_table(pool_scale, gla_gate_bias, gla_norm, hgrn_norm, ssm_conv_b, ssm_dt_bias, ssm_A_log, ssm_D,
                 ssm_norm, ssm_conv_w, hgrn_lb_logits):
    def rows(v):
        v = v.astype(F32).reshape(DEPTH, -1, v.shape[-1])
        return jnp.pad(v, ((0, 0), (0, 0), (0, SSM_CONV_DIM - v.shape[-1])))
    lbl = jnp.broadcast_to(hgrn_lb_logits[None], (DEPTH,) + hgrn_lb_logits.shape)
    parts = [rows(pool_scale), rows(gla_gate_bias), rows(jnp.tile(gla_norm, (1, GLA_H))),
             rows(jnp.tile(hgrn_norm, (1, HG_H))), rows(ssm_conv_b), rows(jnp.repeat(ssm_dt_bias, SSM_P, axis=-1)),
             rows(jnp.repeat(ssm_A_log, SSM_P, axis=-1)), rows(jnp.repeat(ssm_D, SSM_P, axis=-1)), rows(ssm_norm),
             rows(ssm_conv_w), rows(lbl)]
    tab = jnp.concatenate(parts, axis=1)
    return jnp.pad(tab, ((0, 0), (0, PT_ROWS - tab.shape[1]), (0, 0)))


def kernel(x_prompt, x_sample, state_pool, state_gla, state_hgrn, state_ssm, state_conv, ffn1_norm, ffn1_w_gate, ffn1_w_up, ffn1_w_down, mix_norm, w_in, pool_w, pool_scale, gla_w_gate, gla_gate_bias, gla_norm, hgrn_lb_logits, hgrn_norm, ssm_conv_w, ssm_conv_b, ssm_dt_bias, ssm_A_log, ssm_D, ssm_norm, w_out, ffn2_norm, ffn2_w_gate, ffn2_w_up, ffn2_w_down, final_norm):
    bp, tp, _ = x_prompt.shape
    bs, ts, _ = x_sample.shape
    rows_p, rows_s = bp * tp, bs * ts
    gain = lambda v: v.reshape(DEPTH, 1, D_MODEL).astype(F32)
    bf = lambda w: w.astype(BF16)

    wgate = bf(jnp.pad(gla_w_gate, ((0, 0), (0, 128 - GLA_RANK), (0, 0))))
    ptab = _param_table(pool_scale, gla_gate_bias, gla_norm, hgrn_norm, ssm_conv_b, ssm_dt_bias, ssm_A_log, ssm_D,
                        ssm_norm, ssm_conv_w, hgrn_lb_logits)
    mats = [bf(_pool_block_diag(pool_w)), ptab]
    seq_last = lambda s: jnp.transpose(s, (0, 2, 3, 4, 1)).reshape(DEPTH, -1, s.shape[-1], bs)
    carried = (jnp.transpose(state_pool, (0, 2, 1, 3)), state_ssm.reshape(DEPTH, bs, SSM_G, 128, SSM_N),
               jnp.transpose(state_conv, (0, 2, 1, 3)))
    gla_t, hg_t = seq_last(state_gla), seq_last(state_hgrn)
    ffn1 = (gain(ffn1_norm), bf(ffn1_w_gate), bf(ffn1_w_up), bf(ffn1_w_down))
    ffn2 = (gain(ffn2_norm), bf(ffn2_w_gate), bf(ffn2_w_up), bf(ffn2_w_down))
    gm, win, wout = gain(mix_norm), _arrange_w_in(w_in), bf(w_out)
    gf = final_norm.reshape(1, D_MODEL).astype(F32)

    assert bp * PROMPT_CHUNK == ROW_TILE and rows_s % ROW_TILE == 0
    n_first = rows_p // ROW_TILE
    xs = [x_prompt, x_sample.reshape(rows_s, D_MODEL)]
    states_p = states_s = new_gla_t = new_hg_t = None
    for l in range(DEPTH):
        x1, feat = _pre_call(xs, *ffn1, gm, win, wgate, ptab, l, rows_p + rows_s, n_first)
        outs_p = _mixer_call(feat, None, None, states_p, mats, n_seq=bp, seq_len=tp, row0=0, chunk_major=True,
                             chunk=PROMPT_CHUNK, group=PROMPT_GROUP, pos0=0, layer=l, lin=True)
        outs_s = _mixer_call(feat, outs_p[0], carried, states_s, mats, n_seq=bs, seq_len=ts, row0=rows_p,
                             chunk_major=False, chunk=ts, group=SAMPLE_GROUP, pos0=PAST_LEN, layer=l, lin=False)
        states_p, states_s = outs_p[1:], outs_s[1:]
        sample = dict(layer=l, row0=rows_p, n_seq=bs, seq_len=ts)
        cat, new_gla_t = _batched_recurrence_call(feat, outs_s[0], gla_t, new_gla_t, q_off=OFF_GQ, k_off=OFF_GK,
                                                  v_off=OFF_GV, la_off=OFF_LR, out_off=C_MIX, n_heads=GLA_H,
                                                  dk=GLA_DK, **sample)
        cat, new_hg_t = _batched_recurrence_call(feat, cat, hg_t, new_hg_t, q_off=OFF_RQ, k_off=None, v_off=OFF_RI,
                                                 la_off=OFF_RF, out_off=2 * C_MIX, n_heads=HG_H, dk=HG_DK, **sample)
        final_shapes = ((bp, tp, D_MODEL), (rows_s, D_MODEL)) if l == DEPTH - 1 else None
        xs = _post_call(x1, cat, feat, ptab, wout, *ffn2, gf, l, n_first, final_shapes)

    seq_first = lambda s, h, dk: jnp.transpose(s.reshape(DEPTH, h, dk, s.shape[-2], bs), (0, 4, 1, 2, 3))
    time_minor = lambda s: jnp.transpose(s, (0, 2, 1, 3))
    pool_p, gla_p, hg_p, ssm_p, conv_p = states_p
    pool_s, ssm_s, conv_s = states_s
    return (xs[0], xs[1].reshape(bs, ts, D_MODEL),
            pool_p, time_minor(pool_s),
            gla_p.reshape(DEPTH, bp, GLA_H, GLA_DK, GLA_DV), seq_first(new_gla_t, GLA_H, GLA_DK),
            hg_p.reshape(DEPTH, bp, HG_H, HG_DK, HG_DV), seq_first(new_hg_t, HG_H, HG_DK),
            ssm_p.reshape(DEPTH, bp, SSM_H, SSM_P, SSM_N), ssm_s.reshape(DEPTH, bs, SSM_H, SSM_P, SSM_N),
            conv_p, time_minor(conv_s))
```

```python
import functools

import jax
import jax.numpy as jnp
from jax import lax
from jax.experimental import pallas as pl
from jax.experimental.pallas import tpu as pltpu

F32 = jnp.float32
BF16 = jnp.bfloat16

D_MODEL = 1024
D_FF = 2816
DEPTH = 2
EPS = 1e-6
PAST_LEN = 16384

C_MIX = 256
POOL_WINDOWS = (2, 4, 8, 16)
POOL_GC = 64
POOL_PAST = 15
GLA_H, GLA_DK, GLA_DV, GLA_RANK, GLA_TAU = 4, 32, 64, 16, 16.0
HG_H, HG_DK, HG_DV = 4, 64, 64
SSM_H, SSM_P, SSM_G, SSM_N, SSM_CONV = 4, 64, 2, 128, 4
SSM_CONV_DIM = 768
N_IN = 3092

OFF_PX, OFF_GQ, OFF_GK, OFF_GV, OFF_GR = 0, 256, 384, 512, 768
OFF_RQ, OFF_RF, OFF_RI, OFF_RG = 1024, 1280, 1536, 1792
OFF_SZ, OFF_XBC = 2048, 2304
OFF_LR = 3072
OFF_DTX = 3200
W_PROJ = 3456

PT_POOL_SCALE, PT_GBIAS, PT_GNORM, PT_HNORM, PT_CONVB, PT_DTB, PT_ALOG, PT_DSKIP, PT_SNORM = range(9)
PT_CONVW = 9
PT_LBL = 13
PT_ROWS = 16

VMEM_LIMIT_BYTES = 56 * 1024 * 1024
SUBLANES = 8
LANES = 128
HEADS_PER_STEP = LANES // GLA_DV

ROW_TILE = 512
FF_CHUNKS = (0, 768, 1536, 2304, D_FF)
PROMPT_CHUNK = 64
PROMPT_GROUP = 8
SAMPLE_GROUP = 8
SUB_BLOCK = 16
MAX_FACTORED_SPAN = 60.0
POOL_BASE = 24


def _sigmoid(x):
    return 1.0 / (1.0 + jnp.exp(-x))


def _silu(x):
    return x * _sigmoid(x)


def _log_sigmoid(x):
    return jnp.minimum(x, 0.0) - jnp.log(1.0 + jnp.exp(-jnp.abs(x)))


def _softplus(x):
    return jnp.maximum(x, 0.0) + jnp.log(1.0 + jnp.exp(-jnp.abs(x)))


def _rms(x, g):
    ms = jnp.mean(x * x, axis=-1, keepdims=True)
    return x * lax.rsqrt(ms + EPS) * g


def _dot(a, b):
    return jnp.dot(a.astype(BF16), b.astype(BF16), preferred_element_type=F32)


def _dot_nt(a, b):
    return lax.dot_general(a.astype(BF16), b.astype(BF16), (((1,), (1,)), ((), ())),
                           preferred_element_type=F32)


def _dot_tn(a, b):
    return lax.dot_general(a.astype(BF16), b.astype(BF16), (((0,), (0,)), ((), ())),
                           preferred_element_type=F32)


def _split3(x):
    hi = x.astype(BF16)
    r1 = x - hi.astype(F32)
    mid = r1.astype(BF16)
    lo = (r1 - mid.astype(F32)).astype(BF16)
    return hi, mid, lo


def _dot_exact_rhs(a01, x):
    hi, mid, lo = _split3(x)
    a = a01.astype(BF16)
    return (jnp.dot(a, hi, preferred_element_type=F32) + jnp.dot(a, mid, preferred_element_type=F32)
            + jnp.dot(a, lo, preferred_element_type=F32))


def _swiglu_half(x, g, wg_ref, wu_ref, wd_ref):
    h = _rms(x, g).astype(BF16)
    acc = None
    for lo, hi in zip(FF_CHUNKS[:-1], FF_CHUNKS[1:]):
        sl = slice(lo, hi)
        gate = jnp.dot(h, wg_ref[:, sl], preferred_element_type=F32)
        up = jnp.dot(h, wu_ref[:, sl], preferred_element_type=F32)
        act = (_silu(gate) * up).astype(BF16)
        d = jnp.dot(act, wd_ref[sl, :], preferred_element_type=F32)
        acc = d if acc is None else acc + d
    return x + 0.5 * acc


def _pre_kernel(*refs, n_first, split_in, layer):
    if split_in:
        x = jnp.where(pl.program_id(0) < n_first, refs[0][...].reshape(ROW_TILE, D_MODEL), refs[1][...])
        refs = refs[2:]
    else:
        x = refs[0][...]
        refs = refs[1:]
    g1_ref, wg_ref, wu_ref, wd_ref, gm_ref, win_ref, wgate_ref, ptab_ref, x1_ref, feat_ref = refs
    x1 = _swiglu_half(x, g1_ref[...], wg_ref, wu_ref, wd_ref)
    x1_ref[...] = x1
    h = _rms(x1, gm_ref[...]).astype(BF16)
    _mixer_features(h, win_ref, wgate_ref, ptab_ref, feat_ref, layer)


def _hgrn_lower_bound(ptab_ref, layer):
    lbl = ptab_ref[PT_LBL:PT_LBL + DEPTH, 0:C_MIX]
    lexp = jnp.exp(lbl - jnp.max(lbl, axis=0, keepdims=True))
    lsum = jnp.sum(lexp, axis=0, keepdims=True)
    lb = jnp.zeros((1, C_MIX), F32)
    for m in range(1, layer + 1):
        lb = lb + lexp[m:m + 1] / lsum
    return lb


def _mixer_features(h, win_ref, wgate_ref, ptab_ref, feat_ref, layer):
    prow = lambda r, w: ptab_ref[r:r + 1, 0:w]
    proj = lambda lo, hi: jnp.dot(h, win_ref[:, lo:hi], preferred_element_type=F32)
    d = proj(OFF_LR, W_PROJ)
    logit = _dot(d[:, 0:LANES], wgate_ref[...]) + prow(PT_GBIAS, LANES)
    feat_ref[:, OFF_LR:OFF_DTX] = _log_sigmoid(logit) * (1.0 / GLA_TAU)
    feat_ref[:, OFF_DTX:W_PROJ] = _softplus(d[:, LANES:LANES + C_MIX] + prow(PT_DTB, C_MIX))
    b = proj(OFF_RQ, OFF_XBC)
    feat_ref[:, OFF_RQ:OFF_RF] = _silu(b[:, 0:C_MIX])
    lb = _hgrn_lower_bound(ptab_ref, layer)
    log_lb = jnp.log(lb)
    t2 = jnp.log(1.0 - lb) + _log_sigmoid(b[:, C_MIX:2 * C_MIX])
    feat_ref[:, OFF_RF:OFF_RI] = jnp.maximum(log_lb, t2) + jnp.log(1.0 + jnp.exp(-jnp.abs(log_lb - t2)))
    feat_ref[:, OFF_RI:OFF_RG] = b[:, 2 * C_MIX:3 * C_MIX]
    feat_ref[:, OFF_RG:OFF_XBC] = _silu(b[:, 3 * C_MIX:5 * C_MIX])
    a = proj(OFF_PX, OFF_RQ)
    feat_ref[:, OFF_GR:OFF_RQ] = _silu(a[:, OFF_GR:OFF_RQ])
    feat_ref[:, OFF_GQ:OFF_GK] = a[:, OFF_GQ:OFF_GK] * (GLA_DK ** -0.5)
    feat_ref[:, OFF_PX:OFF_GQ] = a[:, OFF_PX:OFF_GQ]
    feat_ref[:, OFF_GK:OFF_GR] = a[:, OFF_GK:OFF_GR]
    feat_ref[:, OFF_XBC:OFF_LR] = proj(OFF_XBC, OFF_LR)


def _segment_rms(o, width, gain):
    n = o.shape[-1]
    seg = (lax.broadcasted_iota(jnp.int32, (n, n), 0) // width
           == lax.broadcasted_iota(jnp.int32, (n, n), 1) // width).astype(BF16)
    sq = o * o
    hi = sq.astype(BF16)
    lo = (sq - hi.astype(F32)).astype(BF16)
    ms = (jnp.dot(hi, seg, preferred_element_type=F32) + jnp.dot(lo, seg, preferred_element_type=F32)) * (1.0 / width)
    return o * lax.rsqrt(ms + EPS) * gain


def _mixer_outputs(cat_ref, ggla_ref, ghg_ref, gz_ref, ptab_ref):
    prow = lambda r, w: ptab_ref[r:r + 1, 0:w]
    o_gla = _segment_rms(cat_ref[:, C_MIX:2 * C_MIX], GLA_DV, prow(PT_GNORM, C_MIX)) * ggla_ref[...]
    o_hg = _segment_rms(cat_ref[:, 2 * C_MIX:3 * C_MIX], HG_DV, prow(PT_HNORM, C_MIX)) * ghg_ref[...]
    y = cat_ref[:, 3 * C_MIX:4 * C_MIX] * gz_ref[...]
    half = C_MIX // SSM_G
    o_ssm = [_rms(y[:, half * g:half * (g + 1)], ptab_ref[PT_SNORM:PT_SNORM + 1, half * g:half * (g + 1)])
             for g in range(SSM_G)]
    return jnp.concatenate([cat_ref[:, 0:C_MIX], o_gla, o_hg] + o_ssm, axis=1)


def _post_kernel(x_ref, cat_ref, ggla_ref, ghg_ref, gz_ref, ptab_ref, wout_ref, g2_ref, wg_ref, wu_ref, wd_ref,
                 gf_ref, *o_refs, n_first, final):
    mixed = _mixer_outputs(cat_ref, ggla_ref, ghg_ref, gz_ref, ptab_ref)
    x2 = x_ref[...] + jnp.dot(mixed.astype(BF16), wout_ref[...], preferred_element_type=F32)
    x3 = _swiglu_half(x2, g2_ref[...], wg_ref, wu_ref, wd_ref)
    if final:
        y = _rms(x3, gf_ref[...])
        i = pl.program_id(0)

        @pl.when(i < n_first)
        def _():
            o_refs[0][...] = y.reshape(o_refs[0].shape)

        @pl.when(i >= n_first)
        def _():
            o_refs[1][...] = y
    else:
        o_refs[0][...] = x3


def _layer_spec(shape, layer):
    nd = len(shape)
    return pl.BlockSpec((None,) + tuple(shape), lambda *_: (layer,) + (0,) * nd, pipeline_mode=pl.Buffered(1))


def _dense_params():
    return pltpu.CompilerParams(dimension_semantics=("arbitrary",), vmem_limit_bytes=VMEM_LIMIT_BYTES)


def _row_spec(width):
    return pl.BlockSpec((ROW_TILE, width), lambda i: (i, 0))


def _token_specs(n_first, n_prompt_seq):
    return [pl.BlockSpec((n_prompt_seq, PROMPT_CHUNK, D_MODEL), lambda i: (0, jnp.minimum(i, n_first - 1), 0)),
            pl.BlockSpec((ROW_TILE, D_MODEL), lambda i: (jnp.maximum(i - n_first, 0), 0))]


def _pre_call(xs, g1, wg, wu, wd, gm, win, wgate, ptab, layer, rows, n_first):
    split_in = len(xs) == 2
    x_specs = _token_specs(n_first, xs[0].shape[0]) if split_in else [_row_spec(D_MODEL)]
    return pl.pallas_call(
        functools.partial(_pre_kernel, n_first=n_first, split_in=split_in, layer=layer),
        grid=(rows // ROW_TILE,),
        in_specs=x_specs + [_layer_spec((1, D_MODEL), layer), _layer_spec((D_MODEL, D_FF), layer),
                            _layer_spec((D_MODEL, D_FF), layer), _layer_spec((D_FF, D_MODEL), layer),
                            _layer_spec((1, D_MODEL), layer),
                            pl.BlockSpec((D_MODEL, W_PROJ), lambda i: (0, 0), pipeline_mode=pl.Buffered(1)),
                            _layer_spec(wgate.shape[1:], layer), _layer_spec(ptab.shape[1:], layer)],
        out_specs=[_row_spec(D_MODEL), _row_spec(W_PROJ)],
        out_shape=[jax.ShapeDtypeStruct((rows, D_MODEL), F32), jax.ShapeDtypeStruct((rows, W_PROJ), F32)],
        compiler_params=_dense_params(),
        name="pre_ffn_inproj",
    )(*xs, g1, wg, wu, wd, gm, win, wgate, ptab)


def _post_call(x, cat, feat, ptab, wout, g2, wg, wu, wd, gf, layer, n_first, final_shapes):
    rows = x.shape[0]
    if final_shapes is not None:
        out_specs = _token_specs(n_first, final_shapes[0][0])
        out_shape = [jax.ShapeDtypeStruct(s, F32) for s in final_shapes]
    else:
        out_specs = [_row_spec(D_MODEL)]
        out_shape = [jax.ShapeDtypeStruct((rows, D_MODEL), F32)]
    gate_spec = lambda off: pl.BlockSpec((ROW_TILE, C_MIX), lambda i: (i, off // C_MIX))
    return pl.pallas_call(
        functools.partial(_post_kernel, n_first=n_first, final=final_shapes is not None),
        grid=(rows // ROW_TILE,),
        in_specs=[_row_spec(D_MODEL), _row_spec(D_MODEL), gate_spec(OFF_GR), gate_spec(OFF_RG), gate_spec(OFF_SZ),
                  _layer_spec(ptab.shape[1:], layer), _layer_spec((D_MODEL, D_MODEL), layer),
                  _layer_spec((1, D_MODEL), layer), _layer_spec((D_MODEL, D_FF), layer),
                  _layer_spec((D_MODEL, D_FF), layer), _layer_spec((D_FF, D_MODEL), layer),
                  pl.BlockSpec((1, D_MODEL), lambda i: (0, 0))],
        out_specs=out_specs,
        out_shape=out_shape,
        compiler_params=_dense_params(),
        name="post_outproj_ffn",
    )(x, cat, feat, feat, feat, ptab, wout, g2, wg, wu, wd, gf)


def _lane_group(shape, width):
    return lax.broadcasted_iota(jnp.int32, shape, len(shape) - 1) // width


def _block_diag_mask(rows, row_w, cols, col_w):
    r = lax.broadcasted_iota(jnp.int32, (rows, cols), 0) // row_w
    c = lax.broadcasted_iota(jnp.int32, (rows, cols), 1) // col_w
    return r == c


def _tril(n):
    r = lax.broadcasted_iota(jnp.int32, (n, n), 0)
    c = lax.broadcasted_iota(jnp.int32, (n, n), 1)
    return r >= c


def _select_heads(stacked, n_heads, rows, head_w):
    grp = _lane_group((rows, n_heads * head_w), head_w)
    out = jnp.zeros((rows, n_heads * head_w), F32)
    for h in range(n_heads):
        out = jnp.where(grp == h, stacked[h * rows:(h + 1) * rows], out)
    return out


def _cum_log_decay(la):
    return _dot_exact_rhs(_tril(la.shape[0]).astype(F32), la)


def _decay_span(b):
    mid = b.shape[0] // 2 - 1
    return jnp.max(jnp.abs(b - b[mid:mid + 1, :]))


def _la_factored(q, k, v, b, st, n_heads, dk):
    chunk = q.shape[0]
    nk, nv = n_heads * dk, n_heads * 64
    mid = chunk // 2 - 1
    m = b[mid:mid + 1, :]
    qt = q * jnp.exp(b - m)
    kt = k * jnp.exp(m - b)
    k_bd = jnp.where(_block_diag_mask(n_heads * chunk, chunk, nk, dk), jnp.concatenate([kt] * n_heads, axis=0), 0.0)
    att = _dot_nt(qt, k_bd)
    row = lax.broadcasted_iota(jnp.int32, att.shape, 0)
    col = lax.broadcasted_iota(jnp.int32, att.shape, 1) % chunk
    att = jnp.where(row >= col, att, 0.0)
    v_bd = jnp.where(_block_diag_mask(n_heads * chunk, chunk, nv, 64), jnp.concatenate([v] * n_heads, axis=0), 0.0)
    o = _dot(att, v_bd) + _dot_nt(qt * jnp.exp(m), st)
    b_last = b[chunk - 1:chunk, :]
    upd = _dot_tn(v, kt * jnp.exp(b_last - m))
    st_new = st * jnp.exp(b_last) + jnp.where(_block_diag_mask(nv, 64, nk, dk), upd, 0.0)
    return o, st_new


def _la_direct(q, k, v, b, st, n_heads, dk, sub):
    chunk = q.shape[0]
    nk = n_heads * dk
    nv = n_heads * 64
    n_sub = chunk // sub
    refs = [None] + [b[sub * i - 1:sub * i, :] for i in range(1, n_sub)]
    c_loc = jnp.concatenate([b[0:sub]] + [b[sub * i:sub * (i + 1)] - refs[i] for i in range(1, n_sub)], axis=0) \
        if n_sub > 1 else b
    o = _dot_nt(q * jnp.exp(b), st)

    qt = q * jnp.exp(c_loc)
    kgrp = _lane_group((sub, nk), dk)
    outs = []
    for i in range(n_sub):
        rows = slice(sub * i, sub * (i + 1))
        o_i = o[rows]
        if i > 0:
            prev = slice(0, sub * i)
            kt = k[prev] * jnp.exp(refs[i] - b[prev])
            q_stack = jnp.concatenate([jnp.where(kgrp == h, qt[rows], 0.0) for h in range(n_heads)], axis=0)
            att = _dot_nt(q_stack, kt)
            o_i = o_i + _select_heads(_dot(att, v[prev]), n_heads, sub, 64)
        c_i = c_loc[rows]
        q_i = q[rows]
        row_id = lax.broadcasted_iota(jnp.int32, (sub, nk), 0)
        e_list = []
        for j in range(sub):
            c_j = c_loc[sub * i + j:sub * i + j + 1, :]
            k_j = k[sub * i + j:sub * i + j + 1, :]
            dec = jnp.exp(jnp.where(row_id >= j, c_i - c_j, -jnp.inf))
            e_list.append(dec * q_i * k_j)
        e_all = jnp.concatenate(e_list, axis=0)
        expand = _block_diag_mask(nk, dk, nv, 64).astype(F32)
        r_d = _dot(e_all, expand)
        for j in range(sub):
            o_i = o_i + r_d[j * sub:(j + 1) * sub] * v[sub * i + j:sub * i + j + 1, :]
        outs.append(o_i)
    o = jnp.concatenate(outs, axis=0) if n_sub > 1 else outs[0]

    b_last = b[chunk - 1:chunk, :]
    k_end = k * jnp.exp(b_last - b)
    upd = _dot_tn(v, k_end)
    st_new = st * jnp.exp(b_last) + jnp.where(_block_diag_mask(nv, 64, nk, dk), upd, 0.0)
    return o, st_new


def _load_state_t(s, n_heads, dk):
    st = s.T
    tiled = jnp.concatenate([st] * n_heads, axis=0)
    return jnp.where(_block_diag_mask(n_heads * 64, 64, n_heads * dk, dk), tiled, 0.0)


def _store_state_t(st, n_heads, dk):
    acc = st[0:64]
    for h in range(1, n_heads):
        acc = acc + st[64 * h:64 * (h + 1)]
    return acc.T


def _pool_mixer(proj_ref, pool_scr, poolw_ref, scale, chunk, pos_first):
    xp = proj_ref[:, OFF_PX:OFF_PX + C_MIX]
    pool_scr[0, POOL_BASE:POOL_BASE + chunk, :] = xp
    n_ext = POOL_BASE - SUBLANES + chunk
    for s, shift in enumerate((1, 2, 4)):
        cur = pool_scr[s, SUBLANES:SUBLANES + n_ext, :]
        prev = pool_scr[s, SUBLANES - shift:SUBLANES - shift + n_ext, :]
        pool_scr[s + 1, SUBLANES:SUBLANES + n_ext, :] = cur + prev
    s2 = pool_scr[1, POOL_BASE:POOL_BASE + chunk, :]
    s4 = pool_scr[2, POOL_BASE:POOL_BASE + chunk, :]
    s8 = pool_scr[3, POOL_BASE:POOL_BASE + chunk, :]
    s16 = s8 + pool_scr[3, POOL_BASE - 8:POOL_BASE - 8 + chunk, :]
    grp = _lane_group((chunk, C_MIX), POOL_GC)
    win_sum = jnp.where(grp == 0, s2, jnp.where(grp == 1, s4, jnp.where(grp == 2, s8, s16)))
    win = jnp.where(grp == 0, 2, jnp.where(grp == 1, 4, jnp.where(grp == 2, 8, 16)))
    pos = pos_first + lax.broadcasted_iota(jnp.int32, (chunk, C_MIX), 0)
    cnt = jnp.minimum(pos + 1, win).astype(F32)
    d = win_sum / cnt - xp
    o_pool = _dot(d, poolw_ref[...]) * scale
    new_pool = pool_scr[0, POOL_BASE + chunk - POOL_PAST:POOL_BASE + chunk, :]
    pool_scr[0, POOL_BASE - POOL_PAST:POOL_BASE, :] = new_pool
    return o_pool, new_pool


def _ssd_conv(proj_ref, conv_scr, prow, chunk):
    conv_base = SUBLANES
    conv_scr[conv_base:conv_base + chunk, :] = proj_ref[:, OFF_XBC:OFF_XBC + SSM_CONV_DIM]
    conv = None
    for w in range(SSM_CONV):
        start = conv_base - (SSM_CONV - 1) + w
        term = conv_scr[start:start + chunk, :] * prow(PT_CONVW + w, SSM_CONV_DIM)
        conv = term if conv is None else conv + term
    conv = _silu(conv + prow(PT_CONVB, SSM_CONV_DIM))
    new_conv = conv_scr[conv_base + chunk - (SSM_CONV - 1):conv_base + chunk, :]
    conv_scr[conv_base - (SSM_CONV - 1):conv_base, :] = new_conv
    return conv, new_conv


def _ssd_scan(proj_ref, conv, ssm_st, prow, chunk):
    xs = conv[:, 0:C_MIX]
    dt = proj_ref[:, OFF_DTX:OFF_DTX + C_MIX]
    a_neg = -jnp.exp(prow(PT_ALOG, C_MIX))
    la = dt * a_neg
    xdt = xs * dt
    cum = _cum_log_decay(la)
    heads_per_group = SSM_H // SSM_G
    b_of = lambda g: conv[:, C_MIX + SSM_N * g:C_MIX + SSM_N * (g + 1)]
    c_of = lambda g: conv[:, C_MIX + SSM_G * SSM_N + SSM_N * g:C_MIX + SSM_G * SSM_N + SSM_N * (g + 1)]
    if chunk == SSM_P:
        row = lax.broadcasted_iota(jnp.int32, (chunk, C_MIX), 0)
        col = lax.broadcasted_iota(jnp.int32, (chunk, C_MIX), 1) % chunk
        cum_j = jnp.sum(jnp.where(row == col, cum, 0.0), axis=0, keepdims=True)
        dec = jnp.exp(jnp.where(row >= col, cum - cum_j, -jnp.inf))
        cb = jnp.concatenate([_dot_nt(c_of(g), jnp.concatenate([b_of(g)] * heads_per_group, axis=0))
                              for g in range(SSM_G)], axis=1)
        x_bd = jnp.where(_block_diag_mask(SSM_H * chunk, chunk, C_MIX, SSM_P),
                         jnp.concatenate([xdt] * SSM_H, axis=0), 0.0)
        y = _dot(cb * dec, x_bd)
    else:
        cum_t = cum.T
        causal = _tril(chunk)
        att_rows = []
        for h in range(SSM_H):
            if h % heads_per_group == 0:
                cb = _dot_nt(c_of(h // heads_per_group), b_of(h // heads_per_group))
            col = cum[:, SSM_P * h:SSM_P * h + 1]
            row = cum_t[SSM_P * h:SSM_P * h + 1, :]
            att_rows.append(cb * jnp.exp(jnp.where(causal, col - row, -jnp.inf)))
        y = _select_heads(_dot(jnp.concatenate(att_rows, axis=0), xdt), SSM_H, chunk, SSM_P)
    cum_last = cum[chunk - 1:chunk, :]
    decay_in = jnp.exp(cum)
    xw = xdt * jnp.exp(cum_last - cum)
    decay_state = jnp.exp(cum_last)
    y_state = []
    for g in range(SSM_G):
        lanes = slice(SSM_N * g, SSM_N * (g + 1))
        st = ssm_st[g]
        y_state.append(_dot(c_of(g), st))
        ssm_st[g] = st * decay_state[:, lanes] + _dot_tn(b_of(g), xw[:, lanes])
    return y + jnp.concatenate(y_state, axis=1) * decay_in + prow(PT_DSKIP, C_MIX) * xs


def _mixer_kernel(*refs, chunk, group, pos0, layer, carried, n_alias, lin):
    feat_all = refs[0]
    n_state = 5 if lin else 3
    n_in = 1 + (n_state if carried else 0)
    poolw_ref, ptab_ref = refs[n_in:n_in + 2]
    n_in += 2 + n_alias
    cat_all = refs[n_in]
    new_states = refs[n_in + 1:n_in + 1 + n_state]
    scratch = refs[n_in + 1 + n_state:]
    if lin:
        npool_ref, ngla_ref, nhg_ref, nssm_ref, nconv_ref = new_states
        pool_scr, conv_scr, gla_st, hg_st, ssm_st = scratch
    else:
        npool_ref, nssm_ref, nconv_ref = new_states
        pool_scr, conv_scr, ssm_st = scratch
    seq_state = (lambda ref, s: ref.at[s]) if lin else (lambda ref, s: ref.at[:, s])
    prow = lambda r, w: ptab_ref[r:r + 1, 0:w]
    c = pl.program_id(1)
    n_c = pl.num_programs(1)
    sub = min(SUB_BLOCK, chunk)
    conv_base = SUBLANES

    @pl.when(c == 0)
    def _():
        pool_scr[...] = jnp.zeros(pool_scr.shape, F32)
        conv_scr[...] = jnp.zeros(conv_scr.shape, F32)
        if carried:
            carried_refs = refs[1:1 + n_state]
            poolb_ref, ssms_ref, convb_ref = carried_refs[0], carried_refs[-2], carried_refs[-1]
            for s in range(group):
                pool_scr[s, 0, POOL_BASE - POOL_PAST:POOL_BASE, :] = seq_state(poolb_ref, s)[...]
                conv_scr[s, conv_base - (SSM_CONV - 1):conv_base, :] = seq_state(convb_ref, s)[...]
                for g in range(SSM_G):
                    ssm_st[s, g] = ssms_ref[s, g].T
                if lin:
                    gla_st[s] = _load_state_t(carried_refs[1][s], GLA_H, GLA_DK)
                    hg_st[s] = _load_state_t(carried_refs[2][s], HG_H, HG_DK)
        else:
            ssm_st[...] = jnp.zeros(ssm_st.shape, F32)
            if lin:
                gla_st[...] = jnp.zeros(gla_st.shape, F32)
                hg_st[...] = jnp.zeros(hg_st.shape, F32)

    seq_rows = lambda ref, s: ref.at[pl.ds(s * chunk, chunk)]
    new_pools, new_convs, convs, cum_decays, spans = [], [], [], [], []

    def conv_stage(s):
        conv, new_conv = _ssd_conv(seq_rows(feat_all, s), conv_scr.at[s], prow, chunk)
        convs.append(conv)
        new_convs.append(new_conv)

    def decay_stage(s):
        feat_ref = seq_rows(feat_all, s)
        b_gla = _cum_log_decay(feat_ref[:, OFF_LR:OFF_LR + GLA_H * GLA_DK])
        b_hg = _cum_log_decay(feat_ref[:, OFF_RF:OFF_RF + C_MIX])
        cum_decays.extend([b_gla, b_hg])
        spans.extend([_decay_span(b_gla), _decay_span(b_hg)])

    def scan_stage(s):
        seq_rows(cat_all, s)[:, 3 * C_MIX:4 * C_MIX] = _ssd_scan(seq_rows(feat_all, s), convs[s], ssm_st.at[s], prow,
                                                                chunk)

    def pool_stage(s):
        o_pool, new_pool = _pool_mixer(seq_rows(feat_all, s), pool_scr.at[s], poolw_ref, prow(PT_POOL_SCALE, C_MIX),
                                       chunk, pos0 + c * chunk)
        seq_rows(cat_all, s)[:, 0:C_MIX] = o_pool
        new_pools.append(new_pool)

    stages = (conv_stage, decay_stage, scan_stage, pool_stage) if lin else (conv_stage, scan_stage, pool_stage)
    if chunk >= PROMPT_CHUNK:
        for stage in stages:
            for s in range(group):
                stage(s)
    else:
        for s in range(group):
            for stage in stages:
                stage(s)

    if lin:
        _linear_attention_step(feat_all, cat_all, gla_st, hg_st, cum_decays, spans, seq_rows, group, sub)
    else:
        for s in range(group):
            seq_rows(cat_all, s)[:, C_MIX:3 * C_MIX] = jnp.zeros((chunk, 2 * C_MIX), F32)

    @pl.when(c == n_c - 1)
    def _():
        for s in range(group):
            seq_state(npool_ref, s)[...] = new_pools[s]
            seq_state(nconv_ref, s)[...] = new_convs[s]
            for g in range(SSM_G):
                nssm_ref[s, g] = ssm_st[s, g].T
            if lin:
                ngla_ref[s] = _store_state_t(gla_st[s], GLA_H, GLA_DK)
                nhg_ref[s] = _store_state_t(hg_st[s], HG_H, HG_DK)


def _linear_attention_step(feat_all, cat_all, gla_st, hg_st, cum_decays, spans, seq_rows, group, sub):
    narrow = functools.reduce(jnp.maximum, spans) <= MAX_FACTORED_SPAN

    def run(la_gla, la_hg, *cums):
        for s in range(group):
            feat_ref, cat_ref = seq_rows(feat_all, s), seq_rows(cat_all, s)
            o_g, st_g = la_gla(feat_ref[:, OFF_GQ:OFF_GK], feat_ref[:, OFF_GK:OFF_GV], feat_ref[:, OFF_GV:OFF_GR],
                               cums[2 * s], gla_st[s])
            gla_st[s] = st_g
            cat_ref[:, C_MIX:2 * C_MIX] = o_g
            hk = 1.0 - jnp.exp(feat_ref[:, OFF_RF:OFF_RI])
            o_h, st_h = la_hg(feat_ref[:, OFF_RQ:OFF_RF], hk, feat_ref[:, OFF_RI:OFF_RG], cums[2 * s + 1], hg_st[s])
            hg_st[s] = st_h
            cat_ref[:, 2 * C_MIX:3 * C_MIX] = o_h

    factored = functools.partial(run, functools.partial(_la_factored, n_heads=GLA_H, dk=GLA_DK),
                                 functools.partial(_la_factored, n_heads=HG_H, dk=HG_DK))
    direct = functools.partial(run, functools.partial(_la_direct, n_heads=GLA_H, dk=GLA_DK, sub=sub),
                               functools.partial(_la_direct, n_heads=HG_H, dk=HG_DK, sub=sub))
    lax.cond(narrow, factored, direct, *cum_decays)


def _mixer_call(proj, cat_prev, carried, prev_states, mats, *, n_seq, seq_len, row0, chunk_major, chunk, group,
                pos0, layer, lin):
    rows = proj.shape[0]
    n_c, n_b, blk = seq_len // chunk, n_seq // group, group * chunk
    assert chunk_major or n_c == 1
    row_map = lambda b, c: (row0 // blk + c * n_b + b, 0)

    def seq_major(tail):
        nd = len(tail)
        return (DEPTH, n_seq) + tail, pl.BlockSpec((None, group) + tail, lambda b, c: (layer, b) + (0,) * nd)

    def time_major(n_rows, width):
        return (DEPTH, n_rows, n_seq, width), pl.BlockSpec((None, n_rows, group, width), lambda b, c: (layer, 0, b, 0))

    ssm = seq_major((SSM_G, 128, SSM_N))
    if lin:
        states = [seq_major((POOL_PAST, C_MIX)), seq_major((GLA_H * GLA_DK, GLA_DV)), seq_major((HG_H * HG_DK, HG_DV)),
                  ssm, seq_major((SSM_CONV - 1, SSM_CONV_DIM))]
    else:
        states = [time_major(POOL_PAST, C_MIX), ssm, time_major(SSM_CONV - 1, SSM_CONV_DIM)]

    def mat_spec(m):
        nd = m.ndim - 1
        return pl.BlockSpec((None,) + m.shape[1:], lambda b, c: (layer,) + (0,) * nd)

    state_specs = [spec for _, spec in states]
    in_specs = [pl.BlockSpec((blk, W_PROJ), row_map)]
    args = [proj]
    if carried is not None:
        in_specs += state_specs
        args += list(carried)
    in_specs += [mat_spec(m) for m in mats]
    args += list(mats)
    aliases = {}
    if cat_prev is not None:
        aliases[len(args)] = 0
        args.append(cat_prev)
    if prev_states is not None:
        for i, s in enumerate(prev_states):
            aliases[len(args)] = 1 + i
            args.append(s)
    in_specs += [pl.BlockSpec(memory_space=pl.ANY)] * len(aliases)
    out_shape = [jax.ShapeDtypeStruct((rows, D_MODEL), F32)] + [jax.ShapeDtypeStruct(shape, F32) for shape, _ in states]
    out_specs = [pl.BlockSpec((blk, D_MODEL), row_map)] + state_specs
    scratch = [pltpu.VMEM((group, 4, POOL_BASE + chunk, C_MIX), F32),
               pltpu.VMEM((group, SUBLANES + chunk, SSM_CONV_DIM), F32)]
    if lin:
        scratch += [pltpu.VMEM((group, GLA_H * GLA_DV, GLA_H * GLA_DK), F32),
                    pltpu.VMEM((group, HG_H * HG_DV, HG_H * HG_DK), F32)]
    scratch.append(pltpu.VMEM((group, SSM_G, SSM_N, 128), F32))
    return pl.pallas_call(
        functools.partial(_mixer_kernel, chunk=chunk, group=group, pos0=pos0, layer=layer,
                          carried=carried is not None, n_alias=len(aliases), lin=lin),
        grid=(n_b, n_c),
        in_specs=in_specs,
        out_specs=out_specs,
        out_shape=out_shape,
        scratch_shapes=scratch,
        input_output_aliases=aliases,
        compiler_params=pltpu.CompilerParams(dimension_semantics=("arbitrary", "arbitrary"),
                                             vmem_limit_bytes=VMEM_LIMIT_BYTES),
        name=f"mixer_chunk{chunk}",
    )(*args)


def _batched_recurrence_kernel(*refs, n_seq, seq_len, dk, own_keys, n_alias):
    n_in = 5 if own_keys else 4
    q_ref, v_ref, la_ref, s_in = refs[0], refs[n_in - 3], refs[n_in - 2], refs[n_in - 1]
    o_ref, s_out, q_t, k_t, a_t, v_t = refs[n_in + n_alias:]
    key_w = HEADS_PER_STEP * dk
    base = 0 if key_w == q_ref.shape[1] else pl.multiple_of(pl.program_id(0) * key_w, key_w)
    s_out[...] = s_in[...]
    for t in range(seq_len):
        rows = pl.ds(t, n_seq, stride=seq_len)
        a = jnp.exp(la_ref[rows, :])
        k = refs[1][rows, :] if own_keys else 1.0 - a
        q_t[...] = q_ref[rows, :].T
        k_t[...] = k.T
        a_t[...] = a.T
        v_t[...] = v_ref[rows, :].T
        heads = []
        for h in range(HEADS_PER_STEP):
            v_h = v_t[GLA_DV * h:GLA_DV * (h + 1), :]

            def body(i, acc, h=h, v_h=v_h):
                r = h * dk + i
                s = a_t[pl.ds(base + r, 1), :] * s_out[r] + k_t[pl.ds(base + r, 1), :] * v_h
                s_out[r] = s
                return acc + q_t[pl.ds(base + r, 1), :] * s

            heads.append(lax.fori_loop(0, dk, body, jnp.zeros((GLA_DV, n_seq), F32), unroll=4))
        o_ref[rows, :] = jnp.concatenate(heads, axis=0).T


def _batched_recurrence_call(feat, cat, state_t, prev_state, *, layer, row0, n_seq, seq_len, q_off, k_off, v_off,
                             la_off, out_off, n_heads, dk):
    n_rows = n_seq * seq_len
    rb = row0 // n_rows
    key_w = HEADS_PER_STEP * dk
    assert key_w in (LANES // 2, LANES) and n_heads % HEADS_PER_STEP == 0

    def lanes(off, per_step):
        return pl.BlockSpec((n_rows, LANES), lambda i: (rb, off // LANES + (i if per_step else 0)))

    key = lambda off: lanes(off, key_w == LANES)
    in_specs = [key(q_off)] + ([key(k_off)] if k_off is not None else []) + [lanes(v_off, True), key(la_off)]
    args = [feat] * len(in_specs)
    state_spec = pl.BlockSpec((None, key_w, GLA_DV, n_seq), lambda i: (layer, i, 0, 0))
    in_specs.append(state_spec)
    args.append(state_t)
    aliases = {len(args): 0}
    args.append(cat)
    if prev_state is not None:
        aliases[len(args)] = 1
        args.append(prev_state)
    in_specs += [pl.BlockSpec(memory_space=pl.ANY)] * len(aliases)
    return pl.pallas_call(
        functools.partial(_batched_recurrence_kernel, n_seq=n_seq, seq_len=seq_len, dk=dk,
                          own_keys=k_off is not None, n_alias=len(aliases)),
        grid=(n_heads // HEADS_PER_STEP,),
        in_specs=in_specs,
        out_specs=[lanes(out_off, True), state_spec],
        out_shape=[jax.ShapeDtypeStruct(cat.shape, F32), jax.ShapeDtypeStruct(state_t.shape, F32)],
        scratch_shapes=[pltpu.VMEM((LANES, n_seq), F32)] * 4,
        input_output_aliases=aliases,
        compiler_params=pltpu.CompilerParams(dimension_semantics=("arbitrary",), vmem_limit_bytes=VMEM_LIMIT_BYTES),
        name=f"batched_recurrence_dk{dk}",
    )(*args)


LR_SRC = 1024
DT_SRC = N_IN - SSM_H


def _arrange_w_in_kernel(wt_ref, *o_refs):
    def put(col, slab):
        for layer, o_ref in enumerate(o_refs):
            o_ref[:, col:col + LANES] = slab[:, layer, :].T.astype(BF16)

    for j in range(LR_SRC // LANES):
        put(LANES * j, wt_ref[LANES * j:LANES * (j + 1)])
    for j in range((OFF_LR - LR_SRC) // LANES):
        src = LR_SRC + GLA_RANK + LANES * j
        put(LR_SRC + LANES * j, wt_ref[src:src + LANES])
    row = lax.broadcasted_iota(jnp.int32, (LANES, DEPTH, D_MODEL), 0)
    dt_row = lambda h: jnp.broadcast_to(wt_ref[DT_SRC + h:DT_SRC + h + 1], (LANES, DEPTH, D_MODEL))
    tail = jnp.where(row < GLA_RANK, wt_ref[LR_SRC:LR_SRC + LANES], 0.0)
    for h in range(SSM_H):
        tail = jnp.where(row == GLA_RANK + h, dt_row(h), tail)
    put(OFF_LR, tail)
    for j in range(C_MIX // LANES):
        rep = jnp.zeros((LANES, DEPTH, D_MODEL), F32)
        for h in range(HEADS_PER_STEP * j, HEADS_PER_STEP * (j + 1)):
            rep = jnp.where(row // SSM_P == h - HEADS_PER_STEP * j, dt_row(h), rep)
        put(OFF_DTX + LANES * j, rep)


def _arrange_w_in(w):
    wt = jnp.transpose(w, (2, 0, 1))
    return pl.pallas_call(
        _arrange_w_in_kernel,
        grid=(1,),
        in_specs=[pl.BlockSpec(wt.shape, lambda i: (0, 0, 0), pipeline_mode=pl.Buffered(1))],
        out_specs=[pl.BlockSpec((D_MODEL, W_PROJ), lambda i: (0, 0))] * DEPTH,
        out_shape=[jax.ShapeDtypeStruct((D_MODEL, W_PROJ), BF16)] * DEPTH,
        compiler_params=pltpu.CompilerParams(dimension_semantics=("arbitrary",), vmem_limit_bytes=VMEM_LIMIT_BYTES),
        name="arrange_w_in",
    )(wt)


def _pool_block_diag(pw):
    eye = jnp.eye(len(POOL_WINDOWS), dtype=pw.dtype)
    return jnp.einsum('lgcd,gh->lgchd', pw, eye).reshape(pw.shape[0], C_MIX, C_MIX)


def _param_table(pool_scale, gla_gate_bias, gla_norm, hgrn_norm, ssm_conv_b, ssm_dt_bias, ssm_A_log, ssm_D,
                 ssm_norm, ssm_conv_w, hgrn_lb_logits):
    def rows(v):
        v = v.astype(F32).reshape(DEPTH, -1, v.shape[-1])
        return jnp.pad(v, ((0, 0), (0, 0), (0, SSM_CONV_DIM - v.shape[-1])))
    lbl = jnp.broadcast_to(hgrn_lb_logits[None], (DEPTH,) + hgrn_lb_logits.shape)
    parts = [rows(pool_scale), rows(gla_gate_bias), rows(jnp.tile(gla_norm, (1, GLA_H))),
             rows(jnp.tile(hgrn_norm, (1, HG_H))), rows(ssm_conv_b), rows(jnp.repeat(ssm_dt_bias, SSM_P, axis=-1)),
             rows(jnp.repeat(ssm_A_log, SSM_P, axis=-1)), rows(jnp.repeat(ssm_D, SSM_P, axis=-1)), rows(ssm_norm),
             rows(ssm_conv_w), rows(lbl)]
    tab = jnp.concatenate(parts, axis=1)
    return jnp.pad(tab, ((0, 0), (0, PT_ROWS - tab.shape[1]), (0, 0)))


def kernel(x_prompt, x_sample, state_pool, state_gla, state_hgrn, state_ssm, state_conv, ffn1_norm, ffn1_w_gate, ffn1_w_up, ffn1_w_down, mix_norm, w_in, pool_w, pool_scale, gla_w_gate, gla_gate_bias, gla_norm, hgrn_lb_logits, hgrn_norm, ssm_conv_w, ssm_conv_b, ssm_dt_bias, ssm_A_log, ssm_D, ssm_norm, w_out, ffn2_norm, ffn2_w_gate, ffn2_w_up, ffn2_w_down, final_norm):
    bp, tp, _ = x_prompt.shape
    bs, ts, _ = x_sample.shape
    rows_p, rows_s = bp * tp, bs * ts
    gain = lambda v: v.reshape(DEPTH, 1, D_MODEL).astype(F32)
    bf = lambda w: w.astype(BF16)

    wgate = bf(jnp.pad(gla_w_gate, ((0, 0), (0, 128 - GLA_RANK), (0, 0))))
    ptab = _param_table(pool_scale, gla_gate_bias, gla_norm, hgrn_norm, ssm_conv_b, ssm_dt_bias, ssm_A_log, ssm_D,
                        ssm_norm, ssm_conv_w, hgrn_lb_logits)
    mats = [bf(_pool_block_diag(pool_w)), ptab]
    seq_last = lambda s: jnp.transpose(s, (0, 2, 3, 4, 1)).reshape(DEPTH, -1, s.shape[-1], bs)
    carried = (jnp.transpose(state_pool, (0, 2, 1, 3)), state_ssm.reshape(DEPTH, bs, SSM_G, 128, SSM_N),
               jnp.transpose(state_conv, (0, 2, 1, 3)))
    gla_t, hg_t = seq_last(state_gla), seq_last(state_hgrn)
    ffn1 = (gain(ffn1_norm), bf(ffn1_w_gate), bf(ffn1_w_up), bf(ffn1_w_down))
    ffn2 = (gain(ffn2_norm), bf(ffn2_w_gate), bf(ffn2_w_up), bf(ffn2_w_down))
    gm, win, wout = gain(mix_norm), _arrange_w_in(w_in), bf(w_out)
    gf = final_norm.reshape(1, D_MODEL).astype(F32)

    assert bp * PROMPT_CHUNK == ROW_TILE and rows_s % ROW_TILE == 0
    n_first = rows_p // ROW_TILE
    xs = [x_prompt, x_sample.reshape(rows_s, D_MODEL)]
    states_p = states_s = new_gla_t = new_hg_t = None
    for l in range(DEPTH):
        x1, feat = _pre_call(xs, *ffn1, gm, win[l], wgate, ptab, l, rows_p + rows_s, n_first)
        outs_p = _mixer_call(feat, None, None, states_p, mats, n_seq=bp, seq_len=tp, row0=0, chunk_major=True,
                             chunk=PROMPT_CHUNK, group=PROMPT_GROUP, pos0=0, layer=l, lin=True)
        outs_s = _mixer_call(feat, outs_p[0], carried, states_s, mats, n_seq=bs, seq_len=ts, row0=rows_p,
                             chunk_major=False, chunk=ts, group=SAMPLE_GROUP, pos0=PAST_LEN, layer=l, lin=False)
        states_p, states_s = outs_p[1:], outs_s[1:]
        sample = dict(layer=l, row0=rows_p, n_seq=bs, seq_len=ts)
        cat, new_gla_t = _batched_recurrence_call(feat, outs_s[0], gla_t, new_gla_t, q_off=OFF_GQ, k_off=OFF_GK,
                                                  v_off=OFF_GV, la_off=OFF_LR, out_off=C_MIX, n_heads=GLA_H,
                                                  dk=GLA_DK, **sample)
        cat, new_hg_t = _batched_recurrence_call(feat, cat, hg_t, new_hg_t, q_off=OFF_RQ, k_off=None, v_off=OFF_RI,
                                                 la_off=OFF_RF, out_off=2 * C_MIX, n_heads=HG_H, dk=HG_DK, **sample)
        final_shapes = ((bp, tp, D_MODEL), (rows_s, D_MODEL)) if l == DEPTH - 1 else None
        xs = _post_call(x1, cat, feat, ptab, wout, *ffn2, gf, l, n_first, final_shapes)

    seq_first = lambda s, h, dk: jnp.transpose(s.reshape(DEPTH, h, dk, s.shape[-2], bs), (0, 4, 1, 2, 3))
    time_minor = lambda s: jnp.transpose(s, (0, 2, 1, 3))
    pool_p, gla_p, hg_p, ssm_p, conv_p = states_p
    pool_s, ssm_s, conv_s = states_s
    return (xs[0], xs[1].reshape(bs, ts, D_MODEL),
            pool_p, time_minor(pool_s),
            gla_p.reshape(DEPTH, bp, GLA_H, GLA_DK, GLA_DV), seq_first(new_gla_t, GLA_H, GLA_DK),
            hg_p.reshape(DEPTH, bp, HG_H, HG_DK, HG_DV), seq_first(new_hg_t, HG_H, HG_DK),
            ssm_p.reshape(DEPTH, bp, SSM_H, SSM_P, SSM_N), ssm_s.reshape(DEPTH, bs, SSM_H, SSM_P, SSM_N),
            conv_p, time_minor(conv_s))
```

```python
import functools

import jax
import jax.numpy as jnp
from jax import lax
from jax.experimental import pallas as pl
from jax.experimental.pallas import tpu as pltpu

F32 = jnp.float32
BF16 = jnp.bfloat16

D_MODEL = 1024
D_FF = 2816
DEPTH = 2
EPS = 1e-6
PAST_LEN = 16384

C_MIX = 256
POOL_WINDOWS = (2, 4, 8, 16)
POOL_GC = 64
POOL_PAST = 15
GLA_H, GLA_DK, GLA_DV, GLA_RANK, GLA_TAU = 4, 32, 64, 16, 16.0
HG_H, HG_DK, HG_DV = 4, 64, 64
SSM_H, SSM_P, SSM_G, SSM_N, SSM_CONV = 4, 64, 2, 128, 4
SSM_CONV_DIM = 768
N_IN = 3092

OFF_PX, OFF_GQ, OFF_GK, OFF_GV, OFF_GR = 0, 256, 384, 512, 768
OFF_RQ, OFF_RF, OFF_RI, OFF_RG = 1024, 1280, 1536, 1792
OFF_SZ, OFF_XBC = 2048, 2304
OFF_LR = 3072
OFF_DTX = 3200
W_PROJ = 3456

PT_POOL_SCALE, PT_GBIAS, PT_GNORM, PT_HNORM, PT_CONVB, PT_DTB, PT_ALOG, PT_DSKIP, PT_SNORM = range(9)
PT_CONVW = 9
PT_LBL = 13
PT_ROWS = 16

VMEM_LIMIT_BYTES = 56 * 1024 * 1024
SUBLANES = 8
LANES = 128
HEADS_PER_STEP = LANES // GLA_DV

ROW_TILE = 512
FF_CHUNKS = (0, 768, 1536, 2304, D_FF)
PROMPT_CHUNK = 64
PROMPT_GROUP = 8
SAMPLE_GROUP = 16
SUB_BLOCK = 16
MAX_FACTORED_SPAN = 60.0
POOL_BASE = 24


def _sigmoid(x):
    return 1.0 / (1.0 + jnp.exp(-x))


def _silu(x):
    return x * _sigmoid(x)


def _log_sigmoid(x):
    return jnp.minimum(x, 0.0) - jnp.log(1.0 + jnp.exp(-jnp.abs(x)))


def _softplus(x):
    return jnp.maximum(x, 0.0) + jnp.log(1.0 + jnp.exp(-jnp.abs(x)))


def _rms(x, g):
    ms = jnp.mean(x * x, axis=-1, keepdims=True)
    return x * lax.rsqrt(ms + EPS) * g


def _dot(a, b):
    return jnp.dot(a.astype(BF16), b.astype(BF16), preferred_element_type=F32)


def _dot_nt(a, b):
    return lax.dot_general(a.astype(BF16), b.astype(BF16), (((1,), (1,)), ((), ())),
                           preferred_element_type=F32)


def _dot_tn(a, b):
    return lax.dot_general(a.astype(BF16), b.astype(BF16), (((0,), (0,)), ((), ())),
                           preferred_element_type=F32)


def _split3(x):
    hi = x.astype(BF16)
    r1 = x - hi.astype(F32)
    mid = r1.astype(BF16)
    lo = (r1 - mid.astype(F32)).astype(BF16)
    return hi, mid, lo


def _dot_exact_rhs(a01, x):
    hi, mid, lo = _split3(x)
    a = a01.astype(BF16)
    return (jnp.dot(a, hi, preferred_element_type=F32) + jnp.dot(a, mid, preferred_element_type=F32)
            + jnp.dot(a, lo, preferred_element_type=F32))


def _swiglu_half(x, g, wg_ref, wu_ref, wd_ref):
    h = _rms(x, g).astype(BF16)
    acc = None
    for lo, hi in zip(FF_CHUNKS[:-1], FF_CHUNKS[1:]):
        sl = slice(lo, hi)
        gate = jnp.dot(h, wg_ref[:, sl], preferred_element_type=F32)
        up = jnp.dot(h, wu_ref[:, sl], preferred_element_type=F32)
        act = (_silu(gate) * up).astype(BF16)
        d = jnp.dot(act, wd_ref[sl, :], preferred_element_type=F32)
        acc = d if acc is None else acc + d
    return x + 0.5 * acc


def _pre_kernel(*refs, n_first, split_in, layer):
    if split_in:
        x = jnp.where(pl.program_id(0) < n_first, refs[0][...].reshape(ROW_TILE, D_MODEL), refs[1][...])
        refs = refs[2:]
    else:
        x = refs[0][...]
        refs = refs[1:]
    g1_ref, wg_ref, wu_ref, wd_ref, gm_ref, win_ref, wgate_ref, ptab_ref, x1_ref, feat_ref = refs
    x1 = _swiglu_half(x, g1_ref[...], wg_ref, wu_ref, wd_ref)
    x1_ref[...] = x1
    h = _rms(x1, gm_ref[...]).astype(BF16)
    _mixer_features(h, win_ref, wgate_ref, ptab_ref, feat_ref, layer)


def _hgrn_lower_bound(ptab_ref, layer):
    lbl = ptab_ref[PT_LBL:PT_LBL + DEPTH, 0:C_MIX]
    lexp = jnp.exp(lbl - jnp.max(lbl, axis=0, keepdims=True))
    lsum = jnp.sum(lexp, axis=0, keepdims=True)
    lb = jnp.zeros((1, C_MIX), F32)
    for m in range(1, layer + 1):
        lb = lb + lexp[m:m + 1] / lsum
    return lb


def _mixer_features(h, win_ref, wgate_ref, ptab_ref, feat_ref, layer):
    prow = lambda r, w: ptab_ref[r:r + 1, 0:w]
    proj = lambda lo, hi: jnp.dot(h, win_ref[:, lo:hi], preferred_element_type=F32)
    d = proj(OFF_LR, W_PROJ)
    logit = _dot(d[:, 0:LANES], wgate_ref[...]) + prow(PT_GBIAS, LANES)
    feat_ref[:, OFF_LR:OFF_DTX] = _log_sigmoid(logit) * (1.0 / GLA_TAU)
    feat_ref[:, OFF_DTX:W_PROJ] = _softplus(d[:, LANES:LANES + C_MIX] + prow(PT_DTB, C_MIX))
    b = proj(OFF_RQ, OFF_XBC)
    feat_ref[:, OFF_RQ:OFF_RF] = _silu(b[:, 0:C_MIX])
    lb = _hgrn_lower_bound(ptab_ref, layer)
    log_lb = jnp.log(lb)
    t2 = jnp.log(1.0 - lb) + _log_sigmoid(b[:, C_MIX:2 * C_MIX])
    feat_ref[:, OFF_RF:OFF_RI] = jnp.maximum(log_lb, t2) + jnp.log(1.0 + jnp.exp(-jnp.abs(log_lb - t2)))
    feat_ref[:, OFF_RI:OFF_RG] = b[:, 2 * C_MIX:3 * C_MIX]
    feat_ref[:, OFF_RG:OFF_XBC] = _silu(b[:, 3 * C_MIX:5 * C_MIX])
    a = proj(OFF_PX, OFF_RQ)
    feat_ref[:, OFF_GR:OFF_RQ] = _silu(a[:, OFF_GR:OFF_RQ])
    feat_ref[:, OFF_GQ:OFF_GK] = a[:, OFF_GQ:OFF_GK] * (GLA_DK ** -0.5)
    feat_ref[:, OFF_PX:OFF_GQ] = a[:, OFF_PX:OFF_GQ]
    feat_ref[:, OFF_GK:OFF_GR] = a[:, OFF_GK:OFF_GR]
    feat_ref[:, OFF_XBC:OFF_LR] = proj(OFF_XBC, OFF_LR)


def _segment_rms(o, width, gain):
    n = o.shape[-1]
    seg = (lax.broadcasted_iota(jnp.int32, (n, n), 0) // width
           == lax.broadcasted_iota(jnp.int32, (n, n), 1) // width).astype(BF16)
    sq = o * o
    hi = sq.astype(BF16)
    lo = (sq - hi.astype(F32)).astype(BF16)
    ms = (jnp.dot(hi, seg, preferred_element_type=F32) + jnp.dot(lo, seg, preferred_element_type=F32)) * (1.0 / width)
    return o * lax.rsqrt(ms + EPS) * gain


def _mixer_outputs(cat_ref, ggla_ref, ghg_ref, gz_ref, ptab_ref):
    prow = lambda r, w: ptab_ref[r:r + 1, 0:w]
    o_gla = _segment_rms(cat_ref[:, C_MIX:2 * C_MIX], GLA_DV, prow(PT_GNORM, C_MIX)) * ggla_ref[...]
    o_hg = _segment_rms(cat_ref[:, 2 * C_MIX:3 * C_MIX], HG_DV, prow(PT_HNORM, C_MIX)) * ghg_ref[...]
    y = cat_ref[:, 3 * C_MIX:4 * C_MIX] * gz_ref[...]
    half = C_MIX // SSM_G
    o_ssm = [_rms(y[:, half * g:half * (g + 1)], ptab_ref[PT_SNORM:PT_SNORM + 1, half * g:half * (g + 1)])
             for g in range(SSM_G)]
    return jnp.concatenate([cat_ref[:, 0:C_MIX], o_gla, o_hg] + o_ssm, axis=1)


def _post_kernel(x_ref, cat_ref, ggla_ref, ghg_ref, gz_ref, ptab_ref, wout_ref, g2_ref, wg_ref, wu_ref, wd_ref,
                 gf_ref, *o_refs, n_first, final):
    mixed = _mixer_outputs(cat_ref, ggla_ref, ghg_ref, gz_ref, ptab_ref)
    x2 = x_ref[...] + jnp.dot(mixed.astype(BF16), wout_ref[...], preferred_element_type=F32)
    x3 = _swiglu_half(x2, g2_ref[...], wg_ref, wu_ref, wd_ref)
    if final:
        y = _rms(x3, gf_ref[...])
        i = pl.program_id(0)

        @pl.when(i < n_first)
        def _():
            o_refs[0][...] = y.reshape(o_refs[0].shape)

        @pl.when(i >= n_first)
        def _():
            o_refs[1][...] = y
    else:
        o_refs[0][...] = x3


def _layer_spec(shape, layer):
    nd = len(shape)
    return pl.BlockSpec((None,) + tuple(shape), lambda *_: (layer,) + (0,) * nd, pipeline_mode=pl.Buffered(1))


def _dense_params():
    return pltpu.CompilerParams(dimension_semantics=("arbitrary",), vmem_limit_bytes=VMEM_LIMIT_BYTES)


def _row_spec(width):
    return pl.BlockSpec((ROW_TILE, width), lambda i: (i, 0))


def _token_specs(n_first, n_prompt_seq):
    return [pl.BlockSpec((n_prompt_seq, PROMPT_CHUNK, D_MODEL), lambda i: (0, jnp.minimum(i, n_first - 1), 0)),
            pl.BlockSpec((ROW_TILE, D_MODEL), lambda i: (jnp.maximum(i - n_first, 0), 0))]


def _pre_call(xs, g1, wg, wu, wd, gm, win, wgate, ptab, layer, rows, n_first):
    split_in = len(xs) == 2
    x_specs = _token_specs(n_first, xs[0].shape[0]) if split_in else [_row_spec(D_MODEL)]
    return pl.pallas_call(
        functools.partial(_pre_kernel, n_first=n_first, split_in=split_in, layer=layer),
        grid=(rows // ROW_TILE,),
        in_specs=x_specs + [_layer_spec((1, D_MODEL), layer), _layer_spec((D_MODEL, D_FF), layer),
                            _layer_spec((D_MODEL, D_FF), layer), _layer_spec((D_FF, D_MODEL), layer),
                            _layer_spec((1, D_MODEL), layer),
                            pl.BlockSpec((D_MODEL, W_PROJ), lambda i: (0, 0), pipeline_mode=pl.Buffered(1)),
                            _layer_spec(wgate.shape[1:], layer), _layer_spec(ptab.shape[1:], layer)],
        out_specs=[_row_spec(D_MODEL), _row_spec(W_PROJ)],
        out_shape=[jax.ShapeDtypeStruct((rows, D_MODEL), F32), jax.ShapeDtypeStruct((rows, W_PROJ), F32)],
        compiler_params=_dense_params(),
        name="pre_ffn_inproj",
    )(*xs, g1, wg, wu, wd, gm, win, wgate, ptab)


def _post_call(x, cat, feat, ptab, wout, g2, wg, wu, wd, gf, layer, n_first, final_shapes):
    rows = x.shape[0]
    if final_shapes is not None:
        out_specs = _token_specs(n_first, final_shapes[0][0])
        out_shape = [jax.ShapeDtypeStruct(s, F32) for s in final_shapes]
    else:
        out_specs = [_row_spec(D_MODEL)]
        out_shape = [jax.ShapeDtypeStruct((rows, D_MODEL), F32)]
    gate_spec = lambda off: pl.BlockSpec((ROW_TILE, C_MIX), lambda i: (i, off // C_MIX))
    return pl.pallas_call(
        functools.partial(_post_kernel, n_first=n_first, final=final_shapes is not None),
        grid=(rows // ROW_TILE,),
        in_specs=[_row_spec(D_MODEL), _row_spec(D_MODEL), gate_spec(OFF_GR), gate_spec(OFF_RG), gate_spec(OFF_SZ),
                  _layer_spec(ptab.shape[1:], layer), _layer_spec((D_MODEL, D_MODEL), layer),
                  _layer_spec((1, D_MODEL), layer), _layer_spec((D_MODEL, D_FF), layer),
                  _layer_spec((D_MODEL, D_FF), layer), _layer_spec((D_FF, D_MODEL), layer),
                  pl.BlockSpec((1, D_MODEL), lambda i: (0, 0))],
        out_specs=out_specs,
        out_shape=out_shape,
        compiler_params=_dense_params(),
        name="post_outproj_ffn",
    )(x, cat, feat, feat, feat, ptab, wout, g2, wg, wu, wd, gf)


def _lane_group(shape, width):
    return lax.broadcasted_iota(jnp.int32, shape, len(shape) - 1) // width


def _block_diag_mask(rows, row_w, cols, col_w):
    r = lax.broadcasted_iota(jnp.int32, (rows, cols), 0) // row_w
    c = lax.broadcasted_iota(jnp.int32, (rows, cols), 1) // col_w
    return r == c


def _tril(n):
    r = lax.broadcasted_iota(jnp.int32, (n, n), 0)
    c = lax.broadcasted_iota(jnp.int32, (n, n), 1)
    return r >= c


def _select_heads(stacked, n_heads, rows, head_w):
    grp = _lane_group((rows, n_heads * head_w), head_w)
    out = jnp.zeros((rows, n_heads * head_w), F32)
    for h in range(n_heads):
        out = jnp.where(grp == h, stacked[h * rows:(h + 1) * rows], out)
    return out


def _cum_log_decay(la):
    return _dot_exact_rhs(_tril(la.shape[0]).astype(F32), la)


def _decay_span(b):
    mid = b.shape[0] // 2 - 1
    dev = jnp.abs(b - b[mid:mid + 1, :])
    dev = functools.reduce(jnp.maximum, [dev[:, LANES * j:LANES * (j + 1)] for j in range(dev.shape[1] // LANES)])
    return jnp.max(dev.reshape(-1, SUBLANES, LANES), axis=0)


def _la_factored(q, k, v, b, st, n_heads, dk):
    chunk = q.shape[0]
    nk, nv = n_heads * dk, n_heads * 64
    mid = chunk // 2 - 1
    m = b[mid:mid + 1, :]
    qt = q * jnp.exp(b - m)
    kt = k * jnp.exp(m - b)
    k_bd = jnp.where(_block_diag_mask(n_heads * chunk, chunk, nk, dk), jnp.concatenate([kt] * n_heads, axis=0), 0.0)
    att = _dot_nt(qt, k_bd)
    row = lax.broadcasted_iota(jnp.int32, att.shape, 0)
    col = lax.broadcasted_iota(jnp.int32, att.shape, 1) % chunk
    att = jnp.where(row >= col, att, 0.0)
    v_bd = jnp.where(_block_diag_mask(n_heads * chunk, chunk, nv, 64), jnp.concatenate([v] * n_heads, axis=0), 0.0)
    o = _dot(att, v_bd) + _dot_nt(qt * jnp.exp(m), st)
    b_last = b[chunk - 1:chunk, :]
    upd = _dot_tn(v, kt * jnp.exp(b_last - m))
    st_new = st * jnp.exp(b_last) + jnp.where(_block_diag_mask(nv, 64, nk, dk), upd, 0.0)
    return o, st_new


def _la_direct(q, k, v, b, st, n_heads, dk, sub):
    chunk = q.shape[0]
    nk = n_heads * dk
    nv = n_heads * 64
    n_sub = chunk // sub
    refs = [None] + [b[sub * i - 1:sub * i, :] for i in range(1, n_sub)]
    c_loc = jnp.concatenate([b[0:sub]] + [b[sub * i:sub * (i + 1)] - refs[i] for i in range(1, n_sub)], axis=0) \
        if n_sub > 1 else b
    o = _dot_nt(q * jnp.exp(b), st)

    qt = q * jnp.exp(c_loc)
    kgrp = _lane_group((sub, nk), dk)
    outs = []
    for i in range(n_sub):
        rows = slice(sub * i, sub * (i + 1))
        o_i = o[rows]
        if i > 0:
            prev = slice(0, sub * i)
            kt = k[prev] * jnp.exp(refs[i] - b[prev])
            q_stack = jnp.concatenate([jnp.where(kgrp == h, qt[rows], 0.0) for h in range(n_heads)], axis=0)
            att = _dot_nt(q_stack, kt)
            o_i = o_i + _select_heads(_dot(att, v[prev]), n_heads, sub, 64)
        c_i = c_loc[rows]
        q_i = q[rows]
        row_id = lax.broadcasted_iota(jnp.int32, (sub, nk), 0)
        e_list = []
        for j in range(sub):
            c_j = c_loc[sub * i + j:sub * i + j + 1, :]
            k_j = k[sub * i + j:sub * i + j + 1, :]
            dec = jnp.exp(jnp.where(row_id >= j, c_i - c_j, -jnp.inf))
            e_list.append(dec * q_i * k_j)
        e_all = jnp.concatenate(e_list, axis=0)
        expand = _block_diag_mask(nk, dk, nv, 64).astype(F32)
        r_d = _dot(e_all, expand)
        for j in range(sub):
            o_i = o_i + r_d[j * sub:(j + 1) * sub] * v[sub * i + j:sub * i + j + 1, :]
        outs.append(o_i)
    o = jnp.concatenate(outs, axis=0) if n_sub > 1 else outs[0]

    b_last = b[chunk - 1:chunk, :]
    k_end = k * jnp.exp(b_last - b)
    upd = _dot_tn(v, k_end)
    st_new = st * jnp.exp(b_last) + jnp.where(_block_diag_mask(nv, 64, nk, dk), upd, 0.0)
    return o, st_new


def _load_state_t(s, n_heads, dk):
    st = s.T
    tiled = jnp.concatenate([st] * n_heads, axis=0)
    return jnp.where(_block_diag_mask(n_heads * 64, 64, n_heads * dk, dk), tiled, 0.0)


def _store_state_t(st, n_heads, dk):
    acc = st[0:64]
    for h in range(1, n_heads):
        acc = acc + st[64 * h:64 * (h + 1)]
    return acc.T


def _pool_mixer(proj_ref, pool_scr, poolw_ref, scale, chunk, pos_first):
    xp = proj_ref[:, OFF_PX:OFF_PX + C_MIX]
    pool_scr[0, POOL_BASE:POOL_BASE + chunk, :] = xp
    n_ext = POOL_BASE - SUBLANES + chunk
    for s, shift in enumerate((1, 2, 4)):
        cur = pool_scr[s, SUBLANES:SUBLANES + n_ext, :]
        prev = pool_scr[s, SUBLANES - shift:SUBLANES - shift + n_ext, :]
        pool_scr[s + 1, SUBLANES:SUBLANES + n_ext, :] = cur + prev
    s2 = pool_scr[1, POOL_BASE:POOL_BASE + chunk, :]
    s4 = pool_scr[2, POOL_BASE:POOL_BASE + chunk, :]
    s8 = pool_scr[3, POOL_BASE:POOL_BASE + chunk, :]
    s16 = s8 + pool_scr[3, POOL_BASE - 8:POOL_BASE - 8 + chunk, :]
    grp = _lane_group((chunk, C_MIX), POOL_GC)
    win_sum = jnp.where(grp == 0, s2, jnp.where(grp == 1, s4, jnp.where(grp == 2, s8, s16)))
    win = jnp.where(grp == 0, 2, jnp.where(grp == 1, 4, jnp.where(grp == 2, 8, 16)))
    pos = pos_first + lax.broadcasted_iota(jnp.int32, (chunk, C_MIX), 0)
    cnt = jnp.minimum(pos + 1, win).astype(F32)
    d = win_sum / cnt - xp
    o_pool = _dot(d, poolw_ref[...]) * scale
    new_pool = pool_scr[0, POOL_BASE + chunk - POOL_PAST:POOL_BASE + chunk, :]
    pool_scr[0, POOL_BASE - POOL_PAST:POOL_BASE, :] = new_pool
    return o_pool, new_pool


def _ssd_conv(proj_ref, conv_scr, prow, chunk):
    conv_base = SUBLANES
    conv_scr[conv_base:conv_base + chunk, :] = proj_ref[:, OFF_XBC:OFF_XBC + SSM_CONV_DIM]
    conv = None
    for w in range(SSM_CONV):
        start = conv_base - (SSM_CONV - 1) + w
        term = conv_scr[start:start + chunk, :] * prow(PT_CONVW + w, SSM_CONV_DIM)
        conv = term if conv is None else conv + term
    conv = _silu(conv + prow(PT_CONVB, SSM_CONV_DIM))
    new_conv = conv_scr[conv_base + chunk - (SSM_CONV - 1):conv_base + chunk, :]
    conv_scr[conv_base - (SSM_CONV - 1):conv_base, :] = new_conv
    return conv, new_conv


def _ssd_scan(proj_ref, conv, ssm_st, prow, chunk):
    xs = conv[:, 0:C_MIX]
    dt = proj_ref[:, OFF_DTX:OFF_DTX + C_MIX]
    a_neg = -jnp.exp(prow(PT_ALOG, C_MIX))
    la = dt * a_neg
    xdt = xs * dt
    cum = _cum_log_decay(la)
    heads_per_group = SSM_H // SSM_G
    b_of = lambda g: conv[:, C_MIX + SSM_N * g:C_MIX + SSM_N * (g + 1)]
    c_of = lambda g: conv[:, C_MIX + SSM_G * SSM_N + SSM_N * g:C_MIX + SSM_G * SSM_N + SSM_N * (g + 1)]
    if chunk == SSM_P:
        row = lax.broadcasted_iota(jnp.int32, (chunk, C_MIX), 0)
        col = lax.broadcasted_iota(jnp.int32, (chunk, C_MIX), 1) % chunk
        cum_j = jnp.sum(jnp.where(row == col, cum, 0.0), axis=0, keepdims=True)
        dec = jnp.exp(jnp.where(row >= col, cum - cum_j, -jnp.inf))
        cb = jnp.concatenate([_dot_nt(c_of(g), jnp.concatenate([b_of(g)] * heads_per_group, axis=0))
                              for g in range(SSM_G)], axis=1)
        x_bd = jnp.where(_block_diag_mask(SSM_H * chunk, chunk, C_MIX, SSM_P),
                         jnp.concatenate([xdt] * SSM_H, axis=0), 0.0)
        y = _dot(cb * dec, x_bd)
    else:
        cum_t = cum.T
        causal = _tril(chunk)
        att_rows = []
        for h in range(SSM_H):
            if h % heads_per_group == 0:
                cb = _dot_nt(c_of(h // heads_per_group), b_of(h // heads_per_group))
            col = cum[:, SSM_P * h:SSM_P * h + 1]
            row = cum_t[SSM_P * h:SSM_P * h + 1, :]
            att_rows.append(cb * jnp.exp(jnp.where(causal, col - row, -jnp.inf)))
        y = _select_heads(_dot(jnp.concatenate(att_rows, axis=0), xdt), SSM_H, chunk, SSM_P)
    cum_last = cum[chunk - 1:chunk, :]
    decay_in = jnp.exp(cum)
    xw = xdt * jnp.exp(cum_last - cum)
    decay_state = jnp.exp(cum_last)
    y_state = []
    for g in range(SSM_G):
        lanes = slice(SSM_N * g, SSM_N * (g + 1))
        st = ssm_st[g]
        y_state.append(_dot(c_of(g), st))
        ssm_st[g] = st * decay_state[:, lanes] + _dot_tn(b_of(g), xw[:, lanes])
    return y + jnp.concatenate(y_state, axis=1) * decay_in + prow(PT_DSKIP, C_MIX) * xs


def _mixer_kernel(*refs, chunk, group, pos0, layer, carried, n_alias, lin):
    feat_all = refs[0]
    n_state = 5 if lin else 3
    n_in = 1 + (n_state if carried else 0)
    poolw_ref, ptab_ref = refs[n_in:n_in + 2]
    n_in += 2 + n_alias
    cat_all = refs[n_in]
    new_states = refs[n_in + 1:n_in + 1 + n_state]
    scratch = refs[n_in + 1 + n_state:]
    if lin:
        npool_ref, ngla_ref, nhg_ref, nssm_ref, nconv_ref = new_states
        pool_scr, conv_scr, gla_st, hg_st, ssm_st = scratch
    else:
        npool_ref, nssm_ref, nconv_ref = new_states
        pool_scr, conv_scr, ssm_st = scratch
    seq_state = (lambda ref, s: ref.at[s]) if lin else (lambda ref, s: ref.at[:, s])
    prow = lambda r, w: ptab_ref[r:r + 1, 0:w]
    c = pl.program_id(1)
    n_c = pl.num_programs(1)
    sub = min(SUB_BLOCK, chunk)
    conv_base = SUBLANES

    @pl.when(c == 0)
    def _():
        pool_scr[...] = jnp.zeros(pool_scr.shape, F32)
        conv_scr[...] = jnp.zeros(conv_scr.shape, F32)
        if carried:
            carried_refs = refs[1:1 + n_state]
            poolb_ref, ssms_ref, convb_ref = carried_refs[0], carried_refs[-2], carried_refs[-1]
            for s in range(group):
                pool_scr[s, 0, POOL_BASE - POOL_PAST:POOL_BASE, :] = seq_state(poolb_ref, s)[...]
                conv_scr[s, conv_base - (SSM_CONV - 1):conv_base, :] = seq_state(convb_ref, s)[...]
                for g in range(SSM_G):
                    ssm_st[s, g] = ssms_ref[s, g].T
                if lin:
                    gla_st[s] = _load_state_t(carried_refs[1][s], GLA_H, GLA_DK)
                    hg_st[s] = _load_state_t(carried_refs[2][s], HG_H, HG_DK)
        else:
            ssm_st[...] = jnp.zeros(ssm_st.shape, F32)
            if lin:
                gla_st[...] = jnp.zeros(gla_st.shape, F32)
                hg_st[...] = jnp.zeros(hg_st.shape, F32)

    seq_rows = lambda ref, s: ref.at[pl.ds(s * chunk, chunk)]
    new_pools, new_convs, convs, cum_decays, spans = [], [], [], [], []

    def conv_stage(s):
        conv, new_conv = _ssd_conv(seq_rows(feat_all, s), conv_scr.at[s], prow, chunk)
        convs.append(conv)
        new_convs.append(new_conv)

    def decay_stage(s):
        feat_ref = seq_rows(feat_all, s)
        b_gla = _cum_log_decay(feat_ref[:, OFF_LR:OFF_LR + GLA_H * GLA_DK])
        b_hg = _cum_log_decay(feat_ref[:, OFF_RF:OFF_RF + C_MIX])
        cum_decays.extend([b_gla, b_hg])
        spans.extend([_decay_span(b_gla), _decay_span(b_hg)])

    def scan_stage(s):
        seq_rows(cat_all, s)[:, 3 * C_MIX:4 * C_MIX] = _ssd_scan(seq_rows(feat_all, s), convs[s], ssm_st.at[s], prow,
                                                                chunk)

    def pool_stage(s):
        o_pool, new_pool = _pool_mixer(seq_rows(feat_all, s), pool_scr.at[s], poolw_ref, prow(PT_POOL_SCALE, C_MIX),
                                       chunk, pos0 + c * chunk)
        seq_rows(cat_all, s)[:, 0:C_MIX] = o_pool
        new_pools.append(new_pool)

    stages = (conv_stage, decay_stage, scan_stage, pool_stage) if lin else (conv_stage, scan_stage, pool_stage)
    if chunk >= PROMPT_CHUNK:
        for stage in stages:
            for s in range(group):
                stage(s)
    else:
        for s in range(group):
            for stage in stages:
                stage(s)

    if lin:
        _linear_attention_step(feat_all, cat_all, gla_st, hg_st, cum_decays, spans, seq_rows, group, sub)
    else:
        for s in range(group):
            seq_rows(cat_all, s)[:, C_MIX:3 * C_MIX] = jnp.zeros((chunk, 2 * C_MIX), F32)

    @pl.when(c == n_c - 1)
    def _():
        for s in range(group):
            seq_state(npool_ref, s)[...] = new_pools[s]
            seq_state(nconv_ref, s)[...] = new_convs[s]
            for g in range(SSM_G):
                nssm_ref[s, g] = ssm_st[s, g].T
            if lin:
                ngla_ref[s] = _store_state_t(gla_st[s], GLA_H, GLA_DK)
                nhg_ref[s] = _store_state_t(hg_st[s], HG_H, HG_DK)


def _linear_attention_step(feat_all, cat_all, gla_st, hg_st, cum_decays, spans, seq_rows, group, sub):
    narrow = jnp.max(functools.reduce(jnp.maximum, spans)) <= MAX_FACTORED_SPAN

    def run(la_gla, la_hg, *cums):
        for s in range(group):
            feat_ref, cat_ref = seq_rows(feat_all, s), seq_rows(cat_all, s)
            o_g, st_g = la_gla(feat_ref[:, OFF_GQ:OFF_GK], feat_ref[:, OFF_GK:OFF_GV], feat_ref[:, OFF_GV:OFF_GR],
                               cums[2 * s], gla_st[s])
            gla_st[s] = st_g
            cat_ref[:, C_MIX:2 * C_MIX] = o_g
            hk = 1.0 - jnp.exp(feat_ref[:, OFF_RF:OFF_RI])
            o_h, st_h = la_hg(feat_ref[:, OFF_RQ:OFF_RF], hk, feat_ref[:, OFF_RI:OFF_RG], cums[2 * s + 1], hg_st[s])
            hg_st[s] = st_h
            cat_ref[:, 2 * C_MIX:3 * C_MIX] = o_h

    factored = functools.partial(run, functools.partial(_la_factored, n_heads=GLA_H, dk=GLA_DK),
                                 functools.partial(_la_factored, n_heads=HG_H, dk=HG_DK))
    direct = functools.partial(run, functools.partial(_la_direct, n_heads=GLA_H, dk=GLA_DK, sub=sub),
                               functools.partial(_la_direct, n_heads=HG_H, dk=HG_DK, sub=sub))
    lax.cond(narrow, factored, direct, *cum_decays)


def _mixer_call(proj, cat_prev, carried, prev_states, mats, *, n_seq, seq_len, row0, chunk_major, chunk, group,
                pos0, layer, lin):
    rows = proj.shape[0]
    n_c, n_b, blk = seq_len // chunk, n_seq // group, group * chunk
    assert chunk_major or n_c == 1
    row_map = lambda b, c: (row0 // blk + c * n_b + b, 0)

    def seq_major(tail):
        nd = len(tail)
        return (DEPTH, n_seq) + tail, pl.BlockSpec((None, group) + tail, lambda b, c: (layer, b) + (0,) * nd)

    def time_major(n_rows, width):
        return (DEPTH, n_rows, n_seq, width), pl.BlockSpec((None, n_rows, group, width), lambda b, c: (layer, 0, b, 0))

    ssm = seq_major((SSM_G, 128, SSM_N))
    if lin:
        states = [seq_major((POOL_PAST, C_MIX)), seq_major((GLA_H * GLA_DK, GLA_DV)), seq_major((HG_H * HG_DK, HG_DV)),
                  ssm, seq_major((SSM_CONV - 1, SSM_CONV_DIM))]
    else:
        states = [time_major(POOL_PAST, C_MIX), ssm, time_major(SSM_CONV - 1, SSM_CONV_DIM)]

    def mat_spec(m):
        nd = m.ndim - 1
        return pl.BlockSpec((None,) + m.shape[1:], lambda b, c: (layer,) + (0,) * nd)

    state_specs = [spec for _, spec in states]
    in_specs = [pl.BlockSpec((blk, W_PROJ), row_map)]
    args = [proj]
    if carried is not None:
        in_specs += state_specs
        args += list(carried)
    in_specs += [mat_spec(m) for m in mats]
    args += list(mats)
    aliases = {}
    if cat_prev is not None:
        aliases[len(args)] = 0
        args.append(cat_prev)
    if prev_states is not None:
        for i, s in enumerate(prev_states):
            aliases[len(args)] = 1 + i
            args.append(s)
    in_specs += [pl.BlockSpec(memory_space=pl.ANY)] * len(aliases)
    out_shape = [jax.ShapeDtypeStruct((rows, D_MODEL), F32)] + [jax.ShapeDtypeStruct(shape, F32) for shape, _ in states]
    out_specs = [pl.BlockSpec((blk, D_MODEL), row_map)] + state_specs
    scratch = [pltpu.VMEM((group, 4, POOL_BASE + chunk, C_MIX), F32),
               pltpu.VMEM((group, SUBLANES + chunk, SSM_CONV_DIM), F32)]
    if lin:
        scratch += [pltpu.VMEM((group, GLA_H * GLA_DV, GLA_H * GLA_DK), F32),
                    pltpu.VMEM((group, HG_H * HG_DV, HG_H * HG_DK), F32)]
    scratch.append(pltpu.VMEM((group, SSM_G, SSM_N, 128), F32))
    return pl.pallas_call(
        functools.partial(_mixer_kernel, chunk=chunk, group=group, pos0=pos0, layer=layer,
                          carried=carried is not None, n_alias=len(aliases), lin=lin),
        grid=(n_b, n_c),
        in_specs=in_specs,
        out_specs=out_specs,
        out_shape=out_shape,
        scratch_shapes=scratch,
        input_output_aliases=aliases,
        compiler_params=pltpu.CompilerParams(dimension_semantics=("arbitrary", "arbitrary"),
                                             vmem_limit_bytes=VMEM_LIMIT_BYTES),
        name=f"mixer_chunk{chunk}",
    )(*args)


def _batched_recurrence_kernel(*refs, n_seq, seq_len, dk, own_keys, n_alias):
    n_in = 5 if own_keys else 4
    q_ref, v_ref, la_ref, s_in = refs[0], refs[n_in - 3], refs[n_in - 2], refs[n_in - 1]
    o_ref, s_out, q_t, k_t, a_t, v_t = refs[n_in + n_alias:]
    key_w = HEADS_PER_STEP * dk
    base = 0 if key_w == q_ref.shape[1] else pl.multiple_of(pl.program_id(0) * key_w, key_w)
    s_out[...] = s_in[...]
    for t in range(seq_len):
        rows = pl.ds(t, n_seq, stride=seq_len)
        a = jnp.exp(la_ref[rows, :])
        k = refs[1][rows, :] if own_keys else 1.0 - a
        q_t[...] = q_ref[rows, :].T
        k_t[...] = k.T
        a_t[...] = a.T
        v_t[...] = v_ref[rows, :].T
        heads = []
        for h in range(HEADS_PER_STEP):
            v_h = v_t[GLA_DV * h:GLA_DV * (h + 1), :]

            def body(i, acc, h=h, v_h=v_h):
                r = h * dk + i
                s = a_t[pl.ds(base + r, 1), :] * s_out[r] + k_t[pl.ds(base + r, 1), :] * v_h
                s_out[r] = s
                return acc + q_t[pl.ds(base + r, 1), :] * s

            heads.append(lax.fori_loop(0, dk, body, jnp.zeros((GLA_DV, n_seq), F32), unroll=4))
        o_ref[rows, :] = jnp.concatenate(heads, axis=0).T


def _batched_recurrence_call(feat, cat, state_t, prev_state, *, layer, row0, n_seq, seq_len, q_off, k_off, v_off,
                             la_off, out_off, n_heads, dk):
    n_rows = n_seq * seq_len
    rb = row0 // n_rows
    key_w = HEADS_PER_STEP * dk
    assert key_w in (LANES // 2, LANES) and n_heads % HEADS_PER_STEP == 0

    def lanes(off, per_step):
        return pl.BlockSpec((n_rows, LANES), lambda i: (rb, off // LANES + (i if per_step else 0)))

    key = lambda off: lanes(off, key_w == LANES)
    in_specs = [key(q_off)] + ([key(k_off)] if k_off is not None else []) + [lanes(v_off, True), key(la_off)]
    args = [feat] * len(in_specs)
    state_spec = pl.BlockSpec((None, key_w, GLA_DV, n_seq), lambda i: (layer, i, 0, 0))
    in_specs.append(state_spec)
    args.append(state_t)
    aliases = {len(args): 0}
    args.append(cat)
    if prev_state is not None:
        aliases[len(args)] = 1
        args.append(prev_state)
    in_specs += [pl.BlockSpec(memory_space=pl.ANY)] * len(aliases)
    return pl.pallas_call(
        functools.partial(_batched_recurrence_kernel, n_seq=n_seq, seq_len=seq_len, dk=dk,
                          own_keys=k_off is not None, n_alias=len(aliases)),
        grid=(n_heads // HEADS_PER_STEP,),
        in_specs=in_specs,
        out_specs=[lanes(out_off, True), state_spec],
        out_shape=[jax.ShapeDtypeStruct(cat.shape, F32), jax.ShapeDtypeStruct(state_t.shape, F32)],
        scratch_shapes=[pltpu.VMEM((LANES, n_seq), F32)] * 4,
        input_output_aliases=aliases,
        compiler_params=pltpu.CompilerParams(dimension_semantics=("arbitrary",), vmem_limit_bytes=VMEM_LIMIT_BYTES),
        name=f"batched_recurrence_dk{dk}",
    )(*args)


LR_SRC = 1024
DT_SRC = N_IN - SSM_H


def _arrange_w_in_kernel(wt_ref, *o_refs):
    def put(col, slab):
        for layer, o_ref in enumerate(o_refs):
            o_ref[:, col:col + LANES] = slab[:, layer, :].T.astype(BF16)

    for j in range(LR_SRC // LANES):
        put(LANES * j, wt_ref[LANES * j:LANES * (j + 1)])
    for j in range((OFF_LR - LR_SRC) // LANES):
        src = LR_SRC + GLA_RANK + LANES * j
        put(LR_SRC + LANES * j, wt_ref[src:src + LANES])
    row = lax.broadcasted_iota(jnp.int32, (LANES, DEPTH, D_MODEL), 0)
    dt_row = lambda h: jnp.broadcast_to(wt_ref[DT_SRC + h:DT_SRC + h + 1], (LANES, DEPTH, D_MODEL))
    tail = jnp.where(row < GLA_RANK, wt_ref[LR_SRC:LR_SRC + LANES], 0.0)
    for h in range(SSM_H):
        tail = jnp.where(row == GLA_RANK + h, dt_row(h), tail)
    put(OFF_LR, tail)
    for j in range(C_MIX // LANES):
        rep = jnp.zeros((LANES, DEPTH, D_MODEL), F32)
        for h in range(HEADS_PER_STEP * j, HEADS_PER_STEP * (j + 1)):
            rep = jnp.where(row // SSM_P == h - HEADS_PER_STEP * j, dt_row(h), rep)
        put(OFF_DTX + LANES * j, rep)


def _arrange_w_in(w):
    wt = jnp.transpose(w, (2, 0, 1))
    return pl.pallas_call(
        _arrange_w_in_kernel,
        grid=(1,),
        in_specs=[pl.BlockSpec(wt.shape, lambda i: (0, 0, 0), pipeline_mode=pl.Buffered(1))],
        out_specs=[pl.BlockSpec((D_MODEL, W_PROJ), lambda i: (0, 0))] * DEPTH,
        out_shape=[jax.ShapeDtypeStruct((D_MODEL, W_PROJ), BF16)] * DEPTH,
        compiler_params=pltpu.CompilerParams(dimension_semantics=("arbitrary",), vmem_limit_bytes=VMEM_LIMIT_BYTES),
        name="arrange_w_in",
    )(wt)


def _pool_block_diag(pw):
    eye = jnp.eye(len(POOL_WINDOWS), dtype=pw.dtype)
    return jnp.einsum('lgcd,gh->lgchd', pw, eye).reshape(pw.shape[0], C_MIX, C_MIX)


def _param_table(pool_scale, gla_gate_bias, gla_norm, hgrn_norm, ssm_conv_b, ssm_dt_bias, ssm_A_log, ssm_D,
                 ssm_norm, ssm_conv_w, hgrn_lb_logits):
    def rows(v):
        v = v.astype(F32).reshape(DEPTH, -1, v.shape[-1])
        return jnp.pad(v, ((0, 0), (0, 0), (0, SSM_CONV_DIM - v.shape[-1])))
    lbl = jnp.broadcast_to(hgrn_lb_logits[None], (DEPTH,) + hgrn_lb_logits.shape)
    parts = [rows(pool_scale), rows(gla_gate_bias), rows(jnp.tile(gla_norm, (1, GLA_H))),
             rows(jnp.tile(hgrn_norm, (1, HG_H))), rows(ssm_conv_b), rows(jnp.repeat(ssm_dt_bias, SSM_P, axis=-1)),
             rows(jnp.repeat(ssm_A_log, SSM_P, axis=-1)), rows(jnp.repeat(ssm_D, SSM_P, axis=-1)), rows(ssm_norm),
             rows(ssm_conv_w), rows(lbl)]
    tab = jnp.concatenate(parts, axis=1)
    return jnp.pad(tab, ((0, 0), (0, PT_ROWS - tab.shape[1]), (0, 0)))


def kernel(x_prompt, x_sample, state_pool, state_gla, state_hgrn, state_ssm, state_conv, ffn1_norm, ffn1_w_gate, ffn1_w_up, ffn1_w_down, mix_norm, w_in, pool_w, pool_scale, gla_w_gate, gla_gate_bias, gla_norm, hgrn_lb_logits, hgrn_norm, ssm_conv_w, ssm_conv_b, ssm_dt_bias, ssm_A_log, ssm_D, ssm_norm, w_out, ffn2_norm, ffn2_w_gate, ffn2_w_up, ffn2_w_down, final_norm):
    bp, tp, _ = x_prompt.shape
    bs, ts, _ = x_sample.shape
    rows_p, rows_s = bp * tp, bs * ts
    gain = lambda v: v.reshape(DEPTH, 1, D_MODEL).astype(F32)
    bf = lambda w: w.astype(BF16)

    wgate = bf(jnp.pad(gla_w_gate, ((0, 0), (0, 128 - GLA_RANK), (0, 0))))
    ptab = _param_table(pool_scale, gla_gate_bias, gla_norm, hgrn_norm, ssm_conv_b, ssm_dt_bias, ssm_A_log, ssm_D,
                        ssm_norm, ssm_conv_w, hgrn_lb_logits)
    mats = [bf(_pool_block_diag(pool_w)), ptab]
    seq_last = lambda s: jnp.transpose(s, (0, 2, 3, 4, 1)).reshape(DEPTH, -1, s.shape[-1], bs)
    carried = (jnp.transpose(state_pool, (0, 2, 1, 3)), state_ssm.reshape(DEPTH, bs, SSM_G, 128, SSM_N),
               jnp.transpose(state_conv, (0, 2, 1, 3)))
    gla_t, hg_t = seq_last(state_gla), seq_last(state_hgrn)
    ffn1 = (gain(ffn1_norm), bf(ffn1_w_gate), bf(ffn1_w_up), bf(ffn1_w_down))
    ffn2 = (gain(ffn2_norm), bf(ffn2_w_gate), bf(ffn2_w_up), bf(ffn2_w_down))
    gm, win, wout = gain(mix_norm), _arrange_w_in(w_in), bf(w_out)
    gf = final_norm.reshape(1, D_MODEL).astype(F32)

    assert bp * PROMPT_CHUNK == ROW_TILE and rows_s % ROW_TILE == 0
    n_first = rows_p // ROW_TILE
    xs = [x_prompt, x_sample.reshape(rows_s, D_MODEL)]
    states_p = states_s = new_gla_t = new_hg_t = None
    for l in range(DEPTH):
        x1, feat = _pre_call(xs, *ffn1, gm, win[l], wgate, ptab, l, rows_p + rows_s, n_first)
        outs_p = _mixer_call(feat, None, None, states_p, mats, n_seq=bp, seq_len=tp, row0=0, chunk_major=True,
                             chunk=PROMPT_CHUNK, group=PROMPT_GROUP, pos0=0, layer=l, lin=True)
        outs_s = _mixer_call(feat, outs_p[0], carried, states_s, mats, n_seq=bs, seq_len=ts, row0=rows_p,
                             chunk_major=False, chunk=ts, group=SAMPLE_GROUP, pos0=PAST_LEN, layer=l, lin=False)
        states_p, states_s = outs_p[1:], outs_s[1:]
        sample = dict(layer=l, row0=rows_p, n_seq=bs, seq_len=ts)
        cat, new_gla_t = _batched_recurrence_call(feat, outs_s[0], gla_t, new_gla_t, q_off=OFF_GQ, k_off=OFF_GK,
                                                  v_off=OFF_GV, la_off=OFF_LR, out_off=C_MIX, n_heads=GLA_H,
                                                  dk=GLA_DK, **sample)
        cat, new_hg_t = _batched_recurrence_call(feat, cat, hg_t, new_hg_t, q_off=OFF_RQ, k_off=None, v_off=OFF_RI,
                                                 la_off=OFF_RF, out_off=2 * C_MIX, n_heads=HG_H, dk=HG_DK, **sample)
        final_shapes = ((bp, tp, D_MODEL), (rows_s, D_MODEL)) if l == DEPTH - 1 else None
        xs = _post_call(x1, cat, feat, ptab, wout, *ffn2, gf, l, n_first, final_shapes)

    seq_first = lambda s, h, dk: jnp.transpose(s.reshape(DEPTH, h, dk, s.shape[-2], bs), (0, 4, 1, 2, 3))
    time_minor = lambda s: jnp.transpose(s, (0, 2, 1, 3))
    pool_p, gla_p, hg_p, ssm_p, conv_p = states_p
    pool_s, ssm_s, conv_s = states_s
    return (xs[0], xs[1].reshape(bs, ts, D_MODEL),
            pool_p, time_minor(pool_s),
            gla_p.reshape(DEPTH, bp, GLA_H, GLA_DK, GLA_DV), seq_first(new_gla_t, GLA_H, GLA_DK),
            hg_p.reshape(DEPTH, bp, HG_H, HG_DK, HG_DV), seq_first(new_hg_t, HG_H, HG_DK),
            ssm_p.reshape(DEPTH, bp, SSM_H, SSM_P, SSM_N), ssm_s.reshape(DEPTH, bs, SSM_H, SSM_P, SSM_N),
            conv_p, time_minor(conv_s))
```

```python
import functools

import jax
import jax.numpy as jnp
from jax import lax
from jax.experimental import pallas as pl
from jax.experimental.pallas import tpu as pltpu

F32 = jnp.float32
BF16 = jnp.bfloat16

D_MODEL = 1024
D_FF = 2816
DEPTH = 2
EPS = 1e-6
PAST_LEN = 16384

C_MIX = 256
POOL_WINDOWS = (2, 4, 8, 16)
POOL_GC = 64
POOL_PAST = 15
GLA_H, GLA_DK, GLA_DV, GLA_RANK, GLA_TAU = 4, 32, 64, 16, 16.0
HG_H, HG_DK, HG_DV = 4, 64, 64
SSM_H, SSM_P, SSM_G, SSM_N, SSM_CONV = 4, 64, 2, 128, 4
SSM_CONV_DIM = 768
N_IN = 3092

OFF_PX, OFF_GQ, OFF_GK, OFF_GV, OFF_GR = 0, 256, 384, 512, 768
OFF_RQ, OFF_RF, OFF_RI, OFF_RG = 1024, 1280, 1536, 1792
OFF_SZ, OFF_XBC = 2048, 2304
OFF_LR = 3072
OFF_DTX = 3200
W_PROJ = 3456

PT_POOL_SCALE, PT_GBIAS, PT_GNORM, PT_HNORM, PT_CONVB, PT_DTB, PT_ALOG, PT_DSKIP, PT_SNORM = range(9)
PT_CONVW = 9
PT_LBL = 13
PT_ROWS = 16

VMEM_LIMIT_BYTES = 56 * 1024 * 1024
SUBLANES = 8
LANES = 128
HEADS_PER_STEP = LANES // GLA_DV

ROW_TILE = 512
FF_CHUNKS = (0, 768, 1536, 2304, D_FF)
PROMPT_CHUNK = 64
PROMPT_GROUP = 8
SAMPLE_GROUP = 16
SUB_BLOCK = 16
MAX_FACTORED_SPAN = 60.0
POOL_BASE = 24


def _sigmoid(x):
    return 1.0 / (1.0 + jnp.exp(-x))


def _silu(x):
    return x * _sigmoid(x)


def _log_sigmoid(x):
    return jnp.minimum(x, 0.0) - jnp.log(1.0 + jnp.exp(-jnp.abs(x)))


def _softplus(x):
    return jnp.maximum(x, 0.0) + jnp.log(1.0 + jnp.exp(-jnp.abs(x)))


def _rms(x, g):
    ms = jnp.mean(x * x, axis=-1, keepdims=True)
    return x * lax.rsqrt(ms + EPS) * g


def _dot(a, b):
    return jnp.dot(a.astype(BF16), b.astype(BF16), preferred_element_type=F32)


def _dot_nt(a, b):
    return lax.dot_general(a.astype(BF16), b.astype(BF16), (((1,), (1,)), ((), ())),
                           preferred_element_type=F32)


def _dot_tn(a, b):
    return lax.dot_general(a.astype(BF16), b.astype(BF16), (((0,), (0,)), ((), ())),
                           preferred_element_type=F32)


def _split3(x):
    hi = x.astype(BF16)
    r1 = x - hi.astype(F32)
    mid = r1.astype(BF16)
    lo = (r1 - mid.astype(F32)).astype(BF16)
    return hi, mid, lo


def _dot_exact_rhs(a01, x):
    hi, mid, lo = _split3(x)
    a = a01.astype(BF16)
    return (jnp.dot(a, hi, preferred_element_type=F32) + jnp.dot(a, mid, preferred_element_type=F32)
            + jnp.dot(a, lo, preferred_element_type=F32))


def _swiglu_half(x, g, wg_ref, wu_ref, wd_ref):
    h = _rms(x, g).astype(BF16)
    acc = None
    for lo, hi in zip(FF_CHUNKS[:-1], FF_CHUNKS[1:]):
        sl = slice(lo, hi)
        gate = jnp.dot(h, wg_ref[:, sl], preferred_element_type=F32)
        up = jnp.dot(h, wu_ref[:, sl], preferred_element_type=F32)
        act = (_silu(gate) * up).astype(BF16)
        d = jnp.dot(act, wd_ref[sl, :], preferred_element_type=F32)
        acc = d if acc is None else acc + d
    return x + 0.5 * acc


def _pre_kernel(*refs, n_first, split_in, layer):
    if split_in:
        x = jnp.where(pl.program_id(0) < n_first, refs[0][...].reshape(ROW_TILE, D_MODEL), refs[1][...])
        refs = refs[2:]
    else:
        x = refs[0][...]
        refs = refs[1:]
    g1_ref, wg_ref, wu_ref, wd_ref, gm_ref, win_ref, wgate_ref, ptab_ref, x1_ref, feat_ref = refs
    x1 = _swiglu_half(x, g1_ref[...], wg_ref, wu_ref, wd_ref)
    x1_ref[...] = x1
    h = _rms(x1, gm_ref[...]).astype(BF16)
    _mixer_features(h, win_ref, wgate_ref, ptab_ref, feat_ref, layer)


def _hgrn_lower_bound(ptab_ref, layer):
    lbl = ptab_ref[PT_LBL:PT_LBL + DEPTH, 0:C_MIX]
    lexp = jnp.exp(lbl - jnp.max(lbl, axis=0, keepdims=True))
    lsum = jnp.sum(lexp, axis=0, keepdims=True)
    lb = jnp.zeros((1, C_MIX), F32)
    for m in range(1, layer + 1):
        lb = lb + lexp[m:m + 1] / lsum
    return lb


def _mixer_features(h, win_ref, wgate_ref, ptab_ref, feat_ref, layer):
    prow = lambda r, w: ptab_ref[r:r + 1, 0:w]
    proj = lambda lo, hi: jnp.dot(h, win_ref[:, lo:hi], preferred_element_type=F32)
    d = proj(OFF_LR, W_PROJ)
    logit = _dot(d[:, 0:LANES], wgate_ref[...]) + prow(PT_GBIAS, LANES)
    feat_ref[:, OFF_LR:OFF_DTX] = _log_sigmoid(logit) * (1.0 / GLA_TAU)
    feat_ref[:, OFF_DTX:W_PROJ] = _softplus(d[:, LANES:LANES + C_MIX] + prow(PT_DTB, C_MIX))
    b = proj(OFF_RQ, OFF_XBC)
    feat_ref[:, OFF_RQ:OFF_RF] = _silu(b[:, 0:C_MIX])
    lb = _hgrn_lower_bound(ptab_ref, layer)
    log_lb = jnp.log(lb)
    t2 = jnp.log(1.0 - lb) + _log_sigmoid(b[:, C_MIX:2 * C_MIX])
    feat_ref[:, OFF_RF:OFF_RI] = jnp.maximum(log_lb, t2) + jnp.log(1.0 + jnp.exp(-jnp.abs(log_lb - t2)))
    feat_ref[:, OFF_RI:OFF_RG] = b[:, 2 * C_MIX:3 * C_MIX]
    feat_ref[:, OFF_RG:OFF_XBC] = _silu(b[:, 3 * C_MIX:5 * C_MIX])
    a = proj(OFF_PX, OFF_RQ)
    feat_ref[:, OFF_GR:OFF_RQ] = _silu(a[:, OFF_GR:OFF_RQ])
    feat_ref[:, OFF_GQ:OFF_GK] = a[:, OFF_GQ:OFF_GK] * (GLA_DK ** -0.5)
    feat_ref[:, OFF_PX:OFF_GQ] = a[:, OFF_PX:OFF_GQ]
    feat_ref[:, OFF_GK:OFF_GR] = a[:, OFF_GK:OFF_GR]
    feat_ref[:, OFF_XBC:OFF_LR] = proj(OFF_XBC, OFF_LR)


def _segment_rms(o, width, gain):
    n = o.shape[-1]
    seg = (lax.broadcasted_iota(jnp.int32, (n, n), 0) // width
           == lax.broadcasted_iota(jnp.int32, (n, n), 1) // width).astype(BF16)
    sq = o * o
    hi = sq.astype(BF16)
    lo = (sq - hi.astype(F32)).astype(BF16)
    ms = (jnp.dot(hi, seg, preferred_element_type=F32) + jnp.dot(lo, seg, preferred_element_type=F32)) * (1.0 / width)
    return o * lax.rsqrt(ms + EPS) * gain


def _mixer_outputs(cat_ref, ggla_ref, ghg_ref, gz_ref, ptab_ref):
    prow = lambda r, w: ptab_ref[r:r + 1, 0:w]
    o_gla = _segment_rms(cat_ref[:, C_MIX:2 * C_MIX], GLA_DV, prow(PT_GNORM, C_MIX)) * ggla_ref[...]
    o_hg = _segment_rms(cat_ref[:, 2 * C_MIX:3 * C_MIX], HG_DV, prow(PT_HNORM, C_MIX)) * ghg_ref[...]
    y = cat_ref[:, 3 * C_MIX:4 * C_MIX] * gz_ref[...]
    half = C_MIX // SSM_G
    o_ssm = [_rms(y[:, half * g:half * (g + 1)], ptab_ref[PT_SNORM:PT_SNORM + 1, half * g:half * (g + 1)])
             for g in range(SSM_G)]
    return jnp.concatenate([cat_ref[:, 0:C_MIX], o_gla, o_hg] + o_ssm, axis=1)


def _post_kernel(x_ref, cat_ref, ggla_ref, ghg_ref, gz_ref, ptab_ref, wout_ref, g2_ref, wg_ref, wu_ref, wd_ref,
                 gf_ref, *o_refs, n_first, final):
    mixed = _mixer_outputs(cat_ref, ggla_ref, ghg_ref, gz_ref, ptab_ref)
    x2 = x_ref[...] + jnp.dot(mixed.astype(BF16), wout_ref[...], preferred_element_type=F32)
    x3 = _swiglu_half(x2, g2_ref[...], wg_ref, wu_ref, wd_ref)
    if final:
        y = _rms(x3, gf_ref[...])
        i = pl.program_id(0)

        @pl.when(i < n_first)
        def _():
            o_refs[0][...] = y.reshape(o_refs[0].shape)

        @pl.when(i >= n_first)
        def _():
            o_refs[1][...] = y
    else:
        o_refs[0][...] = x3


def _layer_spec(shape, layer):
    nd = len(shape)
    return pl.BlockSpec((None,) + tuple(shape), lambda *_: (layer,) + (0,) * nd, pipeline_mode=pl.Buffered(1))


def _dense_params():
    return pltpu.CompilerParams(dimension_semantics=("arbitrary",), vmem_limit_bytes=VMEM_LIMIT_BYTES)


def _row_spec(width):
    return pl.BlockSpec((ROW_TILE, width), lambda i: (i, 0))


def _token_specs(n_first, n_prompt_seq):
    return [pl.BlockSpec((n_prompt_seq, PROMPT_CHUNK, D_MODEL), lambda i: (0, jnp.minimum(i, n_first - 1), 0)),
            pl.BlockSpec((ROW_TILE, D_MODEL), lambda i: (jnp.maximum(i - n_first, 0), 0))]


def _pre_call(xs, g1, wg, wu, wd, gm, win, wgate, ptab, layer, rows, n_first):
    split_in = len(xs) == 2
    x_specs = _token_specs(n_first, xs[0].shape[0]) if split_in else [_row_spec(D_MODEL)]
    return pl.pallas_call(
        functools.partial(_pre_kernel, n_first=n_first, split_in=split_in, layer=layer),
        grid=(rows // ROW_TILE,),
        in_specs=x_specs + [_layer_spec((1, D_MODEL), layer), _layer_spec((D_MODEL, D_FF), layer),
                            _layer_spec((D_MODEL, D_FF), layer), _layer_spec((D_FF, D_MODEL), layer),
                            _layer_spec((1, D_MODEL), layer),
                            pl.BlockSpec((D_MODEL, W_PROJ), lambda i: (0, 0), pipeline_mode=pl.Buffered(1)),
                            _layer_spec(wgate.shape[1:], layer), _layer_spec(ptab.shape[1:], layer)],
        out_specs=[_row_spec(D_MODEL), _row_spec(W_PROJ)],
        out_shape=[jax.ShapeDtypeStruct((rows, D_MODEL), F32), jax.ShapeDtypeStruct((rows, W_PROJ), F32)],
        compiler_params=_dense_params(),
        name="pre_ffn_inproj",
    )(*xs, g1, wg, wu, wd, gm, win, wgate, ptab)


def _post_call(x, cat, feat, ptab, wout, g2, wg, wu, wd, gf, layer, n_first, final_shapes):
    rows = x.shape[0]
    if final_shapes is not None:
        out_specs = _token_specs(n_first, final_shapes[0][0])
        out_shape = [jax.ShapeDtypeStruct(s, F32) for s in final_shapes]
    else:
        out_specs = [_row_spec(D_MODEL)]
        out_shape = [jax.ShapeDtypeStruct((rows, D_MODEL), F32)]
    gate_spec = lambda off: pl.BlockSpec((ROW_TILE, C_MIX), lambda i: (i, off // C_MIX))
    return pl.pallas_call(
        functools.partial(_post_kernel, n_first=n_first, final=final_shapes is not None),
        grid=(rows // ROW_TILE,),
        in_specs=[_row_spec(D_MODEL), _row_spec(D_MODEL), gate_spec(OFF_GR), gate_spec(OFF_RG), gate_spec(OFF_SZ),
                  _layer_spec(ptab.shape[1:], layer), _layer_spec((D_MODEL, D_MODEL), layer),
                  _layer_spec((1, D_MODEL), layer), _layer_spec((D_MODEL, D_FF), layer),
                  _layer_spec((D_MODEL, D_FF), layer), _layer_spec((D_FF, D_MODEL), layer),
                  pl.BlockSpec((1, D_MODEL), lambda i: (0, 0))],
        out_specs=out_specs,
        out_shape=out_shape,
        compiler_params=_dense_params(),
        name="post_outproj_ffn",
    )(x, cat, feat, feat, feat, ptab, wout, g2, wg, wu, wd, gf)


def _lane_group(shape, width):
    return lax.broadcasted_iota(jnp.int32, shape, len(shape) - 1) // width


def _block_diag_mask(rows, row_w, cols, col_w):
    r = lax.broadcasted_iota(jnp.int32, (rows, cols), 0) // row_w
    c = lax.broadcasted_iota(jnp.int32, (rows, cols), 1) // col_w
    return r == c


def _tril(n):
    r = lax.broadcasted_iota(jnp.int32, (n, n), 0)
    c = lax.broadcasted_iota(jnp.int32, (n, n), 1)
    return r >= c


def _select_heads(stacked, n_heads, rows, head_w):
    grp = _lane_group((rows, n_heads * head_w), head_w)
    out = jnp.zeros((rows, n_heads * head_w), F32)
    for h in range(n_heads):
        out = jnp.where(grp == h, stacked[h * rows:(h + 1) * rows], out)
    return out


def _cum_log_decay(la):
    return _dot_exact_rhs(_tril(la.shape[0]).astype(F32), la)


def _decay_span(b):
    mid = b.shape[0] // 2 - 1
    dev = jnp.abs(b - b[mid:mid + 1, :])
    dev = functools.reduce(jnp.maximum, [dev[:, LANES * j:LANES * (j + 1)] for j in range(dev.shape[1] // LANES)])
    return jnp.max(dev.reshape(-1, SUBLANES, LANES), axis=0)


def _la_factored(q, k, v, b, st, n_heads, dk):
    chunk = q.shape[0]
    nk, nv = n_heads * dk, n_heads * 64
    mid = chunk // 2 - 1
    m = b[mid:mid + 1, :]
    qt = q * jnp.exp(b - m)
    kt = k * jnp.exp(m - b)
    k_bd = jnp.where(_block_diag_mask(n_heads * chunk, chunk, nk, dk), jnp.concatenate([kt] * n_heads, axis=0), 0.0)
    att = _dot_nt(qt, k_bd)
    row = lax.broadcasted_iota(jnp.int32, att.shape, 0)
    col = lax.broadcasted_iota(jnp.int32, att.shape, 1) % chunk
    att = jnp.where(row >= col, att, 0.0)
    v_bd = jnp.where(_block_diag_mask(n_heads * chunk, chunk, nv, 64), jnp.concatenate([v] * n_heads, axis=0), 0.0)
    o = _dot(att, v_bd) + _dot_nt(qt * jnp.exp(m), st)
    b_last = b[chunk - 1:chunk, :]
    upd = _dot_tn(v, kt * jnp.exp(b_last - m))
    st_new = st * jnp.exp(b_last) + jnp.where(_block_diag_mask(nv, 64, nk, dk), upd, 0.0)
    return o, st_new


def _la_direct(q, k, v, b, st, n_heads, dk, sub):
    chunk = q.shape[0]
    nk = n_heads * dk
    nv = n_heads * 64
    n_sub = chunk // sub
    refs = [None] + [b[sub * i - 1:sub * i, :] for i in range(1, n_sub)]
    c_loc = jnp.concatenate([b[0:sub]] + [b[sub * i:sub * (i + 1)] - refs[i] for i in range(1, n_sub)], axis=0) \
        if n_sub > 1 else b
    o = _dot_nt(q * jnp.exp(b), st)

    qt = q * jnp.exp(c_loc)
    kgrp = _lane_group((sub, nk), dk)
    outs = []
    for i in range(n_sub):
        rows = slice(sub * i, sub * (i + 1))
        o_i = o[rows]
        if i > 0:
            prev = slice(0, sub * i)
            kt = k[prev] * jnp.exp(refs[i] - b[prev])
            q_stack = jnp.concatenate([jnp.where(kgrp == h, qt[rows], 0.0) for h in range(n_heads)], axis=0)
            att = _dot_nt(q_stack, kt)
            o_i = o_i + _select_heads(_dot(att, v[prev]), n_heads, sub, 64)
        c_i = c_loc[rows]
        q_i = q[rows]
        row_id = lax.broadcasted_iota(jnp.int32, (sub, nk), 0)
        e_list = []
        for j in range(sub):
            c_j = c_loc[sub * i + j:sub * i + j + 1, :]
            k_j = k[sub * i + j:sub * i + j + 1, :]
            dec = jnp.exp(jnp.where(row_id >= j, c_i - c_j, -jnp.inf))
            e_list.append(dec * q_i * k_j)
        e_all = jnp.concatenate(e_list, axis=0)
        expand = _block_diag_mask(nk, dk, nv, 64).astype(F32)
        r_d = _dot(e_all, expand)
        for j in range(sub):
            o_i = o_i + r_d[j * sub:(j + 1) * sub] * v[sub * i + j:sub * i + j + 1, :]
        outs.append(o_i)
    o = jnp.concatenate(outs, axis=0) if n_sub > 1 else outs[0]

    b_last = b[chunk - 1:chunk, :]
    k_end = k * jnp.exp(b_last - b)
    upd = _dot_tn(v, k_end)
    st_new = st * jnp.exp(b_last) + jnp.where(_block_diag_mask(nv, 64, nk, dk), upd, 0.0)
    return o, st_new


def _load_state_t(s, n_heads, dk):
    st = s.T
    tiled = jnp.concatenate([st] * n_heads, axis=0)
    return jnp.where(_block_diag_mask(n_heads * 64, 64, n_heads * dk, dk), tiled, 0.0)


def _store_state_t(st, n_heads, dk):
    acc = st[0:64]
    for h in range(1, n_heads):
        acc = acc + st[64 * h:64 * (h + 1)]
    return acc.T


def _pool_mixer(proj_ref, pool_scr, poolw_ref, scale, chunk, pos_first):
    xp = proj_ref[:, OFF_PX:OFF_PX + C_MIX]
    pool_scr[0, POOL_BASE:POOL_BASE + chunk, :] = xp
    n_ext = POOL_BASE - SUBLANES + chunk
    for s, shift in enumerate((1, 2, 4)):
        cur = pool_scr[s, SUBLANES:SUBLANES + n_ext, :]
        prev = pool_scr[s, SUBLANES - shift:SUBLANES - shift + n_ext, :]
        pool_scr[s + 1, SUBLANES:SUBLANES + n_ext, :] = cur + prev
    s2 = pool_scr[1, POOL_BASE:POOL_BASE + chunk, :]
    s4 = pool_scr[2, POOL_BASE:POOL_BASE + chunk, :]
    s8 = pool_scr[3, POOL_BASE:POOL_BASE + chunk, :]
    s16 = s8 + pool_scr[3, POOL_BASE - 8:POOL_BASE - 8 + chunk, :]
    grp = _lane_group((chunk, C_MIX), POOL_GC)
    win_sum = jnp.where(grp == 0, s2, jnp.where(grp == 1, s4, jnp.where(grp == 2, s8, s16)))
    win = jnp.where(grp == 0, 2, jnp.where(grp == 1, 4, jnp.where(grp == 2, 8, 16)))
    pos = pos_first + lax.broadcasted_iota(jnp.int32, (chunk, C_MIX), 0)
    cnt = jnp.minimum(pos + 1, win).astype(F32)
    d = win_sum / cnt - xp
    o_pool = _dot(d, poolw_ref[...]) * scale
    new_pool = pool_scr[0, POOL_BASE + chunk - POOL_PAST:POOL_BASE + chunk, :]
    pool_scr[0, POOL_BASE - POOL_PAST:POOL_BASE, :] = new_pool
    return o_pool, new_pool


def _ssd_conv(proj_ref, conv_scr, prow, chunk):
    conv_base = SUBLANES
    conv_scr[conv_base:conv_base + chunk, :] = proj_ref[:, OFF_XBC:OFF_XBC + SSM_CONV_DIM]
    conv = None
    for w in range(SSM_CONV):
        start = conv_base - (SSM_CONV - 1) + w
        term = conv_scr[start:start + chunk, :] * prow(PT_CONVW + w, SSM_CONV_DIM)
        conv = term if conv is None else conv + term
    conv = _silu(conv + prow(PT_CONVB, SSM_CONV_DIM))
    new_conv = conv_scr[conv_base + chunk - (SSM_CONV - 1):conv_base + chunk, :]
    conv_scr[conv_base - (SSM_CONV - 1):conv_base, :] = new_conv
    return conv, new_conv


def _ssd_scan(proj_ref, conv, ssm_st, prow, chunk):
    xs = conv[:, 0:C_MIX]
    dt = proj_ref[:, OFF_DTX:OFF_DTX + C_MIX]
    a_neg = -jnp.exp(prow(PT_ALOG, C_MIX))
    la = dt * a_neg
    xdt = xs * dt
    cum = _cum_log_decay(la)
    heads_per_group = SSM_H // SSM_G
    b_of = lambda g: conv[:, C_MIX + SSM_N * g:C_MIX + SSM_N * (g + 1)]
    c_of = lambda g: conv[:, C_MIX + SSM_G * SSM_N + SSM_N * g:C_MIX + SSM_G * SSM_N + SSM_N * (g + 1)]
    if chunk == SSM_P:
        row = lax.broadcasted_iota(jnp.int32, (chunk, C_MIX), 0)
        col = lax.broadcasted_iota(jnp.int32, (chunk, C_MIX), 1) % chunk
        cum_j = jnp.sum(jnp.where(row == col, cum, 0.0), axis=0, keepdims=True)
        dec = jnp.exp(jnp.where(row >= col, cum - cum_j, -jnp.inf))
        cb = jnp.concatenate([_dot_nt(c_of(g), jnp.concatenate([b_of(g)] * heads_per_group, axis=0))
                              for g in range(SSM_G)], axis=1)
        x_bd = jnp.where(_block_diag_mask(SSM_H * chunk, chunk, C_MIX, SSM_P),
                         jnp.concatenate([xdt] * SSM_H, axis=0), 0.0)
        y = _dot(cb * dec, x_bd)
    else:
        cum_t = cum.T
        causal = _tril(chunk)
        att_rows = []
        for h in range(SSM_H):
            if h % heads_per_group == 0:
                cb = _dot_nt(c_of(h // heads_per_group), b_of(h // heads_per_group))
            col = cum[:, SSM_P * h:SSM_P * h + 1]
            row = cum_t[SSM_P * h:SSM_P * h + 1, :]
            att_rows.append(cb * jnp.exp(jnp.where(causal, col - row, -jnp.inf)))
        y = _select_heads(_dot(jnp.concatenate(att_rows, axis=0), xdt), SSM_H, chunk, SSM_P)
    cum_last = cum[chunk - 1:chunk, :]
    decay_in = jnp.exp(cum)
    xw = xdt * jnp.exp(cum_last - cum)
    decay_state = jnp.exp(cum_last)
    y_state = []
    for g in range(SSM_G):
        lanes = slice(SSM_N * g, SSM_N * (g + 1))
        st = ssm_st[g]
        y_state.append(_dot(c_of(g), st))
        ssm_st[g] = st * decay_state[:, lanes] + _dot_tn(b_of(g), xw[:, lanes])
    return y + jnp.concatenate(y_state, axis=1) * decay_in + prow(PT_DSKIP, C_MIX) * xs


def _mixer_kernel(*refs, chunk, group, pos0, layer, carried, n_alias, lin):
    feat_all = refs[0]
    n_state = 5 if lin else 3
    n_in = 1 + (n_state if carried else 0)
    poolw_ref, ptab_ref = refs[n_in:n_in + 2]
    n_in += 2 + n_alias
    cat_all = refs[n_in]
    new_states = refs[n_in + 1:n_in + 1 + n_state]
    scratch = refs[n_in + 1 + n_state:]
    if lin:
        npool_ref, ngla_ref, nhg_ref, nssm_ref, nconv_ref = new_states
        pool_scr, conv_scr, gla_st, hg_st, ssm_st = scratch
    else:
        npool_ref, nssm_ref, nconv_ref = new_states
        pool_scr, conv_scr, ssm_st = scratch
    seq_state = (lambda ref, s: ref.at[s]) if lin else (lambda ref, s: ref.at[:, s])
    prow = lambda r, w: ptab_ref[r:r + 1, 0:w]
    c = pl.program_id(1)
    n_c = pl.num_programs(1)
    sub = min(SUB_BLOCK, chunk)
    conv_base = SUBLANES

    @pl.when(c == 0)
    def _():
        pool_scr[...] = jnp.zeros(pool_scr.shape, F32)
        conv_scr[...] = jnp.zeros(conv_scr.shape, F32)
        if carried:
            carried_refs = refs[1:1 + n_state]
            poolb_ref, ssms_ref, convb_ref = carried_refs[0], carried_refs[-2], carried_refs[-1]
            for s in range(group):
                pool_scr[s, 0, POOL_BASE - POOL_PAST:POOL_BASE, :] = seq_state(poolb_ref, s)[...]
                conv_scr[s, conv_base - (SSM_CONV - 1):conv_base, :] = seq_state(convb_ref, s)[...]
                for g in range(SSM_G):
                    ssm_st[s, g] = ssms_ref[s, g].T
                if lin:
                    gla_st[s] = _load_state_t(carried_refs[1][s], GLA_H, GLA_DK)
                    hg_st[s] = _load_state_t(carried_refs[2][s], HG_H, HG_DK)
        else:
            ssm_st[...] = jnp.zeros(ssm_st.shape, F32)
            if lin:
                gla_st[...] = jnp.zeros(gla_st.shape, F32)
                hg_st[...] = jnp.zeros(hg_st.shape, F32)

    seq_rows = lambda ref, s: ref.at[pl.ds(s * chunk, chunk)]
    new_pools, new_convs, convs, cum_decays, spans = [], [], [], [], []

    def conv_stage(s):
        conv, new_conv = _ssd_conv(seq_rows(feat_all, s), conv_scr.at[s], prow, chunk)
        convs.append(conv)
        new_convs.append(new_conv)

    def decay_stage(s):
        feat_ref = seq_rows(feat_all, s)
        b_gla = _cum_log_decay(feat_ref[:, OFF_LR:OFF_LR + GLA_H * GLA_DK])
        b_hg = _cum_log_decay(feat_ref[:, OFF_RF:OFF_RF + C_MIX])
        cum_decays.extend([b_gla, b_hg])
        spans.extend([_decay_span(b_gla), _decay_span(b_hg)])

    def scan_stage(s):
        seq_rows(cat_all, s)[:, 3 * C_MIX:4 * C_MIX] = _ssd_scan(seq_rows(feat_all, s), convs[s], ssm_st.at[s], prow,
                                                                chunk)

    def pool_stage(s):
        o_pool, new_pool = _pool_mixer(seq_rows(feat_all, s), pool_scr.at[s], poolw_ref, prow(PT_POOL_SCALE, C_MIX),
                                       chunk, pos0 + c * chunk)
        seq_rows(cat_all, s)[:, 0:C_MIX] = o_pool
        new_pools.append(new_pool)

    stages = (decay_stage, conv_stage, scan_stage, pool_stage) if lin else (conv_stage, scan_stage, pool_stage)
    if chunk >= PROMPT_CHUNK:
        for stage in stages:
            for s in range(group):
                stage(s)
    else:
        for s in range(group):
            for stage in stages:
                stage(s)

    if lin:
        _linear_attention_step(feat_all, cat_all, gla_st, hg_st, cum_decays, spans, seq_rows, group, sub)
    else:
        for s in range(group):
            seq_rows(cat_all, s)[:, C_MIX:3 * C_MIX] = jnp.zeros((chunk, 2 * C_MIX), F32)

    @pl.when(c == n_c - 1)
    def _():
        for s in range(group):
            seq_state(npool_ref, s)[...] = new_pools[s]
            seq_state(nconv_ref, s)[...] = new_convs[s]
            for g in range(SSM_G):
                nssm_ref[s, g] = ssm_st[s, g].T
            if lin:
                ngla_ref[s] = _store_state_t(gla_st[s], GLA_H, GLA_DK)
                nhg_ref[s] = _store_state_t(hg_st[s], HG_H, HG_DK)


def _linear_attention_step(feat_all, cat_all, gla_st, hg_st, cum_decays, spans, seq_rows, group, sub):
    narrow = jnp.max(functools.reduce(jnp.maximum, spans)) <= MAX_FACTORED_SPAN

    def run(la_gla, la_hg, *cums):
        for s in range(group):
            feat_ref, cat_ref = seq_rows(feat_all, s), seq_rows(cat_all, s)
            o_g, st_g = la_gla(feat_ref[:, OFF_GQ:OFF_GK], feat_ref[:, OFF_GK:OFF_GV], feat_ref[:, OFF_GV:OFF_GR],
                               cums[2 * s], gla_st[s])
            gla_st[s] = st_g
            cat_ref[:, C_MIX:2 * C_MIX] = o_g
            hk = 1.0 - jnp.exp(feat_ref[:, OFF_RF:OFF_RI])
            o_h, st_h = la_hg(feat_ref[:, OFF_RQ:OFF_RF], hk, feat_ref[:, OFF_RI:OFF_RG], cums[2 * s + 1], hg_st[s])
            hg_st[s] = st_h
            cat_ref[:, 2 * C_MIX:3 * C_MIX] = o_h

    factored = functools.partial(run, functools.partial(_la_factored, n_heads=GLA_H, dk=GLA_DK),
                                 functools.partial(_la_factored, n_heads=HG_H, dk=HG_DK))
    direct = functools.partial(run, functools.partial(_la_direct, n_heads=GLA_H, dk=GLA_DK, sub=sub),
                               functools.partial(_la_direct, n_heads=HG_H, dk=HG_DK, sub=sub))
    lax.cond(narrow, factored, direct, *cum_decays)


def _mixer_call(proj, cat_prev, carried, prev_states, mats, *, n_seq, seq_len, row0, chunk_major, chunk, group,
                pos0, layer, lin):
    rows = proj.shape[0]
    n_c, n_b, blk = seq_len // chunk, n_seq // group, group * chunk
    assert chunk_major or n_c == 1
    row_map = lambda b, c: (row0 // blk + c * n_b + b, 0)

    def seq_major(tail):
        nd = len(tail)
        return (DEPTH, n_seq) + tail, pl.BlockSpec((None, group) + tail, lambda b, c: (layer, b) + (0,) * nd)

    def time_major(n_rows, width):
        return (DEPTH, n_rows, n_seq, width), pl.BlockSpec((None, n_rows, group, width), lambda b, c: (layer, 0, b, 0))

    ssm = seq_major((SSM_G, 128, SSM_N))
    if lin:
        states = [seq_major((POOL_PAST, C_MIX)), seq_major((GLA_H * GLA_DK, GLA_DV)), seq_major((HG_H * HG_DK, HG_DV)),
                  ssm, seq_major((SSM_CONV - 1, SSM_CONV_DIM))]
    else:
        states = [time_major(POOL_PAST, C_MIX), ssm, time_major(SSM_CONV - 1, SSM_CONV_DIM)]

    def mat_spec(m):
        nd = m.ndim - 1
        return pl.BlockSpec((None,) + m.shape[1:], lambda b, c: (layer,) + (0,) * nd)

    state_specs = [spec for _, spec in states]
    in_specs = [pl.BlockSpec((blk, W_PROJ), row_map)]
    args = [proj]
    if carried is not None:
        in_specs += state_specs
        args += list(carried)
    in_specs += [mat_spec(m) for m in mats]
    args += list(mats)
    aliases = {}
    if cat_prev is not None:
        aliases[len(args)] = 0
        args.append(cat_prev)
    if prev_states is not None:
        for i, s in enumerate(prev_states):
            aliases[len(args)] = 1 + i
            args.append(s)
    in_specs += [pl.BlockSpec(memory_space=pl.ANY)] * len(aliases)
    out_shape = [jax.ShapeDtypeStruct((rows, D_MODEL), F32)] + [jax.ShapeDtypeStruct(shape, F32) for shape, _ in states]
    out_specs = [pl.BlockSpec((blk, D_MODEL), row_map)] + state_specs
    scratch = [pltpu.VMEM((group, 4, POOL_BASE + chunk, C_MIX), F32),
               pltpu.VMEM((group, SUBLANES + chunk, SSM_CONV_DIM), F32)]
    if lin:
        scratch += [pltpu.VMEM((group, GLA_H * GLA_DV, GLA_H * GLA_DK), F32),
                    pltpu.VMEM((group, HG_H * HG_DV, HG_H * HG_DK), F32)]
    scratch.append(pltpu.VMEM((group, SSM_G, SSM_N, 128), F32))
    return pl.pallas_call(
        functools.partial(_mixer_kernel, chunk=chunk, group=group, pos0=pos0, layer=layer,
                          carried=carried is not None, n_alias=len(aliases), lin=lin),
        grid=(n_b, n_c),
        in_specs=in_specs,
        out_specs=out_specs,
        out_shape=out_shape,
        scratch_shapes=scratch,
        input_output_aliases=aliases,
        compiler_params=pltpu.CompilerParams(dimension_semantics=("arbitrary", "arbitrary"),
                                             vmem_limit_bytes=VMEM_LIMIT_BYTES),
        name=f"mixer_chunk{chunk}",
    )(*args)


def _batched_recurrence_kernel(*refs, n_seq, seq_len, dk, own_keys, n_alias):
    n_in = 5 if own_keys else 4
    q_ref, v_ref, la_ref, s_in = refs[0], refs[n_in - 3], refs[n_in - 2], refs[n_in - 1]
    o_ref, s_out, q_t, k_t, a_t, v_t = refs[n_in + n_alias:]
    key_w = HEADS_PER_STEP * dk
    base = 0 if key_w == q_ref.shape[1] else pl.multiple_of(pl.program_id(0) * key_w, key_w)
    for t in range(seq_len):
        s_prev = s_in if t == 0 else s_out
        rows = pl.ds(t, n_seq, stride=seq_len)
        a = jnp.exp(la_ref[rows, :])
        k = refs[1][rows, :] if own_keys else 1.0 - a
        q_t[...] = q_ref[rows, :].T
        k_t[...] = k.T
        a_t[...] = a.T
        v_t[...] = v_ref[rows, :].T
        heads = []
        for h in range(HEADS_PER_STEP):
            v_h = v_t[GLA_DV * h:GLA_DV * (h + 1), :]

            def body(i, acc, h=h, v_h=v_h, s_prev=s_prev):
                r = h * dk + i
                s = a_t[pl.ds(base + r, 1), :] * s_prev[r] + k_t[pl.ds(base + r, 1), :] * v_h
                s_out[r] = s
                return acc + q_t[pl.ds(base + r, 1), :] * s

            heads.append(lax.fori_loop(0, dk, body, jnp.zeros((GLA_DV, n_seq), F32), unroll=8))
        o_ref[rows, :] = jnp.concatenate(heads, axis=0).T


def _batched_recurrence_call(feat, cat, state_t, prev_state, *, layer, row0, n_seq, seq_len, q_off, k_off, v_off,
                             la_off, out_off, n_heads, dk):
    n_rows = n_seq * seq_len
    rb = row0 // n_rows
    key_w = HEADS_PER_STEP * dk
    assert key_w in (LANES // 2, LANES) and n_heads % HEADS_PER_STEP == 0

    def lanes(off, per_step):
        return pl.BlockSpec((n_rows, LANES), lambda i: (rb, off // LANES + (i if per_step else 0)))

    key = lambda off: lanes(off, key_w == LANES)
    in_specs = [key(q_off)] + ([key(k_off)] if k_off is not None else []) + [lanes(v_off, True), key(la_off)]
    args = [feat] * len(in_specs)
    state_spec = pl.BlockSpec((None, key_w, GLA_DV, n_seq), lambda i: (layer, i, 0, 0))
    in_specs.append(state_spec)
    args.append(state_t)
    aliases = {len(args): 0}
    args.append(cat)
    if prev_state is not None:
        aliases[len(args)] = 1
        args.append(prev_state)
    in_specs += [pl.BlockSpec(memory_space=pl.ANY)] * len(aliases)
    return pl.pallas_call(
        functools.partial(_batched_recurrence_kernel, n_seq=n_seq, seq_len=seq_len, dk=dk,
                          own_keys=k_off is not None, n_alias=len(aliases)),
        grid=(n_heads // HEADS_PER_STEP,),
        in_specs=in_specs,
        out_specs=[lanes(out_off, True), state_spec],
        out_shape=[jax.ShapeDtypeStruct(cat.shape, F32), jax.ShapeDtypeStruct(state_t.shape, F32)],
        scratch_shapes=[pltpu.VMEM((LANES, n_seq), F32)] * 4,
        input_output_aliases=aliases,
        compiler_params=pltpu.CompilerParams(dimension_semantics=("arbitrary",), vmem_limit_bytes=VMEM_LIMIT_BYTES),
        name=f"batched_recurrence_dk{dk}",
    )(*args)


LR_SRC = 1024
DT_SRC = N_IN - SSM_H


def _arrange_w_in_kernel(wt_ref, *o_refs):
    def put(col, slab):
        for layer, o_ref in enumerate(o_refs):
            o_ref[:, col:col + LANES] = slab[:, layer, :].T.astype(BF16)

    for j in range(LR_SRC // LANES):
        put(LANES * j, wt_ref[LANES * j:LANES * (j + 1)])
    for j in range((OFF_LR - LR_SRC) // LANES):
        src = LR_SRC + GLA_RANK + LANES * j
        put(LR_SRC + LANES * j, wt_ref[src:src + LANES])
    row = lax.broadcasted_iota(jnp.int32, (LANES, DEPTH, D_MODEL), 0)
    dt_row = lambda h: jnp.broadcast_to(wt_ref[DT_SRC + h:DT_SRC + h + 1], (LANES, DEPTH, D_MODEL))
    tail = jnp.where(row < GLA_RANK, wt_ref[LR_SRC:LR_SRC + LANES], 0.0)
    for h in range(SSM_H):
        tail = jnp.where(row == GLA_RANK + h, dt_row(h), tail)
    put(OFF_LR, tail)
    for j in range(C_MIX // LANES):
        rep = jnp.zeros((LANES, DEPTH, D_MODEL), F32)
        for h in range(HEADS_PER_STEP * j, HEADS_PER_STEP * (j + 1)):
            rep = jnp.where(row // SSM_P == h - HEADS_PER_STEP * j, dt_row(h), rep)
        put(OFF_DTX + LANES * j, rep)


def _arrange_w_in(w):
    wt = jnp.transpose(w, (2, 0, 1))
    return pl.pallas_call(
        _arrange_w_in_kernel,
        grid=(1,),
        in_specs=[pl.BlockSpec(wt.shape, lambda i: (0, 0, 0), pipeline_mode=pl.Buffered(1))],
        out_specs=[pl.BlockSpec((D_MODEL, W_PROJ), lambda i: (0, 0))] * DEPTH,
        out_shape=[jax.ShapeDtypeStruct((D_MODEL, W_PROJ), BF16)] * DEPTH,
        compiler_params=pltpu.CompilerParams(dimension_semantics=("arbitrary",), vmem_limit_bytes=VMEM_LIMIT_BYTES),
        name="arrange_w_in",
    )(wt)


def _pool_block_diag(pw):
    eye = jnp.eye(len(POOL_WINDOWS), dtype=pw.dtype)
    return jnp.einsum('lgcd,gh->lgchd', pw, eye).reshape(pw.shape[0], C_MIX, C_MIX)


def _param_table(pool_scale, gla_gate_bias, gla_norm, hgrn_norm, ssm_conv_b, ssm_dt_bias, ssm_A_log, ssm_D,
                 ssm_norm, ssm_conv_w, hgrn_lb_logits):
    def rows(v):
        v = v.astype(F32).reshape(DEPTH, -1, v.shape[-1])
        return jnp.pad(v, ((0, 0), (0, 0), (0, SSM_CONV_DIM - v.shape[-1])))
    lbl = jnp.broadcast_to(hgrn_lb_logits[None], (DEPTH,) + hgrn_lb_logits.shape)
    parts = [rows(pool_scale), rows(gla_gate_bias), rows(jnp.tile(gla_norm, (1, GLA_H))),
             rows(jnp.tile(hgrn_norm, (1, HG_H))), rows(ssm_conv_b), rows(jnp.repeat(ssm_dt_bias, SSM_P, axis=-1)),
             rows(jnp.repeat(ssm_A_log, SSM_P, axis=-1)), rows(jnp.repeat(ssm_D, SSM_P, axis=-1)), rows(ssm_norm),
             rows(ssm_conv_w), rows(lbl)]
    tab = jnp.concatenate(parts, axis=1)
    return jnp.pad(tab, ((0, 0), (0, PT_ROWS - tab.shape[1]), (0, 0)))


def kernel(x_prompt, x_sample, state_pool, state_gla, state_hgrn, state_ssm, state_conv, ffn1_norm, ffn1_w_gate, ffn1_w_up, ffn1_w_down, mix_norm, w_in, pool_w, pool_scale, gla_w_gate, gla_gate_bias, gla_norm, hgrn_lb_logits, hgrn_norm, ssm_conv_w, ssm_conv_b, ssm_dt_bias, ssm_A_log, ssm_D, ssm_norm, w_out, ffn2_norm, ffn2_w_gate, ffn2_w_up, ffn2_w_down, final_norm):
    bp, tp, _ = x_prompt.shape
    bs, ts, _ = x_sample.shape
    rows_p, rows_s = bp * tp, bs * ts
    gain = lambda v: v.reshape(DEPTH, 1, D_MODEL).astype(F32)
    bf = lambda w: w.astype(BF16)

    wgate = bf(jnp.pad(gla_w_gate, ((0, 0), (0, 128 - GLA_RANK), (0, 0))))
    ptab = _param_table(pool_scale, gla_gate_bias, gla_norm, hgrn_norm, ssm_conv_b, ssm_dt_bias, ssm_A_log, ssm_D,
                        ssm_norm, ssm_conv_w, hgrn_lb_logits)
    mats = [bf(_pool_block_diag(pool_w)), ptab]
    seq_last = lambda s: jnp.transpose(s, (0, 2, 3, 4, 1)).reshape(DEPTH, -1, s.shape[-1], bs)
    carried = (jnp.transpose(state_pool, (0, 2, 1, 3)), state_ssm.reshape(DEPTH, bs, SSM_G, 128, SSM_N),
               jnp.transpose(state_conv, (0, 2, 1, 3)))
    gla_t, hg_t = seq_last(state_gla), seq_last(state_hgrn)
    ffn1 = (gain(ffn1_norm), bf(ffn1_w_gate), bf(ffn1_w_up), bf(ffn1_w_down))
    ffn2 = (gain(ffn2_norm), bf(ffn2_w_gate), bf(ffn2_w_up), bf(ffn2_w_down))
    gm, win, wout = gain(mix_norm), _arrange_w_in(w_in), bf(w_out)
    gf = final_norm.reshape(1, D_MODEL).astype(F32)

    assert bp * PROMPT_CHUNK == ROW_TILE and rows_s % ROW_TILE == 0
    n_first = rows_p // ROW_TILE
    xs = [x_prompt, x_sample.reshape(rows_s, D_MODEL)]
    states_p = states_s = new_gla_t = new_hg_t = None
    for l in range(DEPTH):
        x1, feat = _pre_call(xs, *ffn1, gm, win[l], wgate, ptab, l, rows_p + rows_s, n_first)
        outs_p = _mixer_call(feat, None, None, states_p, mats, n_seq=bp, seq_len=tp, row0=0, chunk_major=True,
                             chunk=PROMPT_CHUNK, group=PROMPT_GROUP, pos0=0, layer=l, lin=True)
        outs_s = _mixer_call(feat, outs_p[0], carried, states_s, mats, n_seq=bs, seq_len=ts, row0=rows_p,
                             chunk_major=False, chunk=ts, group=SAMPLE_GROUP, pos0=PAST_LEN, layer=l, lin=False)
        states_p, states_s = outs_p[1:], outs_s[1:]
        sample = dict(layer=l, row0=rows_p, n_seq=bs, seq_len=ts)
        cat, new_gla_t = _batched_recurrence_call(feat, outs_s[0], gla_t, new_gla_t, q_off=OFF_GQ, k_off=OFF_GK,
                                                  v_off=OFF_GV, la_off=OFF_LR, out_off=C_MIX, n_heads=GLA_H,
                                                  dk=GLA_DK, **sample)
        cat, new_hg_t = _batched_recurrence_call(feat, cat, hg_t, new_hg_t, q_off=OFF_RQ, k_off=None, v_off=OFF_RI,
                                                 la_off=OFF_RF, out_off=2 * C_MIX, n_heads=HG_H, dk=HG_DK, **sample)
        final_shapes = ((bp, tp, D_MODEL), (rows_s, D_MODEL)) if l == DEPTH - 1 else None
        xs = _post_call(x1, cat, feat, ptab, wout, *ffn2, gf, l, n_first, final_shapes)

    seq_first = lambda s, h, dk: jnp.transpose(s.reshape(DEPTH, h, dk, s.shape[-2], bs), (0, 4, 1, 2, 3))
    time_minor = lambda s: jnp.transpose(s, (0, 2, 1, 3))
    pool_p, gla_p, hg_p, ssm_p, conv_p = states_p
    pool_s, ssm_s, conv_s = states_s
    return (xs[0], xs[1].reshape(bs, ts, D_MODEL),
            pool_p, time_minor(pool_s),
            gla_p.reshape(DEPTH, bp, GLA_H, GLA_DK, GLA_DV), seq_first(new_gla_t, GLA_H, GLA_DK),
            hg_p.reshape(DEPTH, bp, HG_H, HG_DK, HG_DV), seq_first(new_hg_t, HG_H, HG_DK),
            ssm_p.reshape(DEPTH, bp, SSM_H, SSM_P, SSM_N), ssm_s.reshape(DEPTH, bs, SSM_H, SSM_P, SSM_N),
            conv_p, time_minor(conv_s))
```

```python
import functools

import jax
import jax.numpy as jnp
from jax import lax
from jax.experimental import pallas as pl
from jax.experimental.pallas import tpu as pltpu

F32 = jnp.float32
BF16 = jnp.bfloat16

D_MODEL = 1024
D_FF = 2816
DEPTH = 2
EPS = 1e-6
PAST_LEN = 16384

C_MIX = 256
POOL_WINDOWS = (2, 4, 8, 16)
POOL_GC = 64
POOL_PAST = 15
GLA_H, GLA_DK, GLA_DV, GLA_RANK, GLA_TAU = 4, 32, 64, 16, 16.0
HG_H, HG_DK, HG_DV = 4, 64, 64
SSM_H, SSM_P, SSM_G, SSM_N, SSM_CONV = 4, 64, 2, 128, 4
SSM_CONV_DIM = 768
N_IN = 3092

OFF_PX, OFF_GQ, OFF_GK, OFF_GV, OFF_GR = 0, 256, 384, 512, 768
OFF_RQ, OFF_RF, OFF_RI, OFF_RG = 1024, 1280, 1536, 1792
OFF_SZ, OFF_XBC = 2048, 2304
OFF_LR = 3072
OFF_DTX = 3200
W_PROJ = 3456

PT_POOL_SCALE, PT_GBIAS, PT_GNORM, PT_HNORM, PT_CONVB, PT_DTB, PT_ALOG, PT_DSKIP, PT_SNORM = range(9)
PT_CONVW = 9
PT_LBL = 13
PT_ROWS = 16

VMEM_LIMIT_BYTES = 56 * 1024 * 1024
SUBLANES = 8
LANES = 128
HEADS_PER_STEP = LANES // GLA_DV

ROW_TILE = 512
FF_CHUNKS = (0, 768, 1536, 2304, D_FF)
PROMPT_CHUNK = 64
PROMPT_GROUP = 8
SAMPLE_GROUP = 16
SUB_BLOCK = 16
MAX_FACTORED_SPAN = 60.0
POOL_BASE = 24


def _sigmoid(x):
    return 1.0 / (1.0 + jnp.exp(-x))


def _silu(x):
    return x * _sigmoid(x)


def _log_sigmoid(x):
    return jnp.minimum(x, 0.0) - jnp.log(1.0 + jnp.exp(-jnp.abs(x)))


def _softplus(x):
    return jnp.maximum(x, 0.0) + jnp.log(1.0 + jnp.exp(-jnp.abs(x)))


def _rms(x, g):
    ms = jnp.mean(x * x, axis=-1, keepdims=True)
    return x * lax.rsqrt(ms + EPS) * g


def _dot(a, b):
    return jnp.dot(a.astype(BF16), b.astype(BF16), preferred_element_type=F32)


def _dot_nt(a, b):
    return lax.dot_general(a.astype(BF16), b.astype(BF16), (((1,), (1,)), ((), ())),
                           preferred_element_type=F32)


def _dot_tn(a, b):
    return lax.dot_general(a.astype(BF16), b.astype(BF16), (((0,), (0,)), ((), ())),
                           preferred_element_type=F32)


def _split3(x):
    hi = x.astype(BF16)
    r1 = x - hi.astype(F32)
    mid = r1.astype(BF16)
    lo = (r1 - mid.astype(F32)).astype(BF16)
    return hi, mid, lo


def _dot_exact_rhs(a01, x):
    hi, mid, lo = _split3(x)
    a = a01.astype(BF16)
    return (jnp.dot(a, hi, preferred_element_type=F32) + jnp.dot(a, mid, preferred_element_type=F32)
            + jnp.dot(a, lo, preferred_element_type=F32))


def _swiglu_half(x, g, wg_ref, wu_ref, wd_ref):
    h = _rms(x, g).astype(BF16)
    acc = None
    for lo, hi in zip(FF_CHUNKS[:-1], FF_CHUNKS[1:]):
        sl = slice(lo, hi)
        gate = jnp.dot(h, wg_ref[:, sl], preferred_element_type=F32)
        up = jnp.dot(h, wu_ref[:, sl], preferred_element_type=F32)
        act = (_silu(gate) * up).astype(BF16)
        d = jnp.dot(act, wd_ref[sl, :], preferred_element_type=F32)
        acc = d if acc is None else acc + d
    return x + 0.5 * acc


def _pre_kernel(*refs, n_first, split_in, layer):
    if split_in:
        x = jnp.where(pl.program_id(0) < n_first, refs[0][...].reshape(ROW_TILE, D_MODEL), refs[1][...])
        refs = refs[2:]
    else:
        x = refs[0][...]
        refs = refs[1:]
    g1_ref, wg_ref, wu_ref, wd_ref, gm_ref, win_ref, wgate_ref, ptab_ref, x1_ref, feat_ref = refs
    x1 = _swiglu_half(x, g1_ref[...], wg_ref, wu_ref, wd_ref)
    x1_ref[...] = x1
    h = _rms(x1, gm_ref[...]).astype(BF16)
    _mixer_features(h, win_ref, wgate_ref, ptab_ref, feat_ref, layer)


def _hgrn_lower_bound(ptab_ref, layer):
    lbl = ptab_ref[PT_LBL:PT_LBL + DEPTH, 0:C_MIX]
    lexp = jnp.exp(lbl - jnp.max(lbl, axis=0, keepdims=True))
    lsum = jnp.sum(lexp, axis=0, keepdims=True)
    lb = jnp.zeros((1, C_MIX), F32)
    for m in range(1, layer + 1):
        lb = lb + lexp[m:m + 1] / lsum
    return lb


def _mixer_features(h, win_ref, wgate_ref, ptab_ref, feat_ref, layer):
    prow = lambda r, w: ptab_ref[r:r + 1, 0:w]
    proj = lambda lo, hi: jnp.dot(h, win_ref[:, lo:hi], preferred_element_type=F32)
    d = proj(OFF_LR, W_PROJ)
    logit = _dot(d[:, 0:LANES], wgate_ref[...]) + prow(PT_GBIAS, LANES)
    feat_ref[:, OFF_LR:OFF_DTX] = _log_sigmoid(logit) * (1.0 / GLA_TAU)
    feat_ref[:, OFF_DTX:W_PROJ] = _softplus(d[:, LANES:LANES + C_MIX] + prow(PT_DTB, C_MIX))
    b = proj(OFF_RQ, OFF_XBC)
    feat_ref[:, OFF_RQ:OFF_RF] = _silu(b[:, 0:C_MIX])
    lb = _hgrn_lower_bound(ptab_ref, layer)
    log_lb = jnp.log(lb)
    t2 = jnp.log(1.0 - lb) + _log_sigmoid(b[:, C_MIX:2 * C_MIX])
    feat_ref[:, OFF_RF:OFF_RI] = jnp.maximum(log_lb, t2) + jnp.log(1.0 + jnp.exp(-jnp.abs(log_lb - t2)))
    feat_ref[:, OFF_RI:OFF_RG] = b[:, 2 * C_MIX:3 * C_MIX]
    feat_ref[:, OFF_RG:OFF_XBC] = _silu(b[:, 3 * C_MIX:5 * C_MIX])
    a = proj(OFF_PX, OFF_RQ)
    feat_ref[:, OFF_GR:OFF_RQ] = _silu(a[:, OFF_GR:OFF_RQ])
    feat_ref[:, OFF_GQ:OFF_GK] = a[:, OFF_GQ:OFF_GK] * (GLA_DK ** -0.5)
    feat_ref[:, OFF_PX:OFF_GQ] = a[:, OFF_PX:OFF_GQ]
    feat_ref[:, OFF_GK:OFF_GR] = a[:, OFF_GK:OFF_GR]
    feat_ref[:, OFF_XBC:OFF_LR] = proj(OFF_XBC, OFF_LR)


def _segment_rms(o, width, gain):
    n = o.shape[-1]
    seg = (lax.broadcasted_iota(jnp.int32, (n, n), 0) // width
           == lax.broadcasted_iota(jnp.int32, (n, n), 1) // width).astype(BF16)
    sq = o * o
    hi = sq.astype(BF16)
    lo = (sq - hi.astype(F32)).astype(BF16)
    ms = (jnp.dot(hi, seg, preferred_element_type=F32) + jnp.dot(lo, seg, preferred_element_type=F32)) * (1.0 / width)
    return o * lax.rsqrt(ms + EPS) * gain


def _mixer_outputs(cat_ref, ggla_ref, ghg_ref, gz_ref, ptab_ref):
    prow = lambda r, w: ptab_ref[r:r + 1, 0:w]
    o_gla = _segment_rms(cat_ref[:, C_MIX:2 * C_MIX], GLA_DV, prow(PT_GNORM, C_MIX)) * ggla_ref[...]
    o_hg = _segment_rms(cat_ref[:, 2 * C_MIX:3 * C_MIX], HG_DV, prow(PT_HNORM, C_MIX)) * ghg_ref[...]
    y = cat_ref[:, 3 * C_MIX:4 * C_MIX] * gz_ref[...]
    half = C_MIX // SSM_G
    o_ssm = [_rms(y[:, half * g:half * (g + 1)], ptab_ref[PT_SNORM:PT_SNORM + 1, half * g:half * (g + 1)])
             for g in range(SSM_G)]
    return jnp.concatenate([cat_ref[:, 0:C_MIX], o_gla, o_hg] + o_ssm, axis=1)


STAGE_SLOTS = 3
WIDE_CHUNK = 128
TALL_CHUNK = 352
WOUT_CHUNK = 256


def _stage_weight(src, dst, stage, sems, chunk_rows):
    n = src.shape[0] // chunk_rows
    slot = lambda c: stage.at[c % STAGE_SLOTS, pl.ds(0, chunk_rows)]
    copy = lambda c: pltpu.make_async_copy(src.at[pl.ds(c * chunk_rows, chunk_rows)], slot(c), sems.at[c % STAGE_SLOTS])
    for c in range(min(STAGE_SLOTS, n)):
        copy(c).start()
    for c in range(n):
        copy(c).wait()
        dst[c * chunk_rows:(c + 1) * chunk_rows, :] = slot(c)[...].astype(BF16)
        if c + STAGE_SLOTS < n:
            copy(c + STAGE_SLOTS).start()


def _post_kernel(x_ref, cat_ref, ggla_ref, ghg_ref, gz_ref, ptab_ref, wout_hbm, g2_ref, wg_hbm, wu_hbm, wd_hbm,
                 gf_ref, *rest, n_first, final, layer):
    n_out = 2 if final else 1
    o_refs = rest[:n_out]
    wout_ref, wg_ref, wu_ref, wd_ref, wide_stage, tall_stage, wide_sems, tall_sems = rest[n_out:]

    @pl.when(pl.program_id(0) == 0)
    def _():
        _stage_weight(wout_hbm.at[layer], wout_ref, tall_stage, tall_sems, WOUT_CHUNK)
        _stage_weight(wg_hbm.at[layer], wg_ref, wide_stage, wide_sems, WIDE_CHUNK)
        _stage_weight(wu_hbm.at[layer], wu_ref, wide_stage, wide_sems, WIDE_CHUNK)
        _stage_weight(wd_hbm.at[layer], wd_ref, tall_stage, tall_sems, TALL_CHUNK)

    mixed = _mixer_outputs(cat_ref, ggla_ref, ghg_ref, gz_ref, ptab_ref)
    x2 = x_ref[...] + jnp.dot(mixed.astype(BF16), wout_ref[...], preferred_element_type=F32)
    x3 = _swiglu_half(x2, g2_ref[...], wg_ref, wu_ref, wd_ref)
    if final:
        y = _rms(x3, gf_ref[...])
        i = pl.program_id(0)

        @pl.when(i < n_first)
        def _():
            o_refs[0][...] = y.reshape(o_refs[0].shape)

        @pl.when(i >= n_first)
        def _():
            o_refs[1][...] = y
    else:
        o_refs[0][...] = x3


def _layer_spec(shape, layer):
    nd = len(shape)
    return pl.BlockSpec((None,) + tuple(shape), lambda *_: (layer,) + (0,) * nd, pipeline_mode=pl.Buffered(1))


def _dense_params():
    return pltpu.CompilerParams(dimension_semantics=("arbitrary",), vmem_limit_bytes=VMEM_LIMIT_BYTES)


def _row_spec(width):
    return pl.BlockSpec((ROW_TILE, width), lambda i: (i, 0))


def _token_specs(n_first, n_prompt_seq):
    return [pl.BlockSpec((n_prompt_seq, PROMPT_CHUNK, D_MODEL), lambda i: (0, jnp.minimum(i, n_first - 1), 0)),
            pl.BlockSpec((ROW_TILE, D_MODEL), lambda i: (jnp.maximum(i - n_first, 0), 0))]


def _pre_call(xs, g1, wg, wu, wd, gm, win, wgate, ptab, layer, rows, n_first):
    split_in = len(xs) == 2
    x_specs = _token_specs(n_first, xs[0].shape[0]) if split_in else [_row_spec(D_MODEL)]
    return pl.pallas_call(
        functools.partial(_pre_kernel, n_first=n_first, split_in=split_in, layer=layer),
        grid=(rows // ROW_TILE,),
        in_specs=x_specs + [_layer_spec((1, D_MODEL), layer), _layer_spec((D_MODEL, D_FF), layer),
                            _layer_spec((D_MODEL, D_FF), layer), _layer_spec((D_FF, D_MODEL), layer),
                            _layer_spec((1, D_MODEL), layer),
                            pl.BlockSpec((D_MODEL, W_PROJ), lambda i: (0, 0), pipeline_mode=pl.Buffered(1)),
                            _layer_spec(wgate.shape[1:], layer), _layer_spec(ptab.shape[1:], layer)],
        out_specs=[_row_spec(D_MODEL), _row_spec(W_PROJ)],
        out_shape=[jax.ShapeDtypeStruct((rows, D_MODEL), F32), jax.ShapeDtypeStruct((rows, W_PROJ), F32)],
        compiler_params=_dense_params(),
        name="pre_ffn_inproj",
    )(*xs, g1, wg, wu, wd, gm, win, wgate, ptab)


def _post_call(x, cat, feat, ptab, wout, g2, wg, wu, wd, gf, layer, n_first, final_shapes):
    rows = x.shape[0]
    if final_shapes is not None:
        out_specs = _token_specs(n_first, final_shapes[0][0])
        out_shape = [jax.ShapeDtypeStruct(s, F32) for s in final_shapes]
    else:
        out_specs = [_row_spec(D_MODEL)]
        out_shape = [jax.ShapeDtypeStruct((rows, D_MODEL), F32)]
    gate_spec = lambda off: pl.BlockSpec((ROW_TILE, C_MIX), lambda i: (i, off // C_MIX))
    hbm = pl.BlockSpec(memory_space=pl.ANY)
    return pl.pallas_call(
        functools.partial(_post_kernel, n_first=n_first, final=final_shapes is not None, layer=layer),
        grid=(rows // ROW_TILE,),
        in_specs=[_row_spec(D_MODEL), _row_spec(D_MODEL), gate_spec(OFF_GR), gate_spec(OFF_RG), gate_spec(OFF_SZ),
                  _layer_spec(ptab.shape[1:], layer), hbm, _layer_spec((1, D_MODEL), layer), hbm, hbm, hbm,
                  pl.BlockSpec((1, D_MODEL), lambda i: (0, 0))],
        out_specs=out_specs,
        out_shape=out_shape,
        scratch_shapes=[pltpu.VMEM((D_MODEL, D_MODEL), BF16), pltpu.VMEM((D_MODEL, D_FF), BF16),
                        pltpu.VMEM((D_MODEL, D_FF), BF16), pltpu.VMEM((D_FF, D_MODEL), BF16),
                        pltpu.VMEM((STAGE_SLOTS, WIDE_CHUNK, D_FF), F32),
                        pltpu.VMEM((STAGE_SLOTS, TALL_CHUNK, D_MODEL), F32),
                        pltpu.SemaphoreType.DMA((STAGE_SLOTS,)), pltpu.SemaphoreType.DMA((STAGE_SLOTS,))],
        compiler_params=_dense_params(),
        name="post_outproj_ffn",
    )(x, cat, feat, feat, feat, ptab, wout, g2, wg, wu, wd, gf)


def _lane_group(shape, width):
    return lax.broadcasted_iota(jnp.int32, shape, len(shape) - 1) // width


def _block_diag_mask(rows, row_w, cols, col_w):
    r = lax.broadcasted_iota(jnp.int32, (rows, cols), 0) // row_w
    c = lax.broadcasted_iota(jnp.int32, (rows, cols), 1) // col_w
    return r == c


def _tril(n):
    r = lax.broadcasted_iota(jnp.int32, (n, n), 0)
    c = lax.broadcasted_iota(jnp.int32, (n, n), 1)
    return r >= c


def _select_heads(stacked, n_heads, rows, head_w):
    grp = _lane_group((rows, n_heads * head_w), head_w)
    out = jnp.zeros((rows, n_heads * head_w), F32)
    for h in range(n_heads):
        out = jnp.where(grp == h, stacked[h * rows:(h + 1) * rows], out)
    return out


def _cum_log_decay(la):
    return _dot_exact_rhs(_tril(la.shape[0]).astype(F32), la)


def _decay_span(b):
    mid = b.shape[0] // 2 - 1
    dev = jnp.abs(b - b[mid:mid + 1, :])
    dev = functools.reduce(jnp.maximum, [dev[:, LANES * j:LANES * (j + 1)] for j in range(dev.shape[1] // LANES)])
    return jnp.max(dev.reshape(-1, SUBLANES, LANES), axis=0)


def _la_factored(q, k, v, b, st, n_heads, dk):
    chunk = q.shape[0]
    nk, nv = n_heads * dk, n_heads * 64
    mid = chunk // 2 - 1
    m = b[mid:mid + 1, :]
    qt = q * jnp.exp(b - m)
    kt = k * jnp.exp(m - b)
    k_bd = jnp.where(_block_diag_mask(n_heads * chunk, chunk, nk, dk), jnp.concatenate([kt] * n_heads, axis=0), 0.0)
    att = _dot_nt(qt, k_bd)
    row = lax.broadcasted_iota(jnp.int32, att.shape, 0)
    col = lax.broadcasted_iota(jnp.int32, att.shape, 1) % chunk
    att = jnp.where(row >= col, att, 0.0)
    v_bd = jnp.where(_block_diag_mask(n_heads * chunk, chunk, nv, 64), jnp.concatenate([v] * n_heads, axis=0), 0.0)
    o = _dot(att, v_bd) + _dot_nt(qt * jnp.exp(m), st)
    b_last = b[chunk - 1:chunk, :]
    upd = _dot_tn(v, kt * jnp.exp(b_last - m))
    st_new = st * jnp.exp(b_last) + jnp.where(_block_diag_mask(nv, 64, nk, dk), upd, 0.0)
    return o, st_new


def _la_direct(q, k, v, b, st, n_heads, dk, sub):
    chunk = q.shape[0]
    nk = n_heads * dk
    nv = n_heads * 64
    n_sub = chunk // sub
    refs = [None] + [b[sub * i - 1:sub * i, :] for i in range(1, n_sub)]
    c_loc = jnp.concatenate([b[0:sub]] + [b[sub * i:sub * (i + 1)] - refs[i] for i in range(1, n_sub)], axis=0) \
        if n_sub > 1 else b
    o = _dot_nt(q * jnp.exp(b), st)

    qt = q * jnp.exp(c_loc)
    kgrp = _lane_group((sub, nk), dk)
    outs = []
    for i in range(n_sub):
        rows = slice(sub * i, sub * (i + 1))
        o_i = o[rows]
        if i > 0:
            prev = slice(0, sub * i)
            kt = k[prev] * jnp.exp(refs[i] - b[prev])
            q_stack = jnp.concatenate([jnp.where(kgrp == h, qt[rows], 0.0) for h in range(n_heads)], axis=0)
            att = _dot_nt(q_stack, kt)
            o_i = o_i + _select_heads(_dot(att, v[prev]), n_heads, sub, 64)
        c_i = c_loc[rows]
        q_i = q[rows]
        row_id = lax.broadcasted_iota(jnp.int32, (sub, nk), 0)
        e_list = []
        for j in range(sub):
            c_j = c_loc[sub * i + j:sub * i + j + 1, :]
            k_j = k[sub * i + j:sub * i + j + 1, :]
            dec = jnp.exp(jnp.where(row_id >= j, c_i - c_j, -jnp.inf))
            e_list.append(dec * q_i * k_j)
        e_all = jnp.concatenate(e_list, axis=0)
        expand = _block_diag_mask(nk, dk, nv, 64).astype(F32)
        r_d = _dot(e_all, expand)
        for j in range(sub):
            o_i = o_i + r_d[j * sub:(j + 1) * sub] * v[sub * i + j:sub * i + j + 1, :]
        outs.append(o_i)
    o = jnp.concatenate(outs, axis=0) if n_sub > 1 else outs[0]

    b_last = b[chunk - 1:chunk, :]
    k_end = k * jnp.exp(b_last - b)
    upd = _dot_tn(v, k_end)
    st_new = st * jnp.exp(b_last) + jnp.where(_block_diag_mask(nv, 64, nk, dk), upd, 0.0)
    return o, st_new


def _load_state_t(s, n_heads, dk):
    st = s.T
    tiled = jnp.concatenate([st] * n_heads, axis=0)
    return jnp.where(_block_diag_mask(n_heads * 64, 64, n_heads * dk, dk), tiled, 0.0)


def _store_state_t(st, n_heads, dk):
    acc = st[0:64]
    for h in range(1, n_heads):
        acc = acc + st[64 * h:64 * (h + 1)]
    return acc.T


def _pool_mixer(proj_ref, pool_scr, poolw_ref, scale, chunk, pos_first):
    xp = proj_ref[:, OFF_PX:OFF_PX + C_MIX]
    pool_scr[0, POOL_BASE:POOL_BASE + chunk, :] = xp
    n_ext = POOL_BASE - SUBLANES + chunk
    for s, shift in enumerate((1, 2, 4)):
        cur = pool_scr[s, SUBLANES:SUBLANES + n_ext, :]
        prev = pool_scr[s, SUBLANES - shift:SUBLANES - shift + n_ext, :]
        pool_scr[s + 1, SUBLANES:SUBLANES + n_ext, :] = cur + prev
    s2 = pool_scr[1, POOL_BASE:POOL_BASE + chunk, :]
    s4 = pool_scr[2, POOL_BASE:POOL_BASE + chunk, :]
    s8 = pool_scr[3, POOL_BASE:POOL_BASE + chunk, :]
    s16 = s8 + pool_scr[3, POOL_BASE - 8:POOL_BASE - 8 + chunk, :]
    grp = _lane_group((chunk, C_MIX), POOL_GC)
    win_sum = jnp.where(grp == 0, s2, jnp.where(grp == 1, s4, jnp.where(grp == 2, s8, s16)))
    win = jnp.where(grp == 0, 2, jnp.where(grp == 1, 4, jnp.where(grp == 2, 8, 16)))
    pos = pos_first + lax.broadcasted_iota(jnp.int32, (chunk, C_MIX), 0)
    cnt = jnp.minimum(pos + 1, win).astype(F32)
    d = win_sum / cnt - xp
    o_pool = _dot(d, poolw_ref[...]) * scale
    new_pool = pool_scr[0, POOL_BASE + chunk - POOL_PAST:POOL_BASE + chunk, :]
    pool_scr[0, POOL_BASE - POOL_PAST:POOL_BASE, :] = new_pool
    return o_pool, new_pool


def _ssd_conv(proj_ref, conv_scr, prow, chunk):
    conv_base = SUBLANES
    conv_scr[conv_base:conv_base + chunk, :] = proj_ref[:, OFF_XBC:OFF_XBC + SSM_CONV_DIM]
    conv = None
    for w in range(SSM_CONV):
        start = conv_base - (SSM_CONV - 1) + w
        term = conv_scr[start:start + chunk, :] * prow(PT_CONVW + w, SSM_CONV_DIM)
        conv = term if conv is None else conv + term
    conv = _silu(conv + prow(PT_CONVB, SSM_CONV_DIM))
    new_conv = conv_scr[conv_base + chunk - (SSM_CONV - 1):conv_base + chunk, :]
    conv_scr[conv_base - (SSM_CONV - 1):conv_base, :] = new_conv
    return conv, new_conv


def _ssd_scan(proj_ref, conv, ssm_st, prow, chunk):
    xs = conv[:, 0:C_MIX]
    dt = proj_ref[:, OFF_DTX:OFF_DTX + C_MIX]
    a_neg = -jnp.exp(prow(PT_ALOG, C_MIX))
    la = dt * a_neg
    xdt = xs * dt
    cum = _cum_log_decay(la)
    heads_per_group = SSM_H // SSM_G
    b_of = lambda g: conv[:, C_MIX + SSM_N * g:C_MIX + SSM_N * (g + 1)]
    c_of = lambda g: conv[:, C_MIX + SSM_G * SSM_N + SSM_N * g:C_MIX + SSM_G * SSM_N + SSM_N * (g + 1)]
    if chunk == SSM_P:
        row = lax.broadcasted_iota(jnp.int32, (chunk, C_MIX), 0)
        col = lax.broadcasted_iota(jnp.int32, (chunk, C_MIX), 1) % chunk
        cum_j = jnp.sum(jnp.where(row == col, cum, 0.0), axis=0, keepdims=True)
        dec = jnp.exp(jnp.where(row >= col, cum - cum_j, -jnp.inf))
        cb = jnp.concatenate([_dot_nt(c_of(g), jnp.concatenate([b_of(g)] * heads_per_group, axis=0))
                              for g in range(SSM_G)], axis=1)
        x_bd = jnp.where(_block_diag_mask(SSM_H * chunk, chunk, C_MIX, SSM_P),
                         jnp.concatenate([xdt] * SSM_H, axis=0), 0.0)
        y = _dot(cb * dec, x_bd)
    else:
        cum_t = cum.T
        causal = _tril(chunk)
        att_rows = []
        for h in range(SSM_H):
            if h % heads_per_group == 0:
                cb = _dot_nt(c_of(h // heads_per_group), b_of(h // heads_per_group))
            col = cum[:, SSM_P * h:SSM_P * h + 1]
            row = cum_t[SSM_P * h:SSM_P * h + 1, :]
            att_rows.append(cb * jnp.exp(jnp.where(causal, col - row, -jnp.inf)))
        y = _select_heads(_dot(jnp.concatenate(att_rows, axis=0), xdt), SSM_H, chunk, SSM_P)
    cum_last = cum[chunk - 1:chunk, :]
    decay_in = jnp.exp(cum)
    xw = xdt * jnp.exp(cum_last - cum)
    decay_state = jnp.exp(cum_last)
    y_state = []
    for g in range(SSM_G):
        lanes = slice(SSM_N * g, SSM_N * (g + 1))
        st = ssm_st[g]
        y_state.append(_dot(c_of(g), st))
        ssm_st[g] = st * decay_state[:, lanes] + _dot_tn(b_of(g), xw[:, lanes])
    return y + jnp.concatenate(y_state, axis=1) * decay_in + prow(PT_DSKIP, C_MIX) * xs


def _mixer_kernel(*refs, chunk, group, pos0, layer, carried, n_alias, lin):
    feat_all = refs[0]
    n_state = 5 if lin else 3
    n_in = 1 + (n_state if carried else 0)
    poolw_ref, ptab_ref = refs[n_in:n_in + 2]
    n_in += 2 + n_alias
    cat_all = refs[n_in]
    new_states = refs[n_in + 1:n_in + 1 + n_state]
    scratch = refs[n_in + 1 + n_state:]
    if lin:
        npool_ref, ngla_ref, nhg_ref, nssm_ref, nconv_ref = new_states
        pool_scr, conv_scr, gla_st, hg_st, ssm_st = scratch
    else:
        npool_ref, nssm_ref, nconv_ref = new_states
        pool_scr, conv_scr, ssm_st = scratch
    seq_state = (lambda ref, s: ref.at[s]) if lin else (lambda ref, s: ref.at[:, s])
    prow = lambda r, w: ptab_ref[r:r + 1, 0:w]
    c = pl.program_id(1)
    n_c = pl.num_programs(1)
    sub = min(SUB_BLOCK, chunk)
    conv_base = SUBLANES

    @pl.when(c == 0)
    def _():
        pool_scr[...] = jnp.zeros(pool_scr.shape, F32)
        conv_scr[...] = jnp.zeros(conv_scr.shape, F32)
        if carried:
            carried_refs = refs[1:1 + n_state]
            poolb_ref, ssms_ref, convb_ref = carried_refs[0], carried_refs[-2], carried_refs[-1]
            for s in range(group):
                pool_scr[s, 0, POOL_BASE - POOL_PAST:POOL_BASE, :] = seq_state(poolb_ref, s)[...]
                conv_scr[s, conv_base - (SSM_CONV - 1):conv_base, :] = seq_state(convb_ref, s)[...]
                for g in range(SSM_G):
                    ssm_st[s, g] = ssms_ref[s, g].T
                if lin:
                    gla_st[s] = _load_state_t(carried_refs[1][s], GLA_H, GLA_DK)
                    hg_st[s] = _load_state_t(carried_refs[2][s], HG_H, HG_DK)
        else:
            ssm_st[...] = jnp.zeros(ssm_st.shape, F32)
            if lin:
                gla_st[...] = jnp.zeros(gla_st.shape, F32)
                hg_st[...] = jnp.zeros(hg_st.shape, F32)

    seq_rows = lambda ref, s: ref.at[pl.ds(s * chunk, chunk)]
    new_pools, new_convs, convs, cum_decays, spans = [], [], [], [], []

    def conv_stage(s):
        conv, new_conv = _ssd_conv(seq_rows(feat_all, s), conv_scr.at[s], prow, chunk)
        convs.append(conv)
        new_convs.append(new_conv)

    def decay_stage(s):
        feat_ref = seq_rows(feat_all, s)
        b_gla = _cum_log_decay(feat_ref[:, OFF_LR:OFF_LR + GLA_H * GLA_DK])
        b_hg = _cum_log_decay(feat_ref[:, OFF_RF:OFF_RF + C_MIX])
        cum_decays.extend([b_gla, b_hg])
        spans.extend([_decay_span(b_gla), _decay_span(b_hg)])

    def scan_stage(s):
        seq_rows(cat_all, s)[:, 3 * C_MIX:4 * C_MIX] = _ssd_scan(seq_rows(feat_all, s), convs[s], ssm_st.at[s], prow,
                                                                chunk)

    def pool_stage(s):
        o_pool, new_pool = _pool_mixer(seq_rows(feat_all, s), pool_scr.at[s], poolw_ref, prow(PT_POOL_SCALE, C_MIX),
                                       chunk, pos0 + c * chunk)
        seq_rows(cat_all, s)[:, 0:C_MIX] = o_pool
        new_pools.append(new_pool)

    stages = (decay_stage, conv_stage, scan_stage, pool_stage) if lin else (conv_stage, scan_stage, pool_stage)
    if chunk >= PROMPT_CHUNK:
        for stage in stages:
            for s in range(group):
                stage(s)
    else:
        for s in range(group):
            for stage in stages:
                stage(s)

    if lin:
        _linear_attention_step(feat_all, cat_all, gla_st, hg_st, cum_decays, spans, seq_rows, group, sub)
    else:
        for s in range(group):
            seq_rows(cat_all, s)[:, C_MIX:3 * C_MIX] = jnp.zeros((chunk, 2 * C_MIX), F32)

    @pl.when(c == n_c - 1)
    def _():
        for s in range(group):
            seq_state(npool_ref, s)[...] = new_pools[s]
            seq_state(nconv_ref, s)[...] = new_convs[s]
            for g in range(SSM_G):
                nssm_ref[s, g] = ssm_st[s, g].T
            if lin:
                ngla_ref[s] = _store_state_t(gla_st[s], GLA_H, GLA_DK)
                nhg_ref[s] = _store_state_t(hg_st[s], HG_H, HG_DK)


def _linear_attention_step(feat_all, cat_all, gla_st, hg_st, cum_decays, spans, seq_rows, group, sub):
    narrow = jnp.max(functools.reduce(jnp.maximum, spans)) <= MAX_FACTORED_SPAN

    def run(la_gla, la_hg, *cums):
        for s in range(group):
            feat_ref, cat_ref = seq_rows(feat_all, s), seq_rows(cat_all, s)
            o_g, st_g = la_gla(feat_ref[:, OFF_GQ:OFF_GK], feat_ref[:, OFF_GK:OFF_GV], feat_ref[:, OFF_GV:OFF_GR],
                               cums[2 * s], gla_st[s])
            gla_st[s] = st_g
            cat_ref[:, C_MIX:2 * C_MIX] = o_g
            hk = 1.0 - jnp.exp(feat_ref[:, OFF_RF:OFF_RI])
            o_h, st_h = la_hg(feat_ref[:, OFF_RQ:OFF_RF], hk, feat_ref[:, OFF_RI:OFF_RG], cums[2 * s + 1], hg_st[s])
            hg_st[s] = st_h
            cat_ref[:, 2 * C_MIX:3 * C_MIX] = o_h

    factored = functools.partial(run, functools.partial(_la_factored, n_heads=GLA_H, dk=GLA_DK),
                                 functools.partial(_la_factored, n_heads=HG_H, dk=HG_DK))
    direct = functools.partial(run, functools.partial(_la_direct, n_heads=GLA_H, dk=GLA_DK, sub=sub),
                               functools.partial(_la_direct, n_heads=HG_H, dk=HG_DK, sub=sub))
    lax.cond(narrow, factored, direct, *cum_decays)


def _mixer_call(proj, cat_prev, carried, prev_states, mats, *, n_seq, seq_len, row0, chunk_major, chunk, group,
                pos0, layer, lin):
    rows = proj.shape[0]
    n_c, n_b, blk = seq_len // chunk, n_seq // group, group * chunk
    assert chunk_major or n_c == 1
    row_map = lambda b, c: (row0 // blk + c * n_b + b, 0)

    def seq_major(tail):
        nd = len(tail)
        return (DEPTH, n_seq) + tail, pl.BlockSpec((None, group) + tail, lambda b, c: (layer, b) + (0,) * nd)

    def time_major(n_rows, width):
        return (DEPTH, n_rows, n_seq, width), pl.BlockSpec((None, n_rows, group, width), lambda b, c: (layer, 0, b, 0))

    ssm = seq_major((SSM_G, 128, SSM_N))
    if lin:
        states = [seq_major((POOL_PAST, C_MIX)), seq_major((GLA_H * GLA_DK, GLA_DV)), seq_major((HG_H * HG_DK, HG_DV)),
                  ssm, seq_major((SSM_CONV - 1, SSM_CONV_DIM))]
    else:
        states = [time_major(POOL_PAST, C_MIX), ssm, time_major(SSM_CONV - 1, SSM_CONV_DIM)]

    def mat_spec(m):
        nd = m.ndim - 1
        return pl.BlockSpec((None,) + m.shape[1:], lambda b, c: (layer,) + (0,) * nd)

    state_specs = [spec for _, spec in states]
    in_specs = [pl.BlockSpec((blk, W_PROJ), row_map)]
    args = [proj]
    if carried is not None:
        in_specs += state_specs
        args += list(carried)
    in_specs += [mat_spec(m) for m in mats]
    args += list(mats)
    aliases = {}
    if cat_prev is not None:
        aliases[len(args)] = 0
        args.append(cat_prev)
    if prev_states is not None:
        for i, s in enumerate(prev_states):
            aliases[len(args)] = 1 + i
            args.append(s)
    in_specs += [pl.BlockSpec(memory_space=pl.ANY)] * len(aliases)
    out_shape = [jax.ShapeDtypeStruct((rows, D_MODEL), F32)] + [jax.ShapeDtypeStruct(shape, F32) for shape, _ in states]
    out_specs = [pl.BlockSpec((blk, D_MODEL), row_map)] + state_specs
    scratch = [pltpu.VMEM((group, 4, POOL_BASE + chunk, C_MIX), F32),
               pltpu.VMEM((group, SUBLANES + chunk, SSM_CONV_DIM), F32)]
    if lin:
        scratch += [pltpu.VMEM((group, GLA_H * GLA_DV, GLA_H * GLA_DK), F32),
                    pltpu.VMEM((group, HG_H * HG_DV, HG_H * HG_DK), F32)]
    scratch.append(pltpu.VMEM((group, SSM_G, SSM_N, 128), F32))
    return pl.pallas_call(
        functools.partial(_mixer_kernel, chunk=chunk, group=group, pos0=pos0, layer=layer,
                          carried=carried is not None, n_alias=len(aliases), lin=lin),
        grid=(n_b, n_c),
        in_specs=in_specs,
        out_specs=out_specs,
        out_shape=out_shape,
        scratch_shapes=scratch,
        input_output_aliases=aliases,
        compiler_params=pltpu.CompilerParams(dimension_semantics=("arbitrary", "arbitrary"),
                                             vmem_limit_bytes=VMEM_LIMIT_BYTES),
        name=f"mixer_chunk{chunk}",
    )(*args)


def _batched_recurrence_kernel(*refs, n_seq, seq_len, dk, own_keys, n_alias):
    n_in = 5 if own_keys else 4
    q_ref, v_ref, la_ref, s_in = refs[0], refs[n_in - 3], refs[n_in - 2], refs[n_in - 1]
    o_ref, s_out, q_t, k_t, a_t, v_t = refs[n_in + n_alias:]
    key_w = HEADS_PER_STEP * dk
    base = 0 if key_w == q_ref.shape[1] else pl.multiple_of(pl.program_id(0) * key_w, key_w)
    for t in range(seq_len):
        s_prev = s_in if t == 0 else s_out
        rows = pl.ds(t, n_seq, stride=seq_len)
        a = jnp.exp(la_ref[rows, :])
        k = refs[1][rows, :] if own_keys else 1.0 - a
        q_t[...] = q_ref[rows, :].T
        k_t[...] = k.T
        a_t[...] = a.T
        v_t[...] = v_ref[rows, :].T
        heads = []
        for h in range(HEADS_PER_STEP):
            v_h = v_t[GLA_DV * h:GLA_DV * (h + 1), :]

            def body(i, acc, h=h, v_h=v_h, s_prev=s_prev):
                r = h * dk + i
                s = a_t[pl.ds(base + r, 1), :] * s_prev[r] + k_t[pl.ds(base + r, 1), :] * v_h
                s_out[r] = s
                return acc + q_t[pl.ds(base + r, 1), :] * s

            heads.append(lax.fori_loop(0, dk, body, jnp.zeros((GLA_DV, n_seq), F32), unroll=8))
        o_ref[rows, :] = jnp.concatenate(heads, axis=0).T


def _batched_recurrence_call(feat, cat, state_t, prev_state, *, layer, row0, n_seq, seq_len, q_off, k_off, v_off,
                             la_off, out_off, n_heads, dk):
    n_rows = n_seq * seq_len
    rb = row0 // n_rows
    key_w = HEADS_PER_STEP * dk
    assert key_w in (LANES // 2, LANES) and n_heads % HEADS_PER_STEP == 0

    def lanes(off, per_step):
        return pl.BlockSpec((n_rows, LANES), lambda i: (rb, off // LANES + (i if per_step else 0)))

    key = lambda off: lanes(off, key_w == LANES)
    in_specs = [key(q_off)] + ([key(k_off)] if k_off is not None else []) + [lanes(v_off, True), key(la_off)]
    args = [feat] * len(in_specs)
    state_spec = pl.BlockSpec((None, key_w, GLA_DV, n_seq), lambda i: (layer, i, 0, 0))
    in_specs.append(state_spec)
    args.append(state_t)
    aliases = {len(args): 0}
    args.append(cat)
    if prev_state is not None:
        aliases[len(args)] = 1
        args.append(prev_state)
    in_specs += [pl.BlockSpec(memory_space=pl.ANY)] * len(aliases)
    return pl.pallas_call(
        functools.partial(_batched_recurrence_kernel, n_seq=n_seq, seq_len=seq_len, dk=dk,
                          own_keys=k_off is not None, n_alias=len(aliases)),
        grid=(n_heads // HEADS_PER_STEP,),
        in_specs=in_specs,
        out_specs=[lanes(out_off, True), state_spec],
        out_shape=[jax.ShapeDtypeStruct(cat.shape, F32), jax.ShapeDtypeStruct(state_t.shape, F32)],
        scratch_shapes=[pltpu.VMEM((LANES, n_seq), F32)] * 4,
        input_output_aliases=aliases,
        compiler_params=pltpu.CompilerParams(dimension_semantics=("arbitrary",), vmem_limit_bytes=VMEM_LIMIT_BYTES),
        name=f"batched_recurrence_dk{dk}",
    )(*args)


LR_SRC = 1024
DT_SRC = N_IN - SSM_H


def _arrange_w_in_kernel(wt_ref, *o_refs):
    def put(col, slab):
        for layer, o_ref in enumerate(o_refs):
            o_ref[:, col:col + LANES] = slab[:, layer, :].T.astype(BF16)

    for j in range(LR_SRC // LANES):
        put(LANES * j, wt_ref[LANES * j:LANES * (j + 1)])
    for j in range((OFF_LR - LR_SRC) // LANES):
        src = LR_SRC + GLA_RANK + LANES * j
        put(LR_SRC + LANES * j, wt_ref[src:src + LANES])
    row = lax.broadcasted_iota(jnp.int32, (LANES, DEPTH, D_MODEL), 0)
    dt_row = lambda h: jnp.broadcast_to(wt_ref[DT_SRC + h:DT_SRC + h + 1], (LANES, DEPTH, D_MODEL))
    tail = jnp.where(row < GLA_RANK, wt_ref[LR_SRC:LR_SRC + LANES], 0.0)
    for h in range(SSM_H):
        tail = jnp.where(row == GLA_RANK + h, dt_row(h), tail)
    put(OFF_LR, tail)
    for j in range(C_MIX // LANES):
        rep = jnp.zeros((LANES, DEPTH, D_MODEL), F32)
        for h in range(HEADS_PER_STEP * j, HEADS_PER_STEP * (j + 1)):
            rep = jnp.where(row // SSM_P == h - HEADS_PER_STEP * j, dt_row(h), rep)
        put(OFF_DTX + LANES * j, rep)


def _arrange_w_in(w):
    wt = jnp.transpose(w, (2, 0, 1))
    return pl.pallas_call(
        _arrange_w_in_kernel,
        grid=(1,),
        in_specs=[pl.BlockSpec(wt.shape, lambda i: (0, 0, 0), pipeline_mode=pl.Buffered(1))],
        out_specs=[pl.BlockSpec((D_MODEL, W_PROJ), lambda i: (0, 0))] * DEPTH,
        out_shape=[jax.ShapeDtypeStruct((D_MODEL, W_PROJ), BF16)] * DEPTH,
        compiler_params=pltpu.CompilerParams(dimension_semantics=("arbitrary",), vmem_limit_bytes=VMEM_LIMIT_BYTES),
        name="arrange_w_in",
    )(wt)


def _pool_block_diag(pw):
    eye = jnp.eye(len(POOL_WINDOWS), dtype=pw.dtype)
    return jnp.einsum('lgcd,gh->lgchd', pw, eye).reshape(pw.shape[0], C_MIX, C_MIX)


def _param_table(pool_scale, gla_gate_bias, gla_norm, hgrn_norm, ssm_conv_b, ssm_dt_bias, ssm_A_log, ssm_D,
                 ssm_norm, ssm_conv_w, hgrn_lb_logits):
    def rows(v):
        v = v.astype(F32).reshape(DEPTH, -1, v.shape[-1])
        return jnp.pad(v, ((0, 0), (0, 0), (0, SSM_CONV_DIM - v.shape[-1])))
    lbl = jnp.broadcast_to(hgrn_lb_logits[None], (DEPTH,) + hgrn_lb_logits.shape)
    parts = [rows(pool_scale), rows(gla_gate_bias), rows(jnp.tile(gla_norm, (1, GLA_H))),
             rows(jnp.tile(hgrn_norm, (1, HG_H))), rows(ssm_conv_b), rows(jnp.repeat(ssm_dt_bias, SSM_P, axis=-1)),
             rows(jnp.repeat(ssm_A_log, SSM_P, axis=-1)), rows(jnp.repeat(ssm_D, SSM_P, axis=-1)), rows(ssm_norm),
             rows(ssm_conv_w), rows(lbl)]
    tab = jnp.concatenate(parts, axis=1)
    return jnp.pad(tab, ((0, 0), (0, PT_ROWS - tab.shape[1]), (0, 0)))


def kernel(x_prompt, x_sample, state_pool, state_gla, state_hgrn, state_ssm, state_conv, ffn1_norm, ffn1_w_gate, ffn1_w_up, ffn1_w_down, mix_norm, w_in, pool_w, pool_scale, gla_w_gate, gla_gate_bias, gla_norm, hgrn_lb_logits, hgrn_norm, ssm_conv_w, ssm_conv_b, ssm_dt_bias, ssm_A_log, ssm_D, ssm_norm, w_out, ffn2_norm, ffn2_w_gate, ffn2_w_up, ffn2_w_down, final_norm):
    bp, tp, _ = x_prompt.shape
    bs, ts, _ = x_sample.shape
    rows_p, rows_s = bp * tp, bs * ts
    gain = lambda v: v.reshape(DEPTH, 1, D_MODEL).astype(F32)
    bf = lambda w: w.astype(BF16)

    wgate = bf(jnp.pad(gla_w_gate, ((0, 0), (0, 128 - GLA_RANK), (0, 0))))
    ptab = _param_table(pool_scale, gla_gate_bias, gla_norm, hgrn_norm, ssm_conv_b, ssm_dt_bias, ssm_A_log, ssm_D,
                        ssm_norm, ssm_conv_w, hgrn_lb_logits)
    mats = [bf(_pool_block_diag(pool_w)), ptab]
    seq_last = lambda s: jnp.transpose(s, (0, 2, 3, 4, 1)).reshape(DEPTH, -1, s.shape[-1], bs)
    carried = (jnp.transpose(state_pool, (0, 2, 1, 3)), state_ssm.reshape(DEPTH, bs, SSM_G, 128, SSM_N),
               jnp.transpose(state_conv, (0, 2, 1, 3)))
    gla_t, hg_t = seq_last(state_gla), seq_last(state_hgrn)
    ffn1 = (gain(ffn1_norm), bf(ffn1_w_gate), bf(ffn1_w_up), bf(ffn1_w_down))
    ffn2 = (gain(ffn2_norm), ffn2_w_gate, ffn2_w_up, ffn2_w_down)
    gm, win, wout = gain(mix_norm), _arrange_w_in(w_in), w_out
    gf = final_norm.reshape(1, D_MODEL).astype(F32)

    assert bp * PROMPT_CHUNK == ROW_TILE and rows_s % ROW_TILE == 0
    n_first = rows_p // ROW_TILE
    xs = [x_prompt, x_sample.reshape(rows_s, D_MODEL)]
    states_p = states_s = new_gla_t = new_hg_t = None
    for l in range(DEPTH):
        x1, feat = _pre_call(xs, *ffn1, gm, win[l], wgate, ptab, l, rows_p + rows_s, n_first)
        outs_p = _mixer_call(feat, None, None, states_p, mats, n_seq=bp, seq_len=tp, row0=0, chunk_major=True,
                             chunk=PROMPT_CHUNK, group=PROMPT_GROUP, pos0=0, layer=l, lin=True)
        outs_s = _mixer_call(feat, outs_p[0], carried, states_s, mats, n_seq=bs, seq_len=ts, row0=rows_p,
                             chunk_major=False, chunk=ts, group=SAMPLE_GROUP, pos0=PAST_LEN, layer=l, lin=False)
        states_p, states_s = outs_p[1:], outs_s[1:]
        sample = dict(layer=l, row0=rows_p, n_seq=bs, seq_len=ts)
        cat, new_gla_t = _batched_recurrence_call(feat, outs_s[0], gla_t, new_gla_t, q_off=OFF_GQ, k_off=OFF_GK,
                                                  v_off=OFF_GV, la_off=OFF_LR, out_off=C_MIX, n_heads=GLA_H,
                                                  dk=GLA_DK, **sample)
        cat, new_hg_t = _batched_recurrence_call(feat, cat, hg_t, new_hg_t, q_off=OFF_RQ, k_off=None, v_off=OFF_RI,
                                                 la_off=OFF_RF, out_off=2 * C_MIX, n_heads=HG_H, dk=HG_DK, **sample)
        final_shapes = ((bp, tp, D_MODEL), (rows_s, D_MODEL)) if l == DEPTH - 1 else None
        xs = _post_call(x1, cat, feat, ptab, wout, *ffn2, gf, l, n_first, final_shapes)

    seq_first = lambda s, h, dk: jnp.transpose(s.reshape(DEPTH, h, dk, s.shape[-2], bs), (0, 4, 1, 2, 3))
    time_minor = lambda s: jnp.transpose(s, (0, 2, 1, 3))
    pool_p, gla_p, hg_p, ssm_p, conv_p = states_p
    pool_s, ssm_s, conv_s = states_s
    return (xs[0], xs[1].reshape(bs, ts, D_MODEL),
            pool_p, time_minor(pool_s),
            gla_p.reshape(DEPTH, bp, GLA_H, GLA_DK, GLA_DV), seq_first(new_gla_t, GLA_H, GLA_DK),
            hg_p.reshape(DEPTH, bp, HG_H, HG_DK, HG_DV), seq_first(new_hg_t, HG_H, HG_DK),
            ssm_p.reshape(DEPTH, bp, SSM_H, SSM_P, SSM_N), ssm_s.reshape(DEPTH, bs, SSM_H, SSM_P, SSM_N),
            conv_p, time_minor(conv_s))
```

```python
import functools

import jax
import jax.numpy as jnp
from jax import lax
from jax.experimental import pallas as pl
from jax.experimental.pallas import tpu as pltpu

F32 = jnp.float32
BF16 = jnp.bfloat16

D_MODEL = 1024
D_FF = 2816
DEPTH = 2
EPS = 1e-6
PAST_LEN = 16384

C_MIX = 256
POOL_WINDOWS = (2, 4, 8, 16)
POOL_GC = 64
POOL_PAST = 15
GLA_H, GLA_DK, GLA_DV, GLA_RANK, GLA_TAU = 4, 32, 64, 16, 16.0
HG_H, HG_DK, HG_DV = 4, 64, 64
SSM_H, SSM_P, SSM_G, SSM_N, SSM_CONV = 4, 64, 2, 128, 4
SSM_CONV_DIM = 768
N_IN = 3092

OFF_PX, OFF_GQ, OFF_GK, OFF_GV, OFF_GR = 0, 256, 384, 512, 768
OFF_RQ, OFF_RF, OFF_RI, OFF_RG = 1024, 1280, 1536, 1792
OFF_SZ, OFF_XBC = 2048, 2304
OFF_LR = 3072
OFF_DTX = 3200
W_PROJ = 3456

PT_POOL_SCALE, PT_GBIAS, PT_GNORM, PT_HNORM, PT_CONVB, PT_DTB, PT_ALOG, PT_DSKIP, PT_SNORM = range(9)
PT_CONVW = 9
PT_LBL = 13
PT_ROWS = 16

VMEM_LIMIT_BYTES = 56 * 1024 * 1024
SUBLANES = 8
LANES = 128
HEADS_PER_STEP = LANES // GLA_DV

ROW_TILE = 512
FF_CHUNKS = (0, 768, 1536, 2304, D_FF)
PROMPT_CHUNK = 64
PROMPT_GROUP = 8
SAMPLE_GROUP = 16
SUB_BLOCK = 16
MAX_FACTORED_SPAN = 60.0
POOL_BASE = 24


def _sigmoid(x):
    return 1.0 / (1.0 + jnp.exp(-x))


def _silu(x):
    return x * _sigmoid(x)


def _log_sigmoid(x):
    return jnp.minimum(x, 0.0) - jnp.log(1.0 + jnp.exp(-jnp.abs(x)))


def _softplus(x):
    return jnp.maximum(x, 0.0) + jnp.log(1.0 + jnp.exp(-jnp.abs(x)))


def _rms(x, g):
    ms = jnp.mean(x * x, axis=-1, keepdims=True)
    return x * lax.rsqrt(ms + EPS) * g


def _dot(a, b):
    return jnp.dot(a.astype(BF16), b.astype(BF16), preferred_element_type=F32)


def _dot_nt(a, b):
    return lax.dot_general(a.astype(BF16), b.astype(BF16), (((1,), (1,)), ((), ())),
                           preferred_element_type=F32)


def _dot_tn(a, b):
    return lax.dot_general(a.astype(BF16), b.astype(BF16), (((0,), (0,)), ((), ())),
                           preferred_element_type=F32)


def _split3(x):
    hi = x.astype(BF16)
    r1 = x - hi.astype(F32)
    mid = r1.astype(BF16)
    lo = (r1 - mid.astype(F32)).astype(BF16)
    return hi, mid, lo


def _dot_exact_rhs(a01, x):
    hi, mid, lo = _split3(x)
    a = a01.astype(BF16)
    return (jnp.dot(a, hi, preferred_element_type=F32) + jnp.dot(a, mid, preferred_element_type=F32)
            + jnp.dot(a, lo, preferred_element_type=F32))


def _swiglu_half(x, g, wg_ref, wu_ref, wd_ref):
    h = _rms(x, g).astype(BF16)
    acc = None
    for lo, hi in zip(FF_CHUNKS[:-1], FF_CHUNKS[1:]):
        sl = slice(lo, hi)
        gate = jnp.dot(h, wg_ref[:, sl], preferred_element_type=F32)
        up = jnp.dot(h, wu_ref[:, sl], preferred_element_type=F32)
        act = (_silu(gate) * up).astype(BF16)
        d = jnp.dot(act, wd_ref[sl, :], preferred_element_type=F32)
        acc = d if acc is None else acc + d
    return x + 0.5 * acc


def _pre_kernel(*refs, n_first, split_in, layer):
    if split_in:
        x = jnp.where(pl.program_id(0) < n_first, refs[0][...].reshape(ROW_TILE, D_MODEL), refs[1][...])
        refs = refs[2:]
    else:
        x = refs[0][...]
        refs = refs[1:]
    g1_ref, wg_ref, wu_ref, wd_ref, gm_ref, win_ref, wgate_ref, ptab_ref, x1_ref, feat_ref = refs
    x1 = _swiglu_half(x, g1_ref[...], wg_ref, wu_ref, wd_ref)
    x1_ref[...] = x1
    h = _rms(x1, gm_ref[...]).astype(BF16)
    _mixer_features(h, win_ref, wgate_ref, ptab_ref, feat_ref, layer)


def _hgrn_lower_bound(ptab_ref, layer):
    lbl = ptab_ref[PT_LBL:PT_LBL + DEPTH, 0:C_MIX]
    lexp = jnp.exp(lbl - jnp.max(lbl, axis=0, keepdims=True))
    lsum = jnp.sum(lexp, axis=0, keepdims=True)
    lb = jnp.zeros((1, C_MIX), F32)
    for m in range(1, layer + 1):
        lb = lb + lexp[m:m + 1] / lsum
    return lb


def _mixer_features(h, win_ref, wgate_ref, ptab_ref, feat_ref, layer):
    prow = lambda r, w: ptab_ref[r:r + 1, 0:w]
    proj = lambda lo, hi: jnp.dot(h, win_ref[:, lo:hi], preferred_element_type=F32)
    d = proj(OFF_LR, W_PROJ)
    logit = _dot(d[:, 0:LANES], wgate_ref[...]) + prow(PT_GBIAS, LANES)
    feat_ref[:, OFF_LR:OFF_DTX] = _log_sigmoid(logit) * (1.0 / GLA_TAU)
    feat_ref[:, OFF_DTX:W_PROJ] = _softplus(d[:, LANES:LANES + C_MIX] + prow(PT_DTB, C_MIX))
    b = proj(OFF_RQ, OFF_XBC)
    feat_ref[:, OFF_RQ:OFF_RF] = _silu(b[:, 0:C_MIX])
    lb = _hgrn_lower_bound(ptab_ref, layer)
    log_lb = jnp.log(lb)
    t2 = jnp.log(1.0 - lb) + _log_sigmoid(b[:, C_MIX:2 * C_MIX])
    feat_ref[:, OFF_RF:OFF_RI] = jnp.maximum(log_lb, t2) + jnp.log(1.0 + jnp.exp(-jnp.abs(log_lb - t2)))
    feat_ref[:, OFF_RI:OFF_RG] = b[:, 2 * C_MIX:3 * C_MIX]
    feat_ref[:, OFF_RG:OFF_XBC] = _silu(b[:, 3 * C_MIX:5 * C_MIX])
    a = proj(OFF_PX, OFF_RQ)
    feat_ref[:, OFF_GR:OFF_RQ] = _silu(a[:, OFF_GR:OFF_RQ])
    feat_ref[:, OFF_GQ:OFF_GK] = a[:, OFF_GQ:OFF_GK] * (GLA_DK ** -0.5)
    feat_ref[:, OFF_PX:OFF_GQ] = a[:, OFF_PX:OFF_GQ]
    feat_ref[:, OFF_GK:OFF_GR] = a[:, OFF_GK:OFF_GR]
    feat_ref[:, OFF_XBC:OFF_LR] = proj(OFF_XBC, OFF_LR)


def _segment_rms(o, width, gain):
    n = o.shape[-1]
    seg = (lax.broadcasted_iota(jnp.int32, (n, n), 0) // width
           == lax.broadcasted_iota(jnp.int32, (n, n), 1) // width).astype(BF16)
    sq = o * o
    hi = sq.astype(BF16)
    lo = (sq - hi.astype(F32)).astype(BF16)
    ms = (jnp.dot(hi, seg, preferred_element_type=F32) + jnp.dot(lo, seg, preferred_element_type=F32)) * (1.0 / width)
    return o * lax.rsqrt(ms + EPS) * gain


def _mixer_outputs(cat_ref, ggla_ref, ghg_ref, gz_ref, ptab_ref):
    prow = lambda r, w: ptab_ref[r:r + 1, 0:w]
    o_gla = _segment_rms(cat_ref[:, C_MIX:2 * C_MIX], GLA_DV, prow(PT_GNORM, C_MIX)) * ggla_ref[...]
    o_hg = _segment_rms(cat_ref[:, 2 * C_MIX:3 * C_MIX], HG_DV, prow(PT_HNORM, C_MIX)) * ghg_ref[...]
    y = cat_ref[:, 3 * C_MIX:4 * C_MIX] * gz_ref[...]
    half = C_MIX // SSM_G
    o_ssm = [_rms(y[:, half * g:half * (g + 1)], ptab_ref[PT_SNORM:PT_SNORM + 1, half * g:half * (g + 1)])
             for g in range(SSM_G)]
    return jnp.concatenate([cat_ref[:, 0:C_MIX], o_gla, o_hg] + o_ssm, axis=1)


STAGE_SLOTS = 3
WIDE_CHUNK = 128
TALL_CHUNK = 352
WOUT_CHUNK = 256


def _row_chunks(src, dst, chunk_rows):
    return [(src.at[pl.ds(r, chunk_rows)], dst.at[pl.ds(r, chunk_rows)], chunk_rows)
            for r in range(0, src.shape[0], chunk_rows)]


def _stage_weights(streams):
    def slot(stream, c):
        stage, _, chunks = stream
        return stage.at[c % STAGE_SLOTS, pl.ds(0, chunks[c][2])]

    def copy(stream, c):
        return pltpu.make_async_copy(stream[2][c][0], slot(stream, c), stream[1].at[c % STAGE_SLOTS])

    for stream in streams:
        for c in range(min(STAGE_SLOTS, len(stream[2]))):
            copy(stream, c).start()
    for c in range(max(len(stream[2]) for stream in streams)):
        for stream in streams:
            if c < len(stream[2]):
                copy(stream, c).wait()
                stream[2][c][1][...] = slot(stream, c)[...].astype(BF16)
                if c + STAGE_SLOTS < len(stream[2]):
                    copy(stream, c + STAGE_SLOTS).start()


def _post_kernel(x_ref, cat_ref, ggla_ref, ghg_ref, gz_ref, ptab_ref, wout_hbm, g2_ref, wg_hbm, wu_hbm, wd_hbm,
                 gf_ref, *rest, n_first, final, layer):
    n_out = 2 if final else 1
    o_refs = rest[:n_out]
    wout_ref, wg_ref, wu_ref, wd_ref, wide_stage, tall_stage, wide_sems, tall_sems = rest[n_out:]

    @pl.when(pl.program_id(0) == 0)
    def _():
        wide = _row_chunks(wg_hbm.at[layer], wg_ref, WIDE_CHUNK) + _row_chunks(wu_hbm.at[layer], wu_ref, WIDE_CHUNK)
        tall = _row_chunks(wout_hbm.at[layer], wout_ref, WOUT_CHUNK) + _row_chunks(wd_hbm.at[layer], wd_ref, TALL_CHUNK)
        _stage_weights([(wide_stage, wide_sems, wide), (tall_stage, tall_sems, tall)])

    mixed = _mixer_outputs(cat_ref, ggla_ref, ghg_ref, gz_ref, ptab_ref)
    x2 = x_ref[...] + jnp.dot(mixed.astype(BF16), wout_ref[...], preferred_element_type=F32)
    x3 = _swiglu_half(x2, g2_ref[...], wg_ref, wu_ref, wd_ref)
    if final:
        y = _rms(x3, gf_ref[...])
        i = pl.program_id(0)

        @pl.when(i < n_first)
        def _():
            o_refs[0][...] = y.reshape(o_refs[0].shape)

        @pl.when(i >= n_first)
        def _():
            o_refs[1][...] = y
    else:
        o_refs[0][...] = x3


def _layer_spec(shape, layer):
    nd = len(shape)
    return pl.BlockSpec((None,) + tuple(shape), lambda *_: (layer,) + (0,) * nd, pipeline_mode=pl.Buffered(1))


def _dense_params():
    return pltpu.CompilerParams(dimension_semantics=("arbitrary",), vmem_limit_bytes=VMEM_LIMIT_BYTES)


def _row_spec(width):
    return pl.BlockSpec((ROW_TILE, width), lambda i: (i, 0))


def _token_specs(n_first, n_prompt_seq):
    return [pl.BlockSpec((n_prompt_seq, PROMPT_CHUNK, D_MODEL), lambda i: (0, jnp.minimum(i, n_first - 1), 0)),
            pl.BlockSpec((ROW_TILE, D_MODEL), lambda i: (jnp.maximum(i - n_first, 0), 0))]


def _pre_call(xs, g1, wg, wu, wd, gm, win, wgate, ptab, layer, rows, n_first):
    split_in = len(xs) == 2
    x_specs = _token_specs(n_first, xs[0].shape[0]) if split_in else [_row_spec(D_MODEL)]
    return pl.pallas_call(
        functools.partial(_pre_kernel, n_first=n_first, split_in=split_in, layer=layer),
        grid=(rows // ROW_TILE,),
        in_specs=x_specs + [_layer_spec((1, D_MODEL), layer), _layer_spec((D_MODEL, D_FF), layer),
                            _layer_spec((D_MODEL, D_FF), layer), _layer_spec((D_FF, D_MODEL), layer),
                            _layer_spec((1, D_MODEL), layer),
                            pl.BlockSpec((D_MODEL, W_PROJ), lambda i: (0, 0), pipeline_mode=pl.Buffered(1)),
                            _layer_spec(wgate.shape[1:], layer), _layer_spec(ptab.shape[1:], layer)],
        out_specs=[_row_spec(D_MODEL), _row_spec(W_PROJ)],
        out_shape=[jax.ShapeDtypeStruct((rows, D_MODEL), F32), jax.ShapeDtypeStruct((rows, W_PROJ), F32)],
        compiler_params=_dense_params(),
        name="pre_ffn_inproj",
    )(*xs, g1, wg, wu, wd, gm, win, wgate, ptab)


def _post_call(x, cat, feat, ptab, wout, g2, wg, wu, wd, gf, layer, n_first, final_shapes):
    rows = x.shape[0]
    if final_shapes is not None:
        out_specs = _token_specs(n_first, final_shapes[0][0])
        out_shape = [jax.ShapeDtypeStruct(s, F32) for s in final_shapes]
    else:
        out_specs = [_row_spec(D_MODEL)]
        out_shape = [jax.ShapeDtypeStruct((rows, D_MODEL), F32)]
    gate_spec = lambda off: pl.BlockSpec((ROW_TILE, C_MIX), lambda i: (i, off // C_MIX))
    hbm = pl.BlockSpec(memory_space=pl.ANY)
    return pl.pallas_call(
        functools.partial(_post_kernel, n_first=n_first, final=final_shapes is not None, layer=layer),
        grid=(rows // ROW_TILE,),
        in_specs=[_row_spec(D_MODEL), _row_spec(D_MODEL), gate_spec(OFF_GR), gate_spec(OFF_RG), gate_spec(OFF_SZ),
                  _layer_spec(ptab.shape[1:], layer), hbm, _layer_spec((1, D_MODEL), layer), hbm, hbm, hbm,
                  pl.BlockSpec((1, D_MODEL), lambda i: (0, 0))],
        out_specs=out_specs,
        out_shape=out_shape,
        scratch_shapes=[pltpu.VMEM((D_MODEL, D_MODEL), BF16), pltpu.VMEM((D_MODEL, D_FF), BF16),
                        pltpu.VMEM((D_MODEL, D_FF), BF16), pltpu.VMEM((D_FF, D_MODEL), BF16),
                        pltpu.VMEM((STAGE_SLOTS, WIDE_CHUNK, D_FF), F32),
                        pltpu.VMEM((STAGE_SLOTS, TALL_CHUNK, D_MODEL), F32),
                        pltpu.SemaphoreType.DMA((STAGE_SLOTS,)), pltpu.SemaphoreType.DMA((STAGE_SLOTS,))],
        compiler_params=_dense_params(),
        name="post_outproj_ffn",
    )(x, cat, feat, feat, feat, ptab, wout, g2, wg, wu, wd, gf)


def _lane_group(shape, width):
    return lax.broadcasted_iota(jnp.int32, shape, len(shape) - 1) // width


def _block_diag_mask(rows, row_w, cols, col_w):
    r = lax.broadcasted_iota(jnp.int32, (rows, cols), 0) // row_w
    c = lax.broadcasted_iota(jnp.int32, (rows, cols), 1) // col_w
    return r == c


def _tril(n):
    r = lax.broadcasted_iota(jnp.int32, (n, n), 0)
    c = lax.broadcasted_iota(jnp.int32, (n, n), 1)
    return r >= c


def _select_heads(stacked, n_heads, rows, head_w):
    grp = _lane_group((rows, n_heads * head_w), head_w)
    out = jnp.zeros((rows, n_heads * head_w), F32)
    for h in range(n_heads):
        out = jnp.where(grp == h, stacked[h * rows:(h + 1) * rows], out)
    return out


def _cum_log_decay(la):
    return _dot_exact_rhs(_tril(la.shape[0]).astype(F32), la)


def _decay_span(b):
    mid = b.shape[0] // 2 - 1
    dev = jnp.abs(b - b[mid:mid + 1, :])
    dev = functools.reduce(jnp.maximum, [dev[:, LANES * j:LANES * (j + 1)] for j in range(dev.shape[1] // LANES)])
    return jnp.max(dev.reshape(-1, SUBLANES, LANES), axis=0)


def _la_factored(q, k, v, b, st, n_heads, dk):
    chunk = q.shape[0]
    nk, nv = n_heads * dk, n_heads * 64
    mid = chunk // 2 - 1
    m = b[mid:mid + 1, :]
    qt = q * jnp.exp(b - m)
    kt = k * jnp.exp(m - b)
    k_bd = jnp.where(_block_diag_mask(n_heads * chunk, chunk, nk, dk), jnp.concatenate([kt] * n_heads, axis=0), 0.0)
    att = _dot_nt(qt, k_bd)
    row = lax.broadcasted_iota(jnp.int32, att.shape, 0)
    col = lax.broadcasted_iota(jnp.int32, att.shape, 1) % chunk
    att = jnp.where(row >= col, att, 0.0)
    v_bd = jnp.where(_block_diag_mask(n_heads * chunk, chunk, nv, 64), jnp.concatenate([v] * n_heads, axis=0), 0.0)
    o = _dot(att, v_bd) + _dot_nt(qt * jnp.exp(m), st)
    b_last = b[chunk - 1:chunk, :]
    upd = _dot_tn(v, kt * jnp.exp(b_last - m))
    st_new = st * jnp.exp(b_last) + jnp.where(_block_diag_mask(nv, 64, nk, dk), upd, 0.0)
    return o, st_new


def _la_direct(q, k, v, b, st, n_heads, dk, sub):
    chunk = q.shape[0]
    nk = n_heads * dk
    nv = n_heads * 64
    n_sub = chunk // sub
    refs = [None] + [b[sub * i - 1:sub * i, :] for i in range(1, n_sub)]
    c_loc = jnp.concatenate([b[0:sub]] + [b[sub * i:sub * (i + 1)] - refs[i] for i in range(1, n_sub)], axis=0) \
        if n_sub > 1 else b
    o = _dot_nt(q * jnp.exp(b), st)

    qt = q * jnp.exp(c_loc)
    kgrp = _lane_group((sub, nk), dk)
    outs = []
    for i in range(n_sub):
        rows = slice(sub * i, sub * (i + 1))
        o_i = o[rows]
        if i > 0:
            prev = slice(0, sub * i)
            kt = k[prev] * jnp.exp(refs[i] - b[prev])
            q_stack = jnp.concatenate([jnp.where(kgrp == h, qt[rows], 0.0) for h in range(n_heads)], axis=0)
            att = _dot_nt(q_stack, kt)
            o_i = o_i + _select_heads(_dot(att, v[prev]), n_heads, sub, 64)
        c_i = c_loc[rows]
        q_i = q[rows]
        row_id = lax.broadcasted_iota(jnp.int32, (sub, nk), 0)
        e_list = []
        for j in range(sub):
            c_j = c_loc[sub * i + j:sub * i + j + 1, :]
            k_j = k[sub * i + j:sub * i + j + 1, :]
            dec = jnp.exp(jnp.where(row_id >= j, c_i - c_j, -jnp.inf))
            e_list.append(dec * q_i * k_j)
        e_all = jnp.concatenate(e_list, axis=0)
        expand = _block_diag_mask(nk, dk, nv, 64).astype(F32)
        r_d = _dot(e_all, expand)
        for j in range(sub):
            o_i = o_i + r_d[j * sub:(j + 1) * sub] * v[sub * i + j:sub * i + j + 1, :]
        outs.append(o_i)
    o = jnp.concatenate(outs, axis=0) if n_sub > 1 else outs[0]

    b_last = b[chunk - 1:chunk, :]
    k_end = k * jnp.exp(b_last - b)
    upd = _dot_tn(v, k_end)
    st_new = st * jnp.exp(b_last) + jnp.where(_block_diag_mask(nv, 64, nk, dk), upd, 0.0)
    return o, st_new


def _load_state_t(s, n_heads, dk):
    st = s.T
    tiled = jnp.concatenate([st] * n_heads, axis=0)
    return jnp.where(_block_diag_mask(n_heads * 64, 64, n_heads * dk, dk), tiled, 0.0)


def _store_state_t(st, n_heads, dk):
    acc = st[0:64]
    for h in range(1, n_heads):
        acc = acc + st[64 * h:64 * (h + 1)]
    return acc.T


def _pool_mixer(proj_ref, pool_scr, poolw_ref, scale, chunk, pos_first):
    xp = proj_ref[:, OFF_PX:OFF_PX + C_MIX]
    pool_scr[0, POOL_BASE:POOL_BASE + chunk, :] = xp
    n_ext = POOL_BASE - SUBLANES + chunk
    for s, shift in enumerate((1, 2, 4)):
        cur = pool_scr[s, SUBLANES:SUBLANES + n_ext, :]
        prev = pool_scr[s, SUBLANES - shift:SUBLANES - shift + n_ext, :]
        pool_scr[s + 1, SUBLANES:SUBLANES + n_ext, :] = cur + prev
    s2 = pool_scr[1, POOL_BASE:POOL_BASE + chunk, :]
    s4 = pool_scr[2, POOL_BASE:POOL_BASE + chunk, :]
    s8 = pool_scr[3, POOL_BASE:POOL_BASE + chunk, :]
    s16 = s8 + pool_scr[3, POOL_BASE - 8:POOL_BASE - 8 + chunk, :]
    grp = _lane_group((chunk, C_MIX), POOL_GC)
    win_sum = jnp.where(grp == 0, s2, jnp.where(grp == 1, s4, jnp.where(grp == 2, s8, s16)))
    win = jnp.where(grp == 0, 2, jnp.where(grp == 1, 4, jnp.where(grp == 2, 8, 16)))
    pos = pos_first + lax.broadcasted_iota(jnp.int32, (chunk, C_MIX), 0)
    cnt = jnp.minimum(pos + 1, win).astype(F32)
    d = win_sum / cnt - xp
    o_pool = _dot(d, poolw_ref[...]) * scale
    new_pool = pool_scr[0, POOL_BASE + chunk - POOL_PAST:POOL_BASE + chunk, :]
    pool_scr[0, POOL_BASE - POOL_PAST:POOL_BASE, :] = new_pool
    return o_pool, new_pool


def _ssd_conv(proj_ref, conv_scr, prow, chunk):
    conv_base = SUBLANES
    conv_scr[conv_base:conv_base + chunk, :] = proj_ref[:, OFF_XBC:OFF_XBC + SSM_CONV_DIM]
    conv = None
    for w in range(SSM_CONV):
        start = conv_base - (SSM_CONV - 1) + w
        term = conv_scr[start:start + chunk, :] * prow(PT_CONVW + w, SSM_CONV_DIM)
        conv = term if conv is None else conv + term
    conv = _silu(conv + prow(PT_CONVB, SSM_CONV_DIM))
    new_conv = conv_scr[conv_base + chunk - (SSM_CONV - 1):conv_base + chunk, :]
    conv_scr[conv_base - (SSM_CONV - 1):conv_base, :] = new_conv
    return conv, new_conv


def _ssd_scan(proj_ref, conv, ssm_st, prow, chunk):
    xs = conv[:, 0:C_MIX]
    dt = proj_ref[:, OFF_DTX:OFF_DTX + C_MIX]
    a_neg = -jnp.exp(prow(PT_ALOG, C_MIX))
    la = dt * a_neg
    xdt = xs * dt
    cum = _cum_log_decay(la)
    heads_per_group = SSM_H // SSM_G
    b_of = lambda g: conv[:, C_MIX + SSM_N * g:C_MIX + SSM_N * (g + 1)]
    c_of = lambda g: conv[:, C_MIX + SSM_G * SSM_N + SSM_N * g:C_MIX + SSM_G * SSM_N + SSM_N * (g + 1)]
    if chunk == SSM_P:
        row = lax.broadcasted_iota(jnp.int32, (chunk, C_MIX), 0)
        col = lax.broadcasted_iota(jnp.int32, (chunk, C_MIX), 1) % chunk
        cum_j = jnp.sum(jnp.where(row == col, cum, 0.0), axis=0, keepdims=True)
        dec = jnp.exp(jnp.where(row >= col, cum - cum_j, -jnp.inf))
        cb = jnp.concatenate([_dot_nt(c_of(g), jnp.concatenate([b_of(g)] * heads_per_group, axis=0))
                              for g in range(SSM_G)], axis=1)
        x_bd = jnp.where(_block_diag_mask(SSM_H * chunk, chunk, C_MIX, SSM_P),
                         jnp.concatenate([xdt] * SSM_H, axis=0), 0.0)
        y = _dot(cb * dec, x_bd)
    else:
        cum_t = cum.T
        causal = _tril(chunk)
        att_rows = []
        for h in range(SSM_H):
            if h % heads_per_group == 0:
                cb = _dot_nt(c_of(h // heads_per_group), b_of(h // heads_per_group))
            col = cum[:, SSM_P * h:SSM_P * h + 1]
            row = cum_t[SSM_P * h:SSM_P * h + 1, :]
            att_rows.append(cb * jnp.exp(jnp.where(causal, col - row, -jnp.inf)))
        y = _select_heads(_dot(jnp.concatenate(att_rows, axis=0), xdt), SSM_H, chunk, SSM_P)
    cum_last = cum[chunk - 1:chunk, :]
    decay_in = jnp.exp(cum)
    xw = xdt * jnp.exp(cum_last - cum)
    decay_state = jnp.exp(cum_last)
    y_state = []
    for g in range(SSM_G):
        lanes = slice(SSM_N * g, SSM_N * (g + 1))
        st = ssm_st[g]
        y_state.append(_dot(c_of(g), st))
        ssm_st[g] = st * decay_state[:, lanes] + _dot_tn(b_of(g), xw[:, lanes])
    return y + jnp.concatenate(y_state, axis=1) * decay_in + prow(PT_DSKIP, C_MIX) * xs


def _mixer_kernel(*refs, chunk, group, pos0, layer, carried, n_alias, lin):
    feat_all = refs[0]
    n_state = 5 if lin else 3
    n_in = 1 + (n_state if carried else 0)
    poolw_ref, ptab_ref = refs[n_in:n_in + 2]
    n_in += 2 + n_alias
    cat_all = refs[n_in]
    new_states = refs[n_in + 1:n_in + 1 + n_state]
    scratch = refs[n_in + 1 + n_state:]
    if lin:
        npool_ref, ngla_ref, nhg_ref, nssm_ref, nconv_ref = new_states
        pool_scr, conv_scr, gla_st, hg_st, ssm_st = scratch
    else:
        npool_ref, nssm_ref, nconv_ref = new_states
        pool_scr, conv_scr, ssm_st = scratch
    seq_state = (lambda ref, s: ref.at[s]) if lin else (lambda ref, s: ref.at[:, s])
    prow = lambda r, w: ptab_ref[r:r + 1, 0:w]
    c = pl.program_id(1)
    n_c = pl.num_programs(1)
    sub = min(SUB_BLOCK, chunk)
    conv_base = SUBLANES

    @pl.when(c == 0)
    def _():
        pool_scr[...] = jnp.zeros(pool_scr.shape, F32)
        conv_scr[...] = jnp.zeros(conv_scr.shape, F32)
        if carried:
            carried_refs = refs[1:1 + n_state]
            poolb_ref, ssms_ref, convb_ref = carried_refs[0], carried_refs[-2], carried_refs[-1]
            for s in range(group):
                pool_scr[s, 0, POOL_BASE - POOL_PAST:POOL_BASE, :] = seq_state(poolb_ref, s)[...]
                conv_scr[s, conv_base - (SSM_CONV - 1):conv_base, :] = seq_state(convb_ref, s)[...]
                for g in range(SSM_G):
                    ssm_st[s, g] = ssms_ref[s, g].T
                if lin:
                    gla_st[s] = _load_state_t(carried_refs[1][s], GLA_H, GLA_DK)
                    hg_st[s] = _load_state_t(carried_refs[2][s], HG_H, HG_DK)
        else:
            ssm_st[...] = jnp.zeros(ssm_st.shape, F32)
            if lin:
                gla_st[...] = jnp.zeros(gla_st.shape, F32)
                hg_st[...] = jnp.zeros(hg_st.shape, F32)

    seq_rows = lambda ref, s: ref.at[pl.ds(s * chunk, chunk)]
    new_pools, new_convs, convs, cum_decays, spans = [], [], [], [], []

    def conv_stage(s):
        conv, new_conv = _ssd_conv(seq_rows(feat_all, s), conv_scr.at[s], prow, chunk)
        convs.append(conv)
        new_convs.append(new_conv)

    def decay_stage(s):
        feat_ref = seq_rows(feat_all, s)
        b_gla = _cum_log_decay(feat_ref[:, OFF_LR:OFF_LR + GLA_H * GLA_DK])
        b_hg = _cum_log_decay(feat_ref[:, OFF_RF:OFF_RF + C_MIX])
        cum_decays.extend([b_gla, b_hg])
        spans.extend([_decay_span(b_gla), _decay_span(b_hg)])

    def scan_stage(s):
        seq_rows(cat_all, s)[:, 3 * C_MIX:4 * C_MIX] = _ssd_scan(seq_rows(feat_all, s), convs[s], ssm_st.at[s], prow,
                                                                chunk)

    def pool_stage(s):
        o_pool, new_pool = _pool_mixer(seq_rows(feat_all, s), pool_scr.at[s], poolw_ref, prow(PT_POOL_SCALE, C_MIX),
                                       chunk, pos0 + c * chunk)
        seq_rows(cat_all, s)[:, 0:C_MIX] = o_pool
        new_pools.append(new_pool)

    stages = (decay_stage, conv_stage, scan_stage, pool_stage) if lin else (conv_stage, scan_stage, pool_stage)
    if chunk >= PROMPT_CHUNK:
        for stage in stages:
            for s in range(group):
                stage(s)
    else:
        for s in range(group):
            for stage in stages:
                stage(s)

    if lin:
        _linear_attention_step(feat_all, cat_all, gla_st, hg_st, cum_decays, spans, seq_rows, group, sub)
    else:
        for s in range(group):
            seq_rows(cat_all, s)[:, C_MIX:3 * C_MIX] = jnp.zeros((chunk, 2 * C_MIX), F32)

    @pl.when(c == n_c - 1)
    def _():
        for s in range(group):
            seq_state(npool_ref, s)[...] = new_pools[s]
            seq_state(nconv_ref, s)[...] = new_convs[s]
            for g in range(SSM_G):
                nssm_ref[s, g] = ssm_st[s, g].T
            if lin:
                ngla_ref[s] = _store_state_t(gla_st[s], GLA_H, GLA_DK)
                nhg_ref[s] = _store_state_t(hg_st[s], HG_H, HG_DK)


def _linear_attention_step(feat_all, cat_all, gla_st, hg_st, cum_decays, spans, seq_rows, group, sub):
    narrow = jnp.max(functools.reduce(jnp.maximum, spans)) <= MAX_FACTORED_SPAN

    def run(la_gla, la_hg, *cums):
        for s in range(group):
            feat_ref, cat_ref = seq_rows(feat_all, s), seq_rows(cat_all, s)
            o_g, st_g = la_gla(feat_ref[:, OFF_GQ:OFF_GK], feat_ref[:, OFF_GK:OFF_GV], feat_ref[:, OFF_GV:OFF_GR],
                               cums[2 * s], gla_st[s])
            gla_st[s] = st_g
            cat_ref[:, C_MIX:2 * C_MIX] = o_g
            hk = 1.0 - jnp.exp(feat_ref[:, OFF_RF:OFF_RI])
            o_h, st_h = la_hg(feat_ref[:, OFF_RQ:OFF_RF], hk, feat_ref[:, OFF_RI:OFF_RG], cums[2 * s + 1], hg_st[s])
            hg_st[s] = st_h
            cat_ref[:, 2 * C_MIX:3 * C_MIX] = o_h

    factored = functools.partial(run, functools.partial(_la_factored, n_heads=GLA_H, dk=GLA_DK),
                                 functools.partial(_la_factored, n_heads=HG_H, dk=HG_DK))
    direct = functools.partial(run, functools.partial(_la_direct, n_heads=GLA_H, dk=GLA_DK, sub=sub),
                               functools.partial(_la_direct, n_heads=HG_H, dk=HG_DK, sub=sub))
    lax.cond(narrow, factored, direct, *cum_decays)


def _mixer_call(proj, cat_prev, carried, prev_states, mats, *, n_seq, seq_len, row0, chunk_major, chunk, group,
                pos0, layer, lin):
    rows = proj.shape[0]
    n_c, n_b, blk = seq_len // chunk, n_seq // group, group * chunk
    assert chunk_major or n_c == 1
    row_map = lambda b, c: (row0 // blk + c * n_b + b, 0)

    def seq_major(tail):
        nd = len(tail)
        return (DEPTH, n_seq) + tail, pl.BlockSpec((None, group) + tail, lambda b, c: (layer, b) + (0,) * nd)

    def time_major(n_rows, width):
        return (DEPTH, n_rows, n_seq, width), pl.BlockSpec((None, n_rows, group, width), lambda b, c: (layer, 0, b, 0))

    ssm = seq_major((SSM_G, 128, SSM_N))
    if lin:
        states = [seq_major((POOL_PAST, C_MIX)), seq_major((GLA_H * GLA_DK, GLA_DV)), seq_major((HG_H * HG_DK, HG_DV)),
                  ssm, seq_major((SSM_CONV - 1, SSM_CONV_DIM))]
    else:
        states = [time_major(POOL_PAST, C_MIX), ssm, time_major(SSM_CONV - 1, SSM_CONV_DIM)]

    def mat_spec(m):
        nd = m.ndim - 1
        return pl.BlockSpec((None,) + m.shape[1:], lambda b, c: (layer,) + (0,) * nd)

    state_specs = [spec for _, spec in states]
    in_specs = [pl.BlockSpec((blk, W_PROJ), row_map)]
    args = [proj]
    if carried is not None:
        in_specs += state_specs
        args += list(carried)
    in_specs += [mat_spec(m) for m in mats]
    args += list(mats)
    aliases = {}
    if cat_prev is not None:
        aliases[len(args)] = 0
        args.append(cat_prev)
    if prev_states is not None:
        for i, s in enumerate(prev_states):
            aliases[len(args)] = 1 + i
            args.append(s)
    in_specs += [pl.BlockSpec(memory_space=pl.ANY)] * len(aliases)
    out_shape = [jax.ShapeDtypeStruct((rows, D_MODEL), F32)] + [jax.ShapeDtypeStruct(shape, F32) for shape, _ in states]
    out_specs = [pl.BlockSpec((blk, D_MODEL), row_map)] + state_specs
    scratch = [pltpu.VMEM((group, 4, POOL_BASE + chunk, C_MIX), F32),
               pltpu.VMEM((group, SUBLANES + chunk, SSM_CONV_DIM), F32)]
    if lin:
        scratch += [pltpu.VMEM((group, GLA_H * GLA_DV, GLA_H * GLA_DK), F32),
                    pltpu.VMEM((group, HG_H * HG_DV, HG_H * HG_DK), F32)]
    scratch.append(pltpu.VMEM((group, SSM_G, SSM_N, 128), F32))
    return pl.pallas_call(
        functools.partial(_mixer_kernel, chunk=chunk, group=group, pos0=pos0, layer=layer,
                          carried=carried is not None, n_alias=len(aliases), lin=lin),
        grid=(n_b, n_c),
        in_specs=in_specs,
        out_specs=out_specs,
        out_shape=out_shape,
        scratch_shapes=scratch,
        input_output_aliases=aliases,
        compiler_params=pltpu.CompilerParams(dimension_semantics=("arbitrary", "arbitrary"),
                                             vmem_limit_bytes=VMEM_LIMIT_BYTES),
        name=f"mixer_chunk{chunk}",
    )(*args)


def _batched_recurrence_kernel(*refs, n_seq, seq_len, dk, own_keys, n_alias):
    n_in = 5 if own_keys else 4
    q_ref, v_ref, la_ref, s_in = refs[0], refs[n_in - 3], refs[n_in - 2], refs[n_in - 1]
    o_ref, s_out, q_t, k_t, a_t, v_t = refs[n_in + n_alias:]
    key_w = HEADS_PER_STEP * dk
    base = 0 if key_w == q_ref.shape[1] else pl.multiple_of(pl.program_id(0) * key_w, key_w)
    for t in range(seq_len):
        s_prev = s_in if t == 0 else s_out
        rows = pl.ds(t, n_seq, stride=seq_len)
        a = jnp.exp(la_ref[rows, :])
        k = refs[1][rows, :] if own_keys else 1.0 - a
        q_t[...] = q_ref[rows, :].T
        k_t[...] = k.T
        a_t[...] = a.T
        v_t[...] = v_ref[rows, :].T
        heads = []
        for h in range(HEADS_PER_STEP):
            v_h = v_t[GLA_DV * h:GLA_DV * (h + 1), :]

            def body(i, acc, h=h, v_h=v_h, s_prev=s_prev):
                r = h * dk + i
                s = a_t[pl.ds(base + r, 1), :] * s_prev[r] + k_t[pl.ds(base + r, 1), :] * v_h
                s_out[r] = s
                return acc + q_t[pl.ds(base + r, 1), :] * s

            heads.append(lax.fori_loop(0, dk, body, jnp.zeros((GLA_DV, n_seq), F32), unroll=8))
        o_ref[rows, :] = jnp.concatenate(heads, axis=0).T


def _batched_recurrence_call(feat, cat, state_t, prev_state, *, layer, row0, n_seq, seq_len, q_off, k_off, v_off,
                             la_off, out_off, n_heads, dk):
    n_rows = n_seq * seq_len
    rb = row0 // n_rows
    key_w = HEADS_PER_STEP * dk
    assert key_w in (LANES // 2, LANES) and n_heads % HEADS_PER_STEP == 0

    def lanes(off, per_step):
        return pl.BlockSpec((n_rows, LANES), lambda i: (rb, off // LANES + (i if per_step else 0)))

    key = lambda off: lanes(off, key_w == LANES)
    in_specs = [key(q_off)] + ([key(k_off)] if k_off is not None else []) + [lanes(v_off, True), key(la_off)]
    args = [feat] * len(in_specs)
    state_spec = pl.BlockSpec((None, key_w, GLA_DV, n_seq), lambda i: (layer, i, 0, 0))
    in_specs.append(state_spec)
    args.append(state_t)
    aliases = {len(args): 0}
    args.append(cat)
    if prev_state is not None:
        aliases[len(args)] = 1
        args.append(prev_state)
    in_specs += [pl.BlockSpec(memory_space=pl.ANY)] * len(aliases)
    return pl.pallas_call(
        functools.partial(_batched_recurrence_kernel, n_seq=n_seq, seq_len=seq_len, dk=dk,
                          own_keys=k_off is not None, n_alias=len(aliases)),
        grid=(n_heads // HEADS_PER_STEP,),
        in_specs=in_specs,
        out_specs=[lanes(out_off, True), state_spec],
        out_shape=[jax.ShapeDtypeStruct(cat.shape, F32), jax.ShapeDtypeStruct(state_t.shape, F32)],
        scratch_shapes=[pltpu.VMEM((LANES, n_seq), F32)] * 4,
        input_output_aliases=aliases,
        compiler_params=pltpu.CompilerParams(dimension_semantics=("arbitrary",), vmem_limit_bytes=VMEM_LIMIT_BYTES),
        name=f"batched_recurrence_dk{dk}",
    )(*args)


LR_SRC = 1024
DT_SRC = N_IN - SSM_H


def _arrange_w_in_kernel(wt_ref, *o_refs):
    def put(col, slab):
        for layer, o_ref in enumerate(o_refs):
            o_ref[:, col:col + LANES] = slab[:, layer, :].T.astype(BF16)

    for j in range(LR_SRC // LANES):
        put(LANES * j, wt_ref[LANES * j:LANES * (j + 1)])
    for j in range((OFF_LR - LR_SRC) // LANES):
        src = LR_SRC + GLA_RANK + LANES * j
        put(LR_SRC + LANES * j, wt_ref[src:src + LANES])
    row = lax.broadcasted_iota(jnp.int32, (LANES, DEPTH, D_MODEL), 0)
    dt_row = lambda h: jnp.broadcast_to(wt_ref[DT_SRC + h:DT_SRC + h + 1], (LANES, DEPTH, D_MODEL))
    tail = jnp.where(row < GLA_RANK, wt_ref[LR_SRC:LR_SRC + LANES], 0.0)
    for h in range(SSM_H):
        tail = jnp.where(row == GLA_RANK + h, dt_row(h), tail)
    put(OFF_LR, tail)
    for j in range(C_MIX // LANES):
        rep = jnp.zeros((LANES, DEPTH, D_MODEL), F32)
        for h in range(HEADS_PER_STEP * j, HEADS_PER_STEP * (j + 1)):
            rep = jnp.where(row // SSM_P == h - HEADS_PER_STEP * j, dt_row(h), rep)
        put(OFF_DTX + LANES * j, rep)


def _arrange_w_in(w):
    wt = jnp.transpose(w, (2, 0, 1))
    return pl.pallas_call(
        _arrange_w_in_kernel,
        grid=(1,),
        in_specs=[pl.BlockSpec(wt.shape, lambda i: (0, 0, 0), pipeline_mode=pl.Buffered(1))],
        out_specs=[pl.BlockSpec((D_MODEL, W_PROJ), lambda i: (0, 0))] * DEPTH,
        out_shape=[jax.ShapeDtypeStruct((D_MODEL, W_PROJ), BF16)] * DEPTH,
        compiler_params=pltpu.CompilerParams(dimension_semantics=("arbitrary",), vmem_limit_bytes=VMEM_LIMIT_BYTES),
        name="arrange_w_in",
    )(wt)


def _pool_block_diag(pw):
    eye = jnp.eye(len(POOL_WINDOWS), dtype=pw.dtype)
    return jnp.einsum('lgcd,gh->lgchd', pw, eye).reshape(pw.shape[0], C_MIX, C_MIX)


def _param_table(pool_scale, gla_gate_bias, gla_norm, hgrn_norm, ssm_conv_b, ssm_dt_bias, ssm_A_log, ssm_D,
                 ssm_norm, ssm_conv_w, hgrn_lb_logits):
    def rows(v):
        v = v.astype(F32).reshape(DEPTH, -1, v.shape[-1])
        return jnp.pad(v, ((0, 0), (0, 0), (0, SSM_CONV_DIM - v.shape[-1])))
    lbl = jnp.broadcast_to(hgrn_lb_logits[None], (DEPTH,) + hgrn_lb_logits.shape)
    parts = [rows(pool_scale), rows(gla_gate_bias), rows(jnp.tile(gla_norm, (1, GLA_H))),
             rows(jnp.tile(hgrn_norm, (1, HG_H))), rows(ssm_conv_b), rows(jnp.repeat(ssm_dt_bias, SSM_P, axis=-1)),
             rows(jnp.repeat(ssm_A_log, SSM_P, axis=-1)), rows(jnp.repeat(ssm_D, SSM_P, axis=-1)), rows(ssm_norm),
             rows(ssm_conv_w), rows(lbl)]
    tab = jnp.concatenate(parts, axis=1)
    return jnp.pad(tab, ((0, 0), (0, PT_ROWS - tab.shape[1]), (0, 0)))


def kernel(x_prompt, x_sample, state_pool, state_gla, state_hgrn, state_ssm, state_conv, ffn1_norm, ffn1_w_gate, ffn1_w_up, ffn1_w_down, mix_norm, w_in, pool_w, pool_scale, gla_w_gate, gla_gate_bias, gla_norm, hgrn_lb_logits, hgrn_norm, ssm_conv_w, ssm_conv_b, ssm_dt_bias, ssm_A_log, ssm_D, ssm_norm, w_out, ffn2_norm, ffn2_w_gate, ffn2_w_up, ffn2_w_down, final_norm):
    bp, tp, _ = x_prompt.shape
    bs, ts, _ = x_sample.shape
    rows_p, rows_s = bp * tp, bs * ts
    gain = lambda v: v.reshape(DEPTH, 1, D_MODEL).astype(F32)
    bf = lambda w: w.astype(BF16)

    wgate = bf(jnp.pad(gla_w_gate, ((0, 0), (0, 128 - GLA_RANK), (0, 0))))
    ptab = _param_table(pool_scale, gla_gate_bias, gla_norm, hgrn_norm, ssm_conv_b, ssm_dt_bias, ssm_A_log, ssm_D,
                        ssm_norm, ssm_conv_w, hgrn_lb_logits)
    mats = [bf(_pool_block_diag(pool_w)), ptab]
    seq_last = lambda s: jnp.transpose(s, (0, 2, 3, 4, 1)).reshape(DEPTH, -1, s.shape[-1], bs)
    carried = (jnp.transpose(state_pool, (0, 2, 1, 3)), state_ssm.reshape(DEPTH, bs, SSM_G, 128, SSM_N),
               jnp.transpose(state_conv, (0, 2, 1, 3)))
    gla_t, hg_t = seq_last(state_gla), seq_last(state_hgrn)
    ffn1 = (gain(ffn1_norm), bf(ffn1_w_gate), bf(ffn1_w_up), bf(ffn1_w_down))
    ffn2 = (gain(ffn2_norm), ffn2_w_gate, ffn2_w_up, ffn2_w_down)
    gm, win, wout = gain(mix_norm), _arrange_w_in(w_in), w_out
    gf = final_norm.reshape(1, D_MODEL).astype(F32)

    assert bp * PROMPT_CHUNK == ROW_TILE and rows_s % ROW_TILE == 0
    n_first = rows_p // ROW_TILE
    xs = [x_prompt, x_sample.reshape(rows_s, D_MODEL)]
    states_p = states_s = new_gla_t = new_hg_t = None
    for l in range(DEPTH):
        x1, feat = _pre_call(xs, *ffn1, gm, win[l], wgate, ptab, l, rows_p + rows_s, n_first)
        outs_p = _mixer_call(feat, None, None, states_p, mats, n_seq=bp, seq_len=tp, row0=0, chunk_major=True,
                             chunk=PROMPT_CHUNK, group=PROMPT_GROUP, pos0=0, layer=l, lin=True)
        outs_s = _mixer_call(feat, outs_p[0], carried, states_s, mats, n_seq=bs, seq_len=ts, row0=rows_p,
                             chunk_major=False, chunk=ts, group=SAMPLE_GROUP, pos0=PAST_LEN, layer=l, lin=False)
        states_p, states_s = outs_p[1:], outs_s[1:]
        sample = dict(layer=l, row0=rows_p, n_seq=bs, seq_len=ts)
        cat, new_gla_t = _batched_recurrence_call(feat, outs_s[0], gla_t, new_gla_t, q_off=OFF_GQ, k_off=OFF_GK,
                                                  v_off=OFF_GV, la_off=OFF_LR, out_off=C_MIX, n_heads=GLA_H,
                                                  dk=GLA_DK, **sample)
        cat, new_hg_t = _batched_recurrence_call(feat, cat, hg_t, new_hg_t, q_off=OFF_RQ, k_off=None, v_off=OFF_RI,
                                                 la_off=OFF_RF, out_off=2 * C_MIX, n_heads=HG_H, dk=HG_DK, **sample)
        final_shapes = ((bp, tp, D_MODEL), (rows_s, D_MODEL)) if l == DEPTH - 1 else None
        xs = _post_call(x1, cat, feat, ptab, wout, *ffn2, gf, l, n_first, final_shapes)

    seq_first = lambda s, h, dk: jnp.transpose(s.reshape(DEPTH, h, dk, s.shape[-2], bs), (0, 4, 1, 2, 3))
    time_minor = lambda s: jnp.transpose(s, (0, 2, 1, 3))
    pool_p, gla_p, hg_p, ssm_p, conv_p = states_p
    pool_s, ssm_s, conv_s = states_s
    return (xs[0], xs[1].reshape(bs, ts, D_MODEL),
            pool_p, time_minor(pool_s),
            gla_p.reshape(DEPTH, bp, GLA_H, GLA_DK, GLA_DV), seq_first(new_gla_t, GLA_H, GLA_DK),
            hg_p.reshape(DEPTH, bp, HG_H, HG_DK, HG_DV), seq_first(new_hg_t, HG_H, HG_DK),
            ssm_p.reshape(DEPTH, bp, SSM_H, SSM_P, SSM_N), ssm_s.reshape(DEPTH, bs, SSM_H, SSM_P, SSM_N),
            conv_p, time_minor(conv_s))
```

```python
import functools

import jax
import jax.numpy as jnp
from jax import lax
from jax.experimental import pallas as pl
from jax.experimental.pallas import tpu as pltpu

F32 = jnp.float32
BF16 = jnp.bfloat16

D_MODEL = 1024
D_FF = 2816
DEPTH = 2
EPS = 1e-6
PAST_LEN = 16384

C_MIX = 256
POOL_WINDOWS = (2, 4, 8, 16)
POOL_GC = 64
POOL_PAST = 15
GLA_H, GLA_DK, GLA_DV, GLA_RANK, GLA_TAU = 4, 32, 64, 16, 16.0
HG_H, HG_DK, HG_DV = 4, 64, 64
SSM_H, SSM_P, SSM_G, SSM_N, SSM_CONV = 4, 64, 2, 128, 4
SSM_CONV_DIM = 768
N_IN = 3092

OFF_PX, OFF_GQ, OFF_GK, OFF_GV, OFF_GR = 0, 256, 384, 512, 768
OFF_RQ, OFF_RF, OFF_RI, OFF_RG = 1024, 1280, 1536, 1792
OFF_SZ, OFF_XBC = 2048, 2304
OFF_LR = 3072
OFF_DTX = 3200
W_PROJ = 3456

PT_POOL_SCALE, PT_GBIAS, PT_GNORM, PT_HNORM, PT_CONVB, PT_DTB, PT_ALOG, PT_DSKIP, PT_SNORM = range(9)
PT_CONVW = 9
PT_LBL = 13
PT_ROWS = 16

VMEM_LIMIT_BYTES = 56 * 1024 * 1024
SUBLANES = 8
LANES = 128
HEADS_PER_STEP = LANES // GLA_DV

ROW_TILE = 512
FF_CHUNKS = (0, 768, 1536, 2304, D_FF)
PROMPT_CHUNK = 64
PROMPT_GROUP = 8
SAMPLE_GROUP = 16
SUB_BLOCK = 16
MAX_FACTORED_SPAN = 60.0
POOL_BASE = 24


def _sigmoid(x):
    return 1.0 / (1.0 + jnp.exp(-x))


def _silu(x):
    return x * _sigmoid(x)


def _log_sigmoid(x):
    return jnp.minimum(x, 0.0) - jnp.log(1.0 + jnp.exp(-jnp.abs(x)))


def _softplus(x):
    return jnp.maximum(x, 0.0) + jnp.log(1.0 + jnp.exp(-jnp.abs(x)))


def _rms(x, g):
    ms = jnp.mean(x * x, axis=-1, keepdims=True)
    return x * lax.rsqrt(ms + EPS) * g


def _dot(a, b):
    return jnp.dot(a.astype(BF16), b.astype(BF16), preferred_element_type=F32)


def _dot_nt(a, b):
    return lax.dot_general(a.astype(BF16), b.astype(BF16), (((1,), (1,)), ((), ())),
                           preferred_element_type=F32)


def _dot_tn(a, b):
    return lax.dot_general(a.astype(BF16), b.astype(BF16), (((0,), (0,)), ((), ())),
                           preferred_element_type=F32)


def _split3(x):
    hi = x.astype(BF16)
    r1 = x - hi.astype(F32)
    mid = r1.astype(BF16)
    lo = (r1 - mid.astype(F32)).astype(BF16)
    return hi, mid, lo


def _dot_exact_rhs(a01, x):
    hi, mid, lo = _split3(x)
    a = a01.astype(BF16)
    return (jnp.dot(a, hi, preferred_element_type=F32) + jnp.dot(a, mid, preferred_element_type=F32)
            + jnp.dot(a, lo, preferred_element_type=F32))


def _swiglu_half(x, g, wg_ref, wu_ref, wd_ref):
    h = _rms(x, g).astype(BF16)
    acc = None
    for lo, hi in zip(FF_CHUNKS[:-1], FF_CHUNKS[1:]):
        sl = slice(lo, hi)
        gate = jnp.dot(h, wg_ref[:, sl], preferred_element_type=F32)
        up = jnp.dot(h, wu_ref[:, sl], preferred_element_type=F32)
        act = (_silu(gate) * up).astype(BF16)
        d = jnp.dot(act, wd_ref[sl, :], preferred_element_type=F32)
        acc = d if acc is None else acc + d
    return x + 0.5 * acc


def _pre_kernel(*refs, n_first, split_in, layer):
    if split_in:
        x = jnp.where(pl.program_id(0) < n_first, refs[0][...].reshape(ROW_TILE, D_MODEL), refs[1][...])
        refs = refs[2:]
    else:
        x = refs[0][...]
        refs = refs[1:]
    g1_ref, wg_ref, wu_ref, wd_ref, gm_ref, win_ref, wgate_ref, ptab_ref, x1_ref, feat_ref = refs
    x1 = _swiglu_half(x, g1_ref[...], wg_ref, wu_ref, wd_ref)
    x1_ref[...] = x1
    h = _rms(x1, gm_ref[...]).astype(BF16)
    _mixer_features(h, win_ref, wgate_ref, ptab_ref, feat_ref, layer)


def _hgrn_lower_bound(ptab_ref, layer):
    lbl = ptab_ref[PT_LBL:PT_LBL + DEPTH, 0:C_MIX]
    lexp = jnp.exp(lbl - jnp.max(lbl, axis=0, keepdims=True))
    lsum = jnp.sum(lexp, axis=0, keepdims=True)
    lb = jnp.zeros((1, C_MIX), F32)
    for m in range(1, layer + 1):
        lb = lb + lexp[m:m + 1] / lsum
    return lb


def _mixer_features(h, win_ref, wgate_ref, ptab_ref, feat_ref, layer):
    prow = lambda r, w: ptab_ref[r:r + 1, 0:w]
    proj = lambda lo, hi: jnp.dot(h, win_ref[:, lo:hi], preferred_element_type=F32)
    d = proj(OFF_LR, W_PROJ)
    logit = _dot(d[:, 0:LANES], wgate_ref[...]) + prow(PT_GBIAS, LANES)
    feat_ref[:, OFF_LR:OFF_DTX] = _log_sigmoid(logit) * (1.0 / GLA_TAU)
    feat_ref[:, OFF_DTX:W_PROJ] = _softplus(d[:, LANES:LANES + C_MIX] + prow(PT_DTB, C_MIX))
    b = proj(OFF_RQ, OFF_XBC)
    feat_ref[:, OFF_RQ:OFF_RF] = _silu(b[:, 0:C_MIX])
    lb = _hgrn_lower_bound(ptab_ref, layer)
    log_lb = jnp.log(lb)
    t2 = jnp.log(1.0 - lb) + _log_sigmoid(b[:, C_MIX:2 * C_MIX])
    feat_ref[:, OFF_RF:OFF_RI] = jnp.maximum(log_lb, t2) + jnp.log(1.0 + jnp.exp(-jnp.abs(log_lb - t2)))
    feat_ref[:, OFF_RI:OFF_RG] = b[:, 2 * C_MIX:3 * C_MIX]
    feat_ref[:, OFF_RG:OFF_XBC] = _silu(b[:, 3 * C_MIX:5 * C_MIX])
    a = proj(OFF_PX, OFF_RQ)
    feat_ref[:, OFF_GR:OFF_RQ] = _silu(a[:, OFF_GR:OFF_RQ])
    feat_ref[:, OFF_GQ:OFF_GK] = a[:, OFF_GQ:OFF_GK] * (GLA_DK ** -0.5)
    feat_ref[:, OFF_PX:OFF_GQ] = a[:, OFF_PX:OFF_GQ]
    feat_ref[:, OFF_GK:OFF_GR] = a[:, OFF_GK:OFF_GR]
    feat_ref[:, OFF_XBC:OFF_LR] = proj(OFF_XBC, OFF_LR)


def _segment_rms(o, width, gain):
    n = o.shape[-1]
    seg = (lax.broadcasted_iota(jnp.int32, (n, n), 0) // width
           == lax.broadcasted_iota(jnp.int32, (n, n), 1) // width).astype(BF16)
    sq = o * o
    hi = sq.astype(BF16)
    lo = (sq - hi.astype(F32)).astype(BF16)
    ms = (jnp.dot(hi, seg, preferred_element_type=F32) + jnp.dot(lo, seg, preferred_element_type=F32)) * (1.0 / width)
    return o * lax.rsqrt(ms + EPS) * gain


def _mixer_outputs(cat_ref, ggla_ref, ghg_ref, gz_ref, ptab_ref):
    prow = lambda r, w: ptab_ref[r:r + 1, 0:w]
    o_gla = _segment_rms(cat_ref[:, C_MIX:2 * C_MIX], GLA_DV, prow(PT_GNORM, C_MIX)) * ggla_ref[...]
    o_hg = _segment_rms(cat_ref[:, 2 * C_MIX:3 * C_MIX], HG_DV, prow(PT_HNORM, C_MIX)) * ghg_ref[...]
    y = cat_ref[:, 3 * C_MIX:4 * C_MIX] * gz_ref[...]
    half = C_MIX // SSM_G
    o_ssm = [_rms(y[:, half * g:half * (g + 1)], ptab_ref[PT_SNORM:PT_SNORM + 1, half * g:half * (g + 1)])
             for g in range(SSM_G)]
    return jnp.concatenate([cat_ref[:, 0:C_MIX], o_gla, o_hg] + o_ssm, axis=1)


STAGE_SLOTS = 3
WIDE_CHUNK = 128
TALL_CHUNK = 352
WOUT_CHUNK = 256


def _row_chunks(src, dst, chunk_rows):
    return [(src.at[pl.ds(r, chunk_rows)], dst.at[pl.ds(r, chunk_rows)], chunk_rows)
            for r in range(0, src.shape[0], chunk_rows)]


def _stage_weights(streams):
    def slot(stream, c):
        stage, _, chunks = stream
        return stage.at[c % STAGE_SLOTS, pl.ds(0, chunks[c][2])]

    def copy(stream, c):
        return pltpu.make_async_copy(stream[2][c][0], slot(stream, c), stream[1].at[c % STAGE_SLOTS])

    for i, stream in enumerate(streams):
        for c in range(min(STAGE_SLOTS, len(stream[2]))):
            copy(stream, c).start(priority=i % 2)
    for c in range(max(len(stream[2]) for stream in streams)):
        for i, stream in enumerate(streams):
            if c < len(stream[2]):
                copy(stream, c).wait()
                stream[2][c][1][...] = slot(stream, c)[...].astype(BF16)
                if c + STAGE_SLOTS < len(stream[2]):
                    copy(stream, c + STAGE_SLOTS).start(priority=i % 2)


def _post_kernel(x_ref, cat_ref, ggla_ref, ghg_ref, gz_ref, ptab_ref, wout_hbm, g2_ref, wg_hbm, wu_hbm, wd_hbm,
                 gf_ref, *rest, n_first, final, layer):
    n_out = 2 if final else 1
    o_refs = rest[:n_out]
    wout_ref, wg_ref, wu_ref, wd_ref, wide_stage, tall_stage, wide_sems, tall_sems = rest[n_out:]

    @pl.when(pl.program_id(0) == 0)
    def _():
        wide = _row_chunks(wg_hbm.at[layer], wg_ref, WIDE_CHUNK) + _row_chunks(wu_hbm.at[layer], wu_ref, WIDE_CHUNK)
        tall = _row_chunks(wout_hbm.at[layer], wout_ref, WOUT_CHUNK) + _row_chunks(wd_hbm.at[layer], wd_ref, TALL_CHUNK)
        _stage_weights([(wide_stage, wide_sems, wide), (tall_stage, tall_sems, tall)])

    mixed = _mixer_outputs(cat_ref, ggla_ref, ghg_ref, gz_ref, ptab_ref)
    x2 = x_ref[...] + jnp.dot(mixed.astype(BF16), wout_ref[...], preferred_element_type=F32)
    x3 = _swiglu_half(x2, g2_ref[...], wg_ref, wu_ref, wd_ref)
    if final:
        y = _rms(x3, gf_ref[...])
        i = pl.program_id(0)

        @pl.when(i < n_first)
        def _():
            o_refs[0][...] = y.reshape(o_refs[0].shape)

        @pl.when(i >= n_first)
        def _():
            o_refs[1][...] = y
    else:
        o_refs[0][...] = x3


def _layer_spec(shape, layer):
    nd = len(shape)
    return pl.BlockSpec((None,) + tuple(shape), lambda *_: (layer,) + (0,) * nd, pipeline_mode=pl.Buffered(1))


def _dense_params():
    return pltpu.CompilerParams(dimension_semantics=("arbitrary",), vmem_limit_bytes=VMEM_LIMIT_BYTES)


def _row_spec(width):
    return pl.BlockSpec((ROW_TILE, width), lambda i: (i, 0))


def _token_specs(n_first, n_prompt_seq):
    return [pl.BlockSpec((n_prompt_seq, PROMPT_CHUNK, D_MODEL), lambda i: (0, jnp.minimum(i, n_first - 1), 0)),
            pl.BlockSpec((ROW_TILE, D_MODEL), lambda i: (jnp.maximum(i - n_first, 0), 0))]


def _pre_call(xs, g1, wg, wu, wd, gm, win, wgate, ptab, layer, rows, n_first):
    split_in = len(xs) == 2
    x_specs = _token_specs(n_first, xs[0].shape[0]) if split_in else [_row_spec(D_MODEL)]
    return pl.pallas_call(
        functools.partial(_pre_kernel, n_first=n_first, split_in=split_in, layer=layer),
        grid=(rows // ROW_TILE,),
        in_specs=x_specs + [_layer_spec((1, D_MODEL), layer), _layer_spec((D_MODEL, D_FF), layer),
                            _layer_spec((D_MODEL, D_FF), layer), _layer_spec((D_FF, D_MODEL), layer),
                            _layer_spec((1, D_MODEL), layer),
                            pl.BlockSpec((D_MODEL, W_PROJ), lambda i: (0, 0), pipeline_mode=pl.Buffered(1)),
                            _layer_spec(wgate.shape[1:], layer), _layer_spec(ptab.shape[1:], layer)],
        out_specs=[_row_spec(D_MODEL), _row_spec(W_PROJ)],
        out_shape=[jax.ShapeDtypeStruct((rows, D_MODEL), F32), jax.ShapeDtypeStruct((rows, W_PROJ), F32)],
        compiler_params=_dense_params(),
        name="pre_ffn_inproj",
    )(*xs, g1, wg, wu, wd, gm, win, wgate, ptab)


def _post_call(x, cat, feat, ptab, wout, g2, wg, wu, wd, gf, layer, n_first, final_shapes):
    rows = x.shape[0]
    if final_shapes is not None:
        out_specs = _token_specs(n_first, final_shapes[0][0])
        out_shape = [jax.ShapeDtypeStruct(s, F32) for s in final_shapes]
    else:
        out_specs = [_row_spec(D_MODEL)]
        out_shape = [jax.ShapeDtypeStruct((rows, D_MODEL), F32)]
    gate_spec = lambda off: pl.BlockSpec((ROW_TILE, C_MIX), lambda i: (i, off // C_MIX))
    hbm = pl.BlockSpec(memory_space=pl.ANY)
    return pl.pallas_call(
        functools.partial(_post_kernel, n_first=n_first, final=final_shapes is not None, layer=layer),
        grid=(rows // ROW_TILE,),
        in_specs=[_row_spec(D_MODEL), _row_spec(D_MODEL), gate_spec(OFF_GR), gate_spec(OFF_RG), gate_spec(OFF_SZ),
                  _layer_spec(ptab.shape[1:], layer), hbm, _layer_spec((1, D_MODEL), layer), hbm, hbm, hbm,
                  pl.BlockSpec((1, D_MODEL), lambda i: (0, 0))],
        out_specs=out_specs,
        out_shape=out_shape,
        scratch_shapes=[pltpu.VMEM((D_MODEL, D_MODEL), BF16), pltpu.VMEM((D_MODEL, D_FF), BF16),
                        pltpu.VMEM((D_MODEL, D_FF), BF16), pltpu.VMEM((D_FF, D_MODEL), BF16),
                        pltpu.VMEM((STAGE_SLOTS, WIDE_CHUNK, D_FF), F32),
                        pltpu.VMEM((STAGE_SLOTS, TALL_CHUNK, D_MODEL), F32),
                        pltpu.SemaphoreType.DMA((STAGE_SLOTS,)), pltpu.SemaphoreType.DMA((STAGE_SLOTS,))],
        compiler_params=_dense_params(),
        name="post_outproj_ffn",
    )(x, cat, feat, feat, feat, ptab, wout, g2, wg, wu, wd, gf)


def _lane_group(shape, width):
    return lax.broadcasted_iota(jnp.int32, shape, len(shape) - 1) // width


def _block_diag_mask(rows, row_w, cols, col_w):
    r = lax.broadcasted_iota(jnp.int32, (rows, cols), 0) // row_w
    c = lax.broadcasted_iota(jnp.int32, (rows, cols), 1) // col_w
    return r == c


def _tril(n):
    r = lax.broadcasted_iota(jnp.int32, (n, n), 0)
    c = lax.broadcasted_iota(jnp.int32, (n, n), 1)
    return r >= c


def _select_heads(stacked, n_heads, rows, head_w):
    grp = _lane_group((rows, n_heads * head_w), head_w)
    out = jnp.zeros((rows, n_heads * head_w), F32)
    for h in range(n_heads):
        out = jnp.where(grp == h, stacked[h * rows:(h + 1) * rows], out)
    return out


def _cum_log_decay(la):
    return _dot_exact_rhs(_tril(la.shape[0]).astype(F32), la)


def _decay_span(b):
    mid = b.shape[0] // 2 - 1
    dev = jnp.abs(b - b[mid:mid + 1, :])
    dev = functools.reduce(jnp.maximum, [dev[:, LANES * j:LANES * (j + 1)] for j in range(dev.shape[1] // LANES)])
    return jnp.max(dev.reshape(-1, SUBLANES, LANES), axis=0)


def _la_factored(q, k, v, b, st, n_heads, dk):
    chunk = q.shape[0]
    nk, nv = n_heads * dk, n_heads * 64
    mid = chunk // 2 - 1
    m = b[mid:mid + 1, :]
    qt = q * jnp.exp(b - m)
    kt = k * jnp.exp(m - b)
    k_bd = jnp.where(_block_diag_mask(n_heads * chunk, chunk, nk, dk), jnp.concatenate([kt] * n_heads, axis=0), 0.0)
    att = _dot_nt(qt, k_bd)
    row = lax.broadcasted_iota(jnp.int32, att.shape, 0)
    col = lax.broadcasted_iota(jnp.int32, att.shape, 1) % chunk
    att = jnp.where(row >= col, att, 0.0)
    v_bd = jnp.where(_block_diag_mask(n_heads * chunk, chunk, nv, 64), jnp.concatenate([v] * n_heads, axis=0), 0.0)
    o = _dot(att, v_bd) + _dot_nt(qt * jnp.exp(m), st)
    b_last = b[chunk - 1:chunk, :]
    upd = _dot_tn(v, kt * jnp.exp(b_last - m))
    st_new = st * jnp.exp(b_last) + jnp.where(_block_diag_mask(nv, 64, nk, dk), upd, 0.0)
    return o, st_new


def _la_direct(q, k, v, b, st, n_heads, dk, sub):
    chunk = q.shape[0]
    nk = n_heads * dk
    nv = n_heads * 64
    n_sub = chunk // sub
    refs = [None] + [b[sub * i - 1:sub * i, :] for i in range(1, n_sub)]
    c_loc = jnp.concatenate([b[0:sub]] + [b[sub * i:sub * (i + 1)] - refs[i] for i in range(1, n_sub)], axis=0) \
        if n_sub > 1 else b
    o = _dot_nt(q * jnp.exp(b), st)

    qt = q * jnp.exp(c_loc)
    kgrp = _lane_group((sub, nk), dk)
    outs = []
    for i in range(n_sub):
        rows = slice(sub * i, sub * (i + 1))
        o_i = o[rows]
        if i > 0:
            prev = slice(0, sub * i)
            kt = k[prev] * jnp.exp(refs[i] - b[prev])
            q_stack = jnp.concatenate([jnp.where(kgrp == h, qt[rows], 0.0) for h in range(n_heads)], axis=0)
            att = _dot_nt(q_stack, kt)
            o_i = o_i + _select_heads(_dot(att, v[prev]), n_heads, sub, 64)
        c_i = c_loc[rows]
        q_i = q[rows]
        row_id = lax.broadcasted_iota(jnp.int32, (sub, nk), 0)
        e_list = []
        for j in range(sub):
            c_j = c_loc[sub * i + j:sub * i + j + 1, :]
            k_j = k[sub * i + j:sub * i + j + 1, :]
            dec = jnp.exp(jnp.where(row_id >= j, c_i - c_j, -jnp.inf))
            e_list.append(dec * q_i * k_j)
        e_all = jnp.concatenate(e_list, axis=0)
        expand = _block_diag_mask(nk, dk, nv, 64).astype(F32)
        r_d = _dot(e_all, expand)
        for j in range(sub):
            o_i = o_i + r_d[j * sub:(j + 1) * sub] * v[sub * i + j:sub * i + j + 1, :]
        outs.append(o_i)
    o = jnp.concatenate(outs, axis=0) if n_sub > 1 else outs[0]

    b_last = b[chunk - 1:chunk, :]
    k_end = k * jnp.exp(b_last - b)
    upd = _dot_tn(v, k_end)
    st_new = st * jnp.exp(b_last) + jnp.where(_block_diag_mask(nv, 64, nk, dk), upd, 0.0)
    return o, st_new


def _load_state_t(s, n_heads, dk):
    st = s.T
    tiled = jnp.concatenate([st] * n_heads, axis=0)
    return jnp.where(_block_diag_mask(n_heads * 64, 64, n_heads * dk, dk), tiled, 0.0)


def _store_state_t(st, n_heads, dk):
    acc = st[0:64]
    for h in range(1, n_heads):
        acc = acc + st[64 * h:64 * (h + 1)]
    return acc.T


def _pool_mixer(proj_ref, pool_scr, poolw_ref, scale, chunk, pos_first):
    xp = proj_ref[:, OFF_PX:OFF_PX + C_MIX]
    pool_scr[0, POOL_BASE:POOL_BASE + chunk, :] = xp
    n_ext = POOL_BASE - SUBLANES + chunk
    for s, shift in enumerate((1, 2, 4)):
        cur = pool_scr[s, SUBLANES:SUBLANES + n_ext, :]
        prev = pool_scr[s, SUBLANES - shift:SUBLANES - shift + n_ext, :]
        pool_scr[s + 1, SUBLANES:SUBLANES + n_ext, :] = cur + prev
    s2 = pool_scr[1, POOL_BASE:POOL_BASE + chunk, :]
    s4 = pool_scr[2, POOL_BASE:POOL_BASE + chunk, :]
    s8 = pool_scr[3, POOL_BASE:POOL_BASE + chunk, :]
    s16 = s8 + pool_scr[3, POOL_BASE - 8:POOL_BASE - 8 + chunk, :]
    grp = _lane_group((chunk, C_MIX), POOL_GC)
    win_sum = jnp.where(grp == 0, s2, jnp.where(grp == 1, s4, jnp.where(grp == 2, s8, s16)))
    win = jnp.where(grp == 0, 2, jnp.where(grp == 1, 4, jnp.where(grp == 2, 8, 16)))
    pos = pos_first + lax.broadcasted_iota(jnp.int32, (chunk, C_MIX), 0)
    cnt = jnp.minimum(pos + 1, win).astype(F32)
    d = win_sum / cnt - xp
    o_pool = _dot(d, poolw_ref[...]) * scale
    new_pool = pool_scr[0, POOL_BASE + chunk - POOL_PAST:POOL_BASE + chunk, :]
    pool_scr[0, POOL_BASE - POOL_PAST:POOL_BASE, :] = new_pool
    return o_pool, new_pool


def _ssd_conv(proj_ref, conv_scr, prow, chunk):
    conv_base = SUBLANES
    conv_scr[conv_base:conv_base + chunk, :] = proj_ref[:, OFF_XBC:OFF_XBC + SSM_CONV_DIM]
    conv = None
    for w in range(SSM_CONV):
        start = conv_base - (SSM_CONV - 1) + w
        term = conv_scr[start:start + chunk, :] * prow(PT_CONVW + w, SSM_CONV_DIM)
        conv = term if conv is None else conv + term
    conv = _silu(conv + prow(PT_CONVB, SSM_CONV_DIM))
    new_conv = conv_scr[conv_base + chunk - (SSM_CONV - 1):conv_base + chunk, :]
    conv_scr[conv_base - (SSM_CONV - 1):conv_base, :] = new_conv
    return conv, new_conv


def _ssd_scan(proj_ref, conv, ssm_st, prow, chunk):
    xs = conv[:, 0:C_MIX]
    dt = proj_ref[:, OFF_DTX:OFF_DTX + C_MIX]
    a_neg = -jnp.exp(prow(PT_ALOG, C_MIX))
    la = dt * a_neg
    xdt = xs * dt
    cum = _cum_log_decay(la)
    heads_per_group = SSM_H // SSM_G
    b_of = lambda g: conv[:, C_MIX + SSM_N * g:C_MIX + SSM_N * (g + 1)]
    c_of = lambda g: conv[:, C_MIX + SSM_G * SSM_N + SSM_N * g:C_MIX + SSM_G * SSM_N + SSM_N * (g + 1)]
    if chunk == SSM_P:
        row = lax.broadcasted_iota(jnp.int32, (chunk, C_MIX), 0)
        col = lax.broadcasted_iota(jnp.int32, (chunk, C_MIX), 1) % chunk
        cum_j = jnp.sum(jnp.where(row == col, cum, 0.0), axis=0, keepdims=True)
        dec = jnp.exp(jnp.where(row >= col, cum - cum_j, -jnp.inf))
        cb = jnp.concatenate([_dot_nt(c_of(g), jnp.concatenate([b_of(g)] * heads_per_group, axis=0))
                              for g in range(SSM_G)], axis=1)
        x_bd = jnp.where(_block_diag_mask(SSM_H * chunk, chunk, C_MIX, SSM_P),
                         jnp.concatenate([xdt] * SSM_H, axis=0), 0.0)
        y = _dot(cb * dec, x_bd)
    else:
        cum_t = cum.T
        causal = _tril(chunk)
        att_rows = []
        for h in range(SSM_H):
            if h % heads_per_group == 0:
                cb = _dot_nt(c_of(h // heads_per_group), b_of(h // heads_per_group))
            col = cum[:, SSM_P * h:SSM_P * h + 1]
            row = cum_t[SSM_P * h:SSM_P * h + 1, :]
            att_rows.append(cb * jnp.exp(jnp.where(causal, col - row, -jnp.inf)))
        y = _select_heads(_dot(jnp.concatenate(att_rows, axis=0), xdt), SSM_H, chunk, SSM_P)
    cum_last = cum[chunk - 1:chunk, :]
    decay_in = jnp.exp(cum)
    xw = xdt * jnp.exp(cum_last - cum)
    decay_state = jnp.exp(cum_last)
    y_state = []
    for g in range(SSM_G):
        lanes = slice(SSM_N * g, SSM_N * (g + 1))
        st = ssm_st[g]
        y_state.append(_dot(c_of(g), st))
        ssm_st[g] = st * decay_state[:, lanes] + _dot_tn(b_of(g), xw[:, lanes])
    return y + jnp.concatenate(y_state, axis=1) * decay_in + prow(PT_DSKIP, C_MIX) * xs


def _mixer_kernel(*refs, chunk, group, pos0, layer, carried, n_alias, lin):
    feat_all = refs[0]
    n_state = 5 if lin else 3
    n_in = 1 + (n_state if carried else 0)
    poolw_ref, ptab_ref = refs[n_in:n_in + 2]
    n_in += 2 + n_alias
    cat_all = refs[n_in]
    new_states = refs[n_in + 1:n_in + 1 + n_state]
    scratch = refs[n_in + 1 + n_state:]
    if lin:
        npool_ref, ngla_ref, nhg_ref, nssm_ref, nconv_ref = new_states
        pool_scr, conv_scr, gla_st, hg_st, ssm_st = scratch
    else:
        npool_ref, nssm_ref, nconv_ref = new_states
        pool_scr, conv_scr, ssm_st = scratch
    seq_state = (lambda ref, s: ref.at[s]) if lin else (lambda ref, s: ref.at[:, s])
    prow = lambda r, w: ptab_ref[r:r + 1, 0:w]
    c = pl.program_id(1)
    n_c = pl.num_programs(1)
    sub = min(SUB_BLOCK, chunk)
    conv_base = SUBLANES

    @pl.when(c == 0)
    def _():
        pool_scr[...] = jnp.zeros(pool_scr.shape, F32)
        conv_scr[...] = jnp.zeros(conv_scr.shape, F32)
        if carried:
            carried_refs = refs[1:1 + n_state]
            poolb_ref, ssms_ref, convb_ref = carried_refs[0], carried_refs[-2], carried_refs[-1]
            for s in range(group):
                pool_scr[s, 0, POOL_BASE - POOL_PAST:POOL_BASE, :] = seq_state(poolb_ref, s)[...]
                conv_scr[s, conv_base - (SSM_CONV - 1):conv_base, :] = seq_state(convb_ref, s)[...]
                for g in range(SSM_G):
                    ssm_st[s, g] = ssms_ref[s, g].T
                if lin:
                    gla_st[s] = _load_state_t(carried_refs[1][s], GLA_H, GLA_DK)
                    hg_st[s] = _load_state_t(carried_refs[2][s], HG_H, HG_DK)
        else:
            ssm_st[...] = jnp.zeros(ssm_st.shape, F32)
            if lin:
                gla_st[...] = jnp.zeros(gla_st.shape, F32)
                hg_st[...] = jnp.zeros(hg_st.shape, F32)

    seq_rows = lambda ref, s: ref.at[pl.ds(s * chunk, chunk)]
    new_pools, new_convs, convs, cum_decays, spans = [], [], [], [], []

    def conv_stage(s):
        conv, new_conv = _ssd_conv(seq_rows(feat_all, s), conv_scr.at[s], prow, chunk)
        convs.append(conv)
        new_convs.append(new_conv)

    def decay_stage(s):
        feat_ref = seq_rows(feat_all, s)
        b_gla = _cum_log_decay(feat_ref[:, OFF_LR:OFF_LR + GLA_H * GLA_DK])
        b_hg = _cum_log_decay(feat_ref[:, OFF_RF:OFF_RF + C_MIX])
        cum_decays.extend([b_gla, b_hg])
        spans.extend([_decay_span(b_gla), _decay_span(b_hg)])

    def scan_stage(s):
        seq_rows(cat_all, s)[:, 3 * C_MIX:4 * C_MIX] = _ssd_scan(seq_rows(feat_all, s), convs[s], ssm_st.at[s], prow,
                                                                chunk)

    def pool_stage(s):
        o_pool, new_pool = _pool_mixer(seq_rows(feat_all, s), pool_scr.at[s], poolw_ref, prow(PT_POOL_SCALE, C_MIX),
                                       chunk, pos0 + c * chunk)
        seq_rows(cat_all, s)[:, 0:C_MIX] = o_pool
        new_pools.append(new_pool)

    stages = (decay_stage, conv_stage, scan_stage, pool_stage) if lin else (conv_stage, scan_stage, pool_stage)
    if chunk >= PROMPT_CHUNK:
        for stage in stages:
            for s in range(group):
                stage(s)
    else:
        for s in range(group):
            for stage in stages:
                stage(s)

    if lin:
        _linear_attention_step(feat_all, cat_all, gla_st, hg_st, cum_decays, spans, seq_rows, group, sub)
    else:
        for s in range(group):
            seq_rows(cat_all, s)[:, C_MIX:3 * C_MIX] = jnp.zeros((chunk, 2 * C_MIX), F32)

    @pl.when(c == n_c - 1)
    def _():
        for s in range(group):
            seq_state(npool_ref, s)[...] = new_pools[s]
            seq_state(nconv_ref, s)[...] = new_convs[s]
            for g in range(SSM_G):
                nssm_ref[s, g] = ssm_st[s, g].T
            if lin:
                ngla_ref[s] = _store_state_t(gla_st[s], GLA_H, GLA_DK)
                nhg_ref[s] = _store_state_t(hg_st[s], HG_H, HG_DK)


def _linear_attention_step(feat_all, cat_all, gla_st, hg_st, cum_decays, spans, seq_rows, group, sub):
    narrow = jnp.max(functools.reduce(jnp.maximum, spans)) <= MAX_FACTORED_SPAN

    def run(la_gla, la_hg, *cums):
        for s in range(group):
            feat_ref, cat_ref = seq_rows(feat_all, s), seq_rows(cat_all, s)
            o_g, st_g = la_gla(feat_ref[:, OFF_GQ:OFF_GK], feat_ref[:, OFF_GK:OFF_GV], feat_ref[:, OFF_GV:OFF_GR],
                               cums[2 * s], gla_st[s])
            gla_st[s] = st_g
            cat_ref[:, C_MIX:2 * C_MIX] = o_g
            hk = 1.0 - jnp.exp(feat_ref[:, OFF_RF:OFF_RI])
            o_h, st_h = la_hg(feat_ref[:, OFF_RQ:OFF_RF], hk, feat_ref[:, OFF_RI:OFF_RG], cums[2 * s + 1], hg_st[s])
            hg_st[s] = st_h
            cat_ref[:, 2 * C_MIX:3 * C_MIX] = o_h

    factored = functools.partial(run, functools.partial(_la_factored, n_heads=GLA_H, dk=GLA_DK),
                                 functools.partial(_la_factored, n_heads=HG_H, dk=HG_DK))
    direct = functools.partial(run, functools.partial(_la_direct, n_heads=GLA_H, dk=GLA_DK, sub=sub),
                               functools.partial(_la_direct, n_heads=HG_H, dk=HG_DK, sub=sub))
    lax.cond(narrow, factored, direct, *cum_decays)


def _mixer_call(proj, cat_prev, carried, prev_states, mats, *, n_seq, seq_len, row0, chunk_major, chunk, group,
                pos0, layer, lin):
    rows = proj.shape[0]
    n_c, n_b, blk = seq_len // chunk, n_seq // group, group * chunk
    assert chunk_major or n_c == 1
    row_map = lambda b, c: (row0 // blk + c * n_b + b, 0)

    def seq_major(tail):
        nd = len(tail)
        return (DEPTH, n_seq) + tail, pl.BlockSpec((None, group) + tail, lambda b, c: (layer, b) + (0,) * nd)

    def time_major(n_rows, width):
        return (DEPTH, n_rows, n_seq, width), pl.BlockSpec((None, n_rows, group, width), lambda b, c: (layer, 0, b, 0))

    ssm = seq_major((SSM_G, 128, SSM_N))
    if lin:
        states = [seq_major((POOL_PAST, C_MIX)), seq_major((GLA_H * GLA_DK, GLA_DV)), seq_major((HG_H * HG_DK, HG_DV)),
                  ssm, seq_major((SSM_CONV - 1, SSM_CONV_DIM))]
    else:
        states = [time_major(POOL_PAST, C_MIX), ssm, time_major(SSM_CONV - 1, SSM_CONV_DIM)]

    def mat_spec(m):
        nd = m.ndim - 1
        return pl.BlockSpec((None,) + m.shape[1:], lambda b, c: (layer,) + (0,) * nd)

    state_specs = [spec for _, spec in states]
    in_specs = [pl.BlockSpec((blk, W_PROJ), row_map)]
    args = [proj]
    if carried is not None:
        in_specs += state_specs
        args += list(carried)
    in_specs += [mat_spec(m) for m in mats]
    args += list(mats)
    aliases = {}
    if cat_prev is not None:
        aliases[len(args)] = 0
        args.append(cat_prev)
    if prev_states is not None:
        for i, s in enumerate(prev_states):
            aliases[len(args)] = 1 + i
            args.append(s)
    in_specs += [pl.BlockSpec(memory_space=pl.ANY)] * len(aliases)
    out_shape = [jax.ShapeDtypeStruct((rows, D_MODEL), F32)] + [jax.ShapeDtypeStruct(shape, F32) for shape, _ in states]
    out_specs = [pl.BlockSpec((blk, D_MODEL), row_map)] + state_specs
    scratch = [pltpu.VMEM((group, 4, POOL_BASE + chunk, C_MIX), F32),
               pltpu.VMEM((group, SUBLANES + chunk, SSM_CONV_DIM), F32)]
    if lin:
        scratch += [pltpu.VMEM((group, GLA_H * GLA_DV, GLA_H * GLA_DK), F32),
                    pltpu.VMEM((group, HG_H * HG_DV, HG_H * HG_DK), F32)]
    scratch.append(pltpu.VMEM((group, SSM_G, SSM_N, 128), F32))
    return pl.pallas_call(
        functools.partial(_mixer_kernel, chunk=chunk, group=group, pos0=pos0, layer=layer,
                          carried=carried is not None, n_alias=len(aliases), lin=lin),
        grid=(n_b, n_c),
        in_specs=in_specs,
        out_specs=out_specs,
        out_shape=out_shape,
        scratch_shapes=scratch,
        input_output_aliases=aliases,
        compiler_params=pltpu.CompilerParams(dimension_semantics=("arbitrary", "arbitrary"),
                                             vmem_limit_bytes=VMEM_LIMIT_BYTES),
        name=f"mixer_chunk{chunk}",
    )(*args)


def _batched_recurrence_kernel(*refs, n_seq, seq_len, dk, own_keys, n_alias):
    n_in = 5 if own_keys else 4
    q_ref, v_ref, la_ref, s_in = refs[0], refs[n_in - 3], refs[n_in - 2], refs[n_in - 1]
    o_ref, s_out, q_t, k_t, a_t, v_t = refs[n_in + n_alias:]
    key_w = HEADS_PER_STEP * dk
    base = 0 if key_w == q_ref.shape[1] else pl.multiple_of(pl.program_id(0) * key_w, key_w)
    for t in range(seq_len):
        s_prev = s_in if t == 0 else s_out
        rows = pl.ds(t, n_seq, stride=seq_len)
        a = jnp.exp(la_ref[rows, :])
        k = refs[1][rows, :] if own_keys else 1.0 - a
        q_t[...] = q_ref[rows, :].T
        k_t[...] = k.T
        a_t[...] = a.T
        v_t[...] = v_ref[rows, :].T
        heads = []
        for h in range(HEADS_PER_STEP):
            v_h = v_t[GLA_DV * h:GLA_DV * (h + 1), :]

            def body(i, acc, h=h, v_h=v_h, s_prev=s_prev):
                r = h * dk + i
                s = a_t[pl.ds(base + r, 1), :] * s_prev[r] + k_t[pl.ds(base + r, 1), :] * v_h
                s_out[r] = s
                return acc + q_t[pl.ds(base + r, 1), :] * s

            heads.append(lax.fori_loop(0, dk, body, jnp.zeros((GLA_DV, n_seq), F32), unroll=8))
        o_ref[rows, :] = jnp.concatenate(heads, axis=0).T


def _batched_recurrence_call(feat, cat, state_t, prev_state, *, layer, row0, n_seq, seq_len, q_off, k_off, v_off,
                             la_off, out_off, n_heads, dk):
    n_rows = n_seq * seq_len
    rb = row0 // n_rows
    key_w = HEADS_PER_STEP * dk
    assert key_w in (LANES // 2, LANES) and n_heads % HEADS_PER_STEP == 0

    def lanes(off, per_step):
        return pl.BlockSpec((n_rows, LANES), lambda i: (rb, off // LANES + (i if per_step else 0)))

    key = lambda off: lanes(off, key_w == LANES)
    in_specs = [key(q_off)] + ([key(k_off)] if k_off is not None else []) + [lanes(v_off, True), key(la_off)]
    args = [feat] * len(in_specs)
    state_spec = pl.BlockSpec((None, key_w, GLA_DV, n_seq), lambda i: (layer, i, 0, 0))
    in_specs.append(state_spec)
    args.append(state_t)
    aliases = {len(args): 0}
    args.append(cat)
    if prev_state is not None:
        aliases[len(args)] = 1
        args.append(prev_state)
    in_specs += [pl.BlockSpec(memory_space=pl.ANY)] * len(aliases)
    return pl.pallas_call(
        functools.partial(_batched_recurrence_kernel, n_seq=n_seq, seq_len=seq_len, dk=dk,
                          own_keys=k_off is not None, n_alias=len(aliases)),
        grid=(n_heads // HEADS_PER_STEP,),
        in_specs=in_specs,
        out_specs=[lanes(out_off, True), state_spec],
        out_shape=[jax.ShapeDtypeStruct(cat.shape, F32), jax.ShapeDtypeStruct(state_t.shape, F32)],
        scratch_shapes=[pltpu.VMEM((LANES, n_seq), F32)] * 4,
        input_output_aliases=aliases,
        compiler_params=pltpu.CompilerParams(dimension_semantics=("arbitrary",), vmem_limit_bytes=VMEM_LIMIT_BYTES),
        name=f"batched_recurrence_dk{dk}",
    )(*args)


LR_SRC = 1024
DT_SRC = N_IN - SSM_H


def _arrange_w_in_kernel(wt_ref, *o_refs):
    def put(col, slab):
        for layer, o_ref in enumerate(o_refs):
            o_ref[:, col:col + LANES] = slab[:, layer, :].T.astype(BF16)

    for j in range(LR_SRC // LANES):
        put(LANES * j, wt_ref[LANES * j:LANES * (j + 1)])
    for j in range((OFF_LR - LR_SRC) // LANES):
        src = LR_SRC + GLA_RANK + LANES * j
        put(LR_SRC + LANES * j, wt_ref[src:src + LANES])
    row = lax.broadcasted_iota(jnp.int32, (LANES, DEPTH, D_MODEL), 0)
    dt_row = lambda h: jnp.broadcast_to(wt_ref[DT_SRC + h:DT_SRC + h + 1], (LANES, DEPTH, D_MODEL))
    tail = jnp.where(row < GLA_RANK, wt_ref[LR_SRC:LR_SRC + LANES], 0.0)
    for h in range(SSM_H):
        tail = jnp.where(row == GLA_RANK + h, dt_row(h), tail)
    put(OFF_LR, tail)
    for j in range(C_MIX // LANES):
        rep = jnp.zeros((LANES, DEPTH, D_MODEL), F32)
        for h in range(HEADS_PER_STEP * j, HEADS_PER_STEP * (j + 1)):
            rep = jnp.where(row // SSM_P == h - HEADS_PER_STEP * j, dt_row(h), rep)
        put(OFF_DTX + LANES * j, rep)


def _arrange_w_in(w):
    wt = jnp.transpose(w, (2, 0, 1))
    return pl.pallas_call(
        _arrange_w_in_kernel,
        grid=(1,),
        in_specs=[pl.BlockSpec(wt.shape, lambda i: (0, 0, 0), pipeline_mode=pl.Buffered(1))],
        out_specs=[pl.BlockSpec((D_MODEL, W_PROJ), lambda i: (0, 0))] * DEPTH,
        out_shape=[jax.ShapeDtypeStruct((D_MODEL, W_PROJ), BF16)] * DEPTH,
        compiler_params=pltpu.CompilerParams(dimension_semantics=("arbitrary",), vmem_limit_bytes=VMEM_LIMIT_BYTES),
        name="arrange_w_in",
    )(wt)


def _pool_block_diag(pw):
    eye = jnp.eye(len(POOL_WINDOWS), dtype=pw.dtype)
    return jnp.einsum('lgcd,gh->lgchd', pw, eye).reshape(pw.shape[0], C_MIX, C_MIX)


def _param_table(pool_scale, gla_gate_bias, gla_norm, hgrn_norm, ssm_conv_b, ssm_dt_bias, ssm_A_log, ssm_D,
                 ssm_norm, ssm_conv_w, hgrn_lb_logits):
    def rows(v):
        v = v.astype(F32).reshape(DEPTH, -1, v.shape[-1])
        return jnp.pad(v, ((0, 0), (0, 0), (0, SSM_CONV_DIM - v.shape[-1])))
    lbl = jnp.broadcast_to(hgrn_lb_logits[None], (DEPTH,) + hgrn_lb_logits.shape)
    parts = [rows(pool_scale), rows(gla_gate_bias), rows(jnp.tile(gla_norm, (1, GLA_H))),
             rows(jnp.tile(hgrn_norm, (1, HG_H))), rows(ssm_conv_b), rows(jnp.repeat(ssm_dt_bias, SSM_P, axis=-1)),
             rows(jnp.repeat(ssm_A_log, SSM_P, axis=-1)), rows(jnp.repeat(ssm_D, SSM_P, axis=-1)), rows(ssm_norm),
             rows(ssm_conv_w), rows(lbl)]
    tab = jnp.concatenate(parts, axis=1)
    return jnp.pad(tab, ((0, 0), (0, PT_ROWS - tab.shape[1]), (0, 0)))


def kernel(x_prompt, x_sample, state_pool, state_gla, state_hgrn, state_ssm, state_conv, ffn1_norm, ffn1_w_gate, ffn1_w_up, ffn1_w_down, mix_norm, w_in, pool_w, pool_scale, gla_w_gate, gla_gate_bias, gla_norm, hgrn_lb_logits, hgrn_norm, ssm_conv_w, ssm_conv_b, ssm_dt_bias, ssm_A_log, ssm_D, ssm_norm, w_out, ffn2_norm, ffn2_w_gate, ffn2_w_up, ffn2_w_down, final_norm):
    bp, tp, _ = x_prompt.shape
    bs, ts, _ = x_sample.shape
    rows_p, rows_s = bp * tp, bs * ts
    gain = lambda v: v.reshape(DEPTH, 1, D_MODEL).astype(F32)
    bf = lambda w: w.astype(BF16)

    wgate = bf(jnp.pad(gla_w_gate, ((0, 0), (0, 128 - GLA_RANK), (0, 0))))
    ptab = _param_table(pool_scale, gla_gate_bias, gla_norm, hgrn_norm, ssm_conv_b, ssm_dt_bias, ssm_A_log, ssm_D,
                        ssm_norm, ssm_conv_w, hgrn_lb_logits)
    mats = [bf(_pool_block_diag(pool_w)), ptab]
    seq_last = lambda s: jnp.transpose(s, (0, 2, 3, 4, 1)).reshape(DEPTH, -1, s.shape[-1], bs)
    carried = (jnp.transpose(state_pool, (0, 2, 1, 3)), state_ssm.reshape(DEPTH, bs, SSM_G, 128, SSM_N),
               jnp.transpose(state_conv, (0, 2, 1, 3)))
    gla_t, hg_t = seq_last(state_gla), seq_last(state_hgrn)
    ffn1 = (gain(ffn1_norm), bf(ffn1_w_gate), bf(ffn1_w_up), bf(ffn1_w_down))
    ffn2 = (gain(ffn2_norm), ffn2_w_gate, ffn2_w_up, ffn2_w_down)
    gm, win, wout = gain(mix_norm), _arrange_w_in(w_in), w_out
    gf = final_norm.reshape(1, D_MODEL).astype(F32)

    assert bp * PROMPT_CHUNK == ROW_TILE and rows_s % ROW_TILE == 0
    n_first = rows_p // ROW_TILE
    xs = [x_prompt, x_sample.reshape(rows_s, D_MODEL)]
    states_p = states_s = new_gla_t = new_hg_t = None
    for l in range(DEPTH):
        x1, feat = _pre_call(xs, *ffn1, gm, win[l], wgate, ptab, l, rows_p + rows_s, n_first)
        outs_p = _mixer_call(feat, None, None, states_p, mats, n_seq=bp, seq_len=tp, row0=0, chunk_major=True,
                             chunk=PROMPT_CHUNK, group=PROMPT_GROUP, pos0=0, layer=l, lin=True)
        outs_s = _mixer_call(feat, outs_p[0], carried, states_s, mats, n_seq=bs, seq_len=ts, row0=rows_p,
                             chunk_major=False, chunk=ts, group=SAMPLE_GROUP, pos0=PAST_LEN, layer=l, lin=False)
        states_p, states_s = outs_p[1:], outs_s[1:]
        sample = dict(layer=l, row0=rows_p, n_seq=bs, seq_len=ts)
        cat, new_gla_t = _batched_recurrence_call(feat, outs_s[0], gla_t, new_gla_t, q_off=OFF_GQ, k_off=OFF_GK,
                                                  v_off=OFF_GV, la_off=OFF_LR, out_off=C_MIX, n_heads=GLA_H,
                                                  dk=GLA_DK, **sample)
        cat, new_hg_t = _batched_recurrence_call(feat, cat, hg_t, new_hg_t, q_off=OFF_RQ, k_off=None, v_off=OFF_RI,
                                                 la_off=OFF_RF, out_off=2 * C_MIX, n_heads=HG_H, dk=HG_DK, **sample)
        final_shapes = ((bp, tp, D_MODEL), (rows_s, D_MODEL)) if l == DEPTH - 1 else None
        xs = _post_call(x1, cat, feat, ptab, wout, *ffn2, gf, l, n_first, final_shapes)

    seq_first = lambda s, h, dk: jnp.transpose(s.reshape(DEPTH, h, dk, s.shape[-2], bs), (0, 4, 1, 2, 3))
    time_minor = lambda s: jnp.transpose(s, (0, 2, 1, 3))
    pool_p, gla_p, hg_p, ssm_p, conv_p = states_p
    pool_s, ssm_s, conv_s = states_s
    return (xs[0], xs[1].reshape(bs, ts, D_MODEL),
            pool_p, time_minor(pool_s),
            gla_p.reshape(DEPTH, bp, GLA_H, GLA_DK, GLA_DV), seq_first(new_gla_t, GLA_H, GLA_DK),
            hg_p.reshape(DEPTH, bp, HG_H, HG_DK, HG_DV), seq_first(new_hg_t, HG_H, HG_DK),
            ssm_p.reshape(DEPTH, bp, SSM_H, SSM_P, SSM_N), ssm_s.reshape(DEPTH, bs, SSM_H, SSM_P, SSM_N),
            conv_p, time_minor(conv_s))
```
